```python
import math
import jax, jax.numpy as jnp
from jax import lax
import numpy as np

D_MODEL = 1024
BATCH = 4
SEQ = 8192
DEPTH = 4

CTX_LEN = 256
GRID_W = 64
HEAD_DIM = 64
NA_HEADS = 12
NA_WIDTH = NA_HEADS * HEAD_DIM
NA_WIN_ROWS = 8
NA_WIN_COLS = 16
FN_GROUPS = 4
FN_GROUP_DIM = 64
FN_WIDTH = FN_GROUPS * FN_GROUP_DIM
EVEN_IN_WIDTH = 3 * NA_WIDTH + FN_WIDTH
HY_SHORT = 3
HY_EMB = 33
HY_BANDS = (HY_EMB - 1) // 2
HY_FILTER_HIDDEN = 64
HY_DECAY_TARGET = 1e-2
HY_FAST_DECAY_PCT = 0.3
HY_SLOW_DECAY_PCT = 1.5
HY_MIN_DECAY = math.log(HY_DECAY_TARGET) / HY_SLOW_DECAY_PCT
HY_MAX_DECAY = math.log(HY_DECAY_TARGET) / HY_FAST_DECAY_PCT
HY_MOD_SHIFT = 0.0
N_EXPERTS = 16
EC_CAPACITY_FACTOR = 2
D_EXPERT = 2048
N_EVEN = (DEPTH + 1) // 2
N_ODD = DEPTH // 2
RMS_EPS = 1e-6

kernel_name = "hybrid_natten_fnet_hyena_ec_dit"


def rms_norm(x, g):
    xf = x.astype(jnp.float32)
    y = xf * lax.rsqrt(jnp.mean(xf * xf, axis=-1, keepdims=True) + RMS_EPS)
    return (y * g.astype(jnp.float32)).astype(x.dtype)


def modulate(h, shift, scale):
    return h * (1 + scale) + shift


def split_even(p):
    B, L, _ = p.shape
    q = p[..., :NA_WIDTH].reshape(B, L, NA_HEADS, HEAD_DIM)
    k = p[..., NA_WIDTH:2 * NA_WIDTH].reshape(B, L, NA_HEADS, HEAD_DIM)
    v = p[..., 2 * NA_WIDTH:3 * NA_WIDTH].reshape(B, L, NA_HEADS, HEAD_DIM)
    f = p[..., 3 * NA_WIDTH:]
    return q, k, v, f


def ctx_attention(q, k, v):
    B, Lc, H, hd = q.shape
    s = jnp.einsum('bqhd,bkhd->bhqk', q, k).astype(jnp.float32) * hd ** -0.5
    p = jax.nn.softmax(s, axis=-1).astype(v.dtype)
    return jnp.einsum('bhqk,bkhd->bqhd', p, v).reshape(B, Lc, H * hd)


def na_latent(q, k, v, k_ctx, v_ctx, rpb):
    B, L, H, hd = q.shape
    rows = L // GRID_W
    kr = min(NA_WIN_ROWS, rows)
    kw = NA_WIN_COLS
    scale = hd ** -0.5
    qg = q.reshape(B, rows, GRID_W, H, hd)
    kg = k.reshape(B, rows, GRID_W, H, hd)
    vg = v.reshape(B, rows, GRID_W, H, hd)
    cols = np.arange(GRID_W)
    col_start = np.clip(cols - kw // 2, 0, GRID_W - kw)
    col_idx = col_start[:, None] + np.arange(kw)[None, :]
    col_off = col_idx - cols[:, None] + (NA_WIN_COLS - 1)
    rpb_cols = rpb[:, :, col_off].astype(jnp.float32)

    def one_row(r):
        rs = jnp.clip(r - kr // 2, 0, rows - kr)
        k_rows = lax.dynamic_slice_in_dim(kg, rs, kr, axis=1)
        v_rows = lax.dynamic_slice_in_dim(vg, rs, kr, axis=1)
        k_win = k_rows[:, :, col_idx]
        v_win = v_rows[:, :, col_idx]
        q_row = lax.dynamic_index_in_dim(qg, r, axis=1, keepdims=False)
        s_loc = jnp.einsum('bqhd,brqwhd->bhqrw', q_row, k_win).astype(jnp.float32) * scale
        row_off = rs + jnp.arange(kr) - r + (NA_WIN_ROWS - 1)
        bias = jnp.take(rpb_cols, row_off, axis=1).transpose(0, 2, 1, 3)
        s_loc = (s_loc + bias[None]).reshape(B, H, GRID_W, kr * kw)
        s_ctx = jnp.einsum('bqhd,bkhd->bhqk', q_row, k_ctx).astype(jnp.float32) * scale
        p = jax.nn.softmax(jnp.concatenate([s_loc, s_ctx], axis=-1), axis=-1).astype(v.dtype)
        p_loc = p[..., :kr * kw].reshape(B, H, GRID_W, kr, kw)
        p_ctx = p[..., kr * kw:]
        return (jnp.einsum('bhqrw,brqwhd->bqhd', p_loc, v_win)
                + jnp.einsum('bhqk,bkhd->bqhd', p_ctx, v_ctx))

    out = lax.map(one_row, jnp.arange(rows))
    return out.transpose(1, 0, 2, 3, 4).reshape(B, L, H * hd)


def fourier_mix(u):
    B, L, _ = u.shape
    ug = u.astype(jnp.float32).reshape(B, L, FN_GROUPS, FN_GROUP_DIM)
    y = jnp.fft.fft2(ug, axes=(1, 3), norm='ortho').real
    return y.reshape(B, L, FN_WIDTH).astype(u.dtype)


def short_conv(u, w, b):
    C = u.shape[-1]
    pad = HY_SHORT // 2
    y = lax.conv_general_dilated(u, w[:, None, :].astype(u.dtype), window_strides=(1,),
                                 padding=((pad, pad),), dimension_numbers=('NWC', 'WIO', 'NWC'),
                                 feature_group_count=C)
    return y + b


def hyena_filters(L, w1, b1, freq, w2, b2, w3):
    f32 = jnp.float32
    t = jnp.linspace(0.0, 1.0, L, dtype=f32)[:, None]
    w = 2 * math.pi * jnp.arange(L, dtype=f32)[:, None] / L
    bands = jnp.linspace(1e-4, HY_BANDS - 1, HY_BANDS, dtype=f32)[None, :]
    z = jnp.concatenate([t, jnp.cos(bands * w), -jnp.sin(bands * w)], axis=-1)
    fr = freq.astype(f32)
    hid = jnp.sin(fr * (z @ w1.astype(f32) + b1.astype(f32)))
    hid = jnp.sin(fr * (hid @ w2.astype(f32) + b2.astype(f32)))
    h = (hid @ w3.astype(f32)).reshape(L, 2, D_MODEL)
    deltas = jnp.abs(jnp.linspace(HY_MIN_DECAY, HY_MAX_DECAY, D_MODEL, dtype=f32))
    window = jnp.exp(-t * deltas[None, :]) + HY_MOD_SHIFT
    h = h * window[:, None, :]
    filt = jnp.concatenate([h[:, 0], jnp.zeros((1, D_MODEL), f32), h[1:, 1][::-1]], axis=0)
    return filt / jnp.sum(jnp.abs(filt), axis=0, keepdims=True)


def long_conv(v, filt, skip):
    L = v.shape[1]
    vf = v.astype(jnp.float32)
    y = jnp.fft.irfft(jnp.fft.rfft(vf, n=2 * L, axis=1) * jnp.fft.rfft(filt, n=2 * L, axis=0)[None],
                      n=2 * L, axis=1)[:, :L]
    return (y + vf * skip.astype(jnp.float32)).astype(v.dtype)


def hyena_mixer(h, w_in, conv_w, conv_b, f_w1, f_b1, f_freq, f_w2, f_b2, f_w3, skip, w_out):
    u = short_conv(h @ w_in, conv_w, conv_b)
    x0, x1, v = jnp.split(u, 3, axis=-1)
    filt = hyena_filters(h.shape[1], f_w1, f_b1, f_freq, f_w2, f_b2, f_w3)
    y = x0 * long_conv(v * x1, filt, skip)
    return y @ w_out


def expert_choice_ffn(h, router_w, w_gate, w_up, w_down):
    B, n, D = h.shape
    cap = EC_CAPACITY_FACTOR * n // N_EXPERTS
    aff = jax.nn.softmax((h @ router_w).astype(jnp.float32), axis=-1)
    gate, idx = lax.top_k(jnp.swapaxes(aff, 1, 2), cap)
    xe = jax.vmap(lambda hb, ib: hb[ib])(h, idx)
    hid = jax.nn.silu(jnp.einsum('becd,edf->becf', xe, w_gate)) * jnp.einsum('becd,edf->becf', xe, w_up)
    ye = jnp.einsum('becf,efd->becd', hid, w_down) * gate[..., None].astype(h.dtype)
    flat = (idx + (jnp.arange(B, dtype=idx.dtype) * n)[:, None, None]).reshape(-1)
    out = jnp.zeros((B * n, D), h.dtype).at[flat].add(ye.reshape(-1, D))
    return out.reshape(B, n, D)


def setup_inputs(seed: int = 0) -> dict:
    key = jax.random.key(seed)
    kit = iter(jax.random.split(key, 27))
    D = D_MODEL

    def nrm(shape, s):
        return jax.random.normal(next(kit), shape, jnp.float32) * s

    return {
        "x": nrm((BATCH, SEQ, D), 1.0),
        "c": nrm((BATCH, D), 1.0),
        "ctx": nrm((BATCH, CTX_LEN, D), 1.0),
        "c_ctx": nrm((D,), 1.0),
        "mod_w": nrm((DEPTH, D, 6 * D), 0.5 * D ** -0.5),
        "mod_b": nrm((DEPTH, 6 * D), 0.02),
        "norm_mix_g": 1.0 + nrm((DEPTH, D), 0.02),
        "norm_ffn_g": 1.0 + nrm((DEPTH, D), 0.02),
        "mix_w_in": nrm((N_EVEN, D, EVEN_IN_WIDTH), D ** -0.5),
        "na_rpb": nrm((N_EVEN, NA_HEADS, 2 * NA_WIN_ROWS - 1, 2 * NA_WIN_COLS - 1), 0.1),
        "mix_w_out": nrm((N_EVEN, NA_WIDTH + FN_WIDTH, D), (NA_WIDTH + FN_WIDTH) ** -0.5),
        "hy_w_in": nrm((N_ODD, D, 3 * D), D ** -0.5),
        "hy_conv_w": nrm((N_ODD, HY_SHORT, 3 * D), HY_SHORT ** -0.5),
        "hy_conv_b": nrm((N_ODD, 3 * D), 0.02),
        "hy_f_w1": nrm((N_ODD, HY_EMB, HY_FILTER_HIDDEN), HY_EMB ** -0.5),
        "hy_f_b1": nrm((N_ODD, HY_FILTER_HIDDEN), 0.02),
        "hy_f_freq": 1.0 + nrm((N_ODD, HY_FILTER_HIDDEN), 0.02),
        "hy_f_w2": nrm((N_ODD, HY_FILTER_HIDDEN, HY_FILTER_HIDDEN), HY_FILTER_HIDDEN ** -0.5),
        "hy_f_b2": nrm((N_ODD, HY_FILTER_HIDDEN), 0.02),
        "hy_f_w3": nrm((N_ODD, HY_FILTER_HIDDEN, 2 * D), HY_FILTER_HIDDEN ** -0.5),
        "hy_skip": nrm((N_ODD, D), 1.0),
        "hy_w_out": nrm((N_ODD, D, D), D ** -0.5),
        "router_w": nrm((DEPTH, D, N_EXPERTS), D ** -0.5),
        "exp_w_gate": nrm((DEPTH, N_EXPERTS, D, D_EXPERT), D ** -0.5),
        "exp_w_up": nrm((DEPTH, N_EXPERTS, D, D_EXPERT), D ** -0.5),
        "exp_w_down": nrm((DEPTH, N_EXPERTS, D_EXPERT, D), D_EXPERT ** -0.5),
        "final_norm_g": 1.0 + nrm((D,), 0.02),
    }


def reference(x, c, ctx, c_ctx, mod_w, mod_b, norm_mix_g, norm_ffn_g, mix_w_in, na_rpb, mix_w_out,
              hy_w_in, hy_conv_w, hy_conv_b, hy_f_w1, hy_f_b1, hy_f_freq, hy_f_w2, hy_f_b2, hy_f_w3,
              hy_skip, hy_w_out, router_w, exp_w_gate, exp_w_up, exp_w_down, final_norm_g):
    last_ctx_read = DEPTH - 1 if (DEPTH - 1) % 2 == 0 else DEPTH - 2
    cs = ctx
    silu_c = jax.nn.silu(c)
    silu_cc = jax.nn.silu(c_ctx)
    for layer in range(DEPTH):
        j = layer // 2
        m_lat = (silu_c @ mod_w[layer] + mod_b[layer])[:, None, :]
        sh1, sc1, g1, sh2, sc2, g2 = jnp.split(m_lat, 6, axis=-1)
        m_ctx = silu_cc @ mod_w[layer] + mod_b[layer]
        csh1, csc1, cg1, csh2, csc2, cg2 = jnp.split(m_ctx, 6, axis=-1)
        upd = layer < last_ctx_read
        hx = modulate(rms_norm(x, norm_mix_g[layer]), sh1, sc1)
        if layer % 2 == 0 or upd:
            hc = modulate(rms_norm(cs, norm_mix_g[layer]), csh1, csc1)
        if layer % 2 == 0:
            w_in, w_out, rpb = mix_w_in[j], mix_w_out[j], na_rpb[j]
            if upd:
                qc, kc, vc, fc = split_even(hc @ w_in)
                cmix = jnp.concatenate([ctx_attention(qc, kc, vc), fourier_mix(fc)], axis=-1) @ w_out
            else:
                kvc = hc @ w_in[:, NA_WIDTH:3 * NA_WIDTH]
                Bc, Lc, _ = kvc.shape
                kc = kvc[..., :NA_WIDTH].reshape(Bc, Lc, NA_HEADS, HEAD_DIM)
                vc = kvc[..., NA_WIDTH:].reshape(Bc, Lc, NA_HEADS, HEAD_DIM)
            q, k, v, f = split_even(hx @ w_in)
            xmix = jnp.concatenate([na_latent(q, k, v, kc, vc, rpb), fourier_mix(f)], axis=-1) @ w_out
        else:
            hy = (hy_w_in[j], hy_conv_w[j], hy_conv_b[j], hy_f_w1[j], hy_f_b1[j], hy_f_freq[j],
                  hy_f_w2[j], hy_f_b2[j], hy_f_w3[j], hy_skip[j], hy_w_out[j])
            xmix = hyena_mixer(hx, *hy)
            if upd:
                cmix = hyena_mixer(hc, *hy)
        x = x + g1 * xmix
        hx2 = modulate(rms_norm(x, norm_ffn_g[layer]), sh2, sc2)
        x = x + g2 * expert_choice_ffn(hx2, router_w[layer], exp_w_gate[layer], exp_w_up[layer], exp_w_down[layer])
        if upd:
            cs = cs + cg1 * cmix
            hc2 = modulate(rms_norm(cs, norm_ffn_g[layer]), csh2, csc2)
            cs = cs + cg2 * expert_choice_ffn(hc2, router_w[layer], exp_w_gate[layer], exp_w_up[layer], exp_w_down[layer])
    return rms_norm(x, final_norm_g)
```

```python
import functools
import math

import numpy as np
import jax
import jax.numpy as jnp
from jax import lax
from jax.experimental import pallas as pl
from jax.experimental.pallas import tpu as pltpu

D_MODEL = 1024
GRID_W = 64
HEAD_DIM = 64
NA_HEADS = 12
NA_WIDTH = NA_HEADS * HEAD_DIM
NA_WIN_ROWS = 8
NA_WIN_COLS = 16
FN_GROUPS = 4
FN_GROUP_DIM = 64
FN_WIDTH = FN_GROUPS * FN_GROUP_DIM
HY_SHORT = 3
HY_EMB = 33
HY_BANDS = (HY_EMB - 1) // 2
HY_MIN_DECAY = math.log(1e-2) / 1.5
HY_MAX_DECAY = math.log(1e-2) / 0.3
N_EXPERTS = 16
EC_CAPACITY_FACTOR = 2
RMS_EPS = 1e-6

LANES = 128
HEAD_PAIRS = NA_WIDTH // LANES
NA_Q_ROWS = 2
NEG_INF = -1e30
VMEM_LIMIT = 56 * 1024 * 1024

BF16 = jnp.bfloat16
F32 = jnp.float32


def _cparams(*sem):
    return pltpu.CompilerParams(dimension_semantics=sem, vmem_limit_bytes=VMEM_LIMIT)


def _small_matmul_kernel(a_ref, w_ref, b_ref, o_ref):
    o_ref[...] = jnp.dot(a_ref[...], w_ref[...], preferred_element_type=F32,
                         precision=lax.Precision.HIGHEST) + b_ref[...]


def small_matmul(a, w, b, tn=1536):
    M, K = a.shape
    N = w.shape[1]
    return pl.pallas_call(
        _small_matmul_kernel,
        grid=(N // tn,),
        in_specs=[pl.BlockSpec((M, K), lambda j: (0, 0)),
                  pl.BlockSpec((K, tn), lambda j: (0, j)),
                  pl.BlockSpec((1, tn), lambda j: (0, j))],
        out_specs=pl.BlockSpec((M, tn), lambda j: (0, j)),
        out_shape=jax.ShapeDtypeStruct((M, N), F32),
        compiler_params=_cparams("arbitrary"),
        name="small_matmul",
    )(a, w, b.reshape(1, N))


def _norm_mod(x, g, shift, scale):
    ms = jnp.mean(x * x, axis=-1, keepdims=True)
    y = x * lax.rsqrt(ms + RMS_EPS) * g
    return y * (1.0 + scale) + shift


def _norm_proj_kernel(x_ref, g_ref, sh_ref, sc_ref, w_ref, o_ref):
    h = _norm_mod(x_ref[0], g_ref[...], sh_ref[0], sc_ref[0])
    o_ref[0] = jnp.dot(h.astype(BF16), w_ref[...], preferred_element_type=F32).astype(o_ref.dtype)


def norm_proj(x, g, shift, scale, w, out_dtype, tm):
    B, L, D = x.shape
    N = w.shape[1]
    tm = min(tm, L)
    return pl.pallas_call(
        _norm_proj_kernel,
        grid=(B, L // tm),
        in_specs=[pl.BlockSpec((1, tm, D), lambda b, i: (b, i, 0)),
                  pl.BlockSpec((1, D), lambda b, i: (0, 0)),
                  pl.BlockSpec((1, 1, D), lambda b, i: (b, 0, 0)),
                  pl.BlockSpec((1, 1, D), lambda b, i: (b, 0, 0)),
                  pl.BlockSpec((D, N), lambda b, i: (0, 0))],
        out_specs=pl.BlockSpec((1, tm, N), lambda b, i: (b, i, 0)),
        out_shape=jax.ShapeDtypeStruct((B, L, N), out_dtype),
        compiler_params=_cparams("arbitrary", "arbitrary"),
        name="norm_proj",
    )(x, g.reshape(1, D), shift.reshape(B, 1, D), scale.reshape(B, 1, D), w)


def _norm_router_kernel(x_ref, g_ref, sh_ref, sc_ref, rw_ref, h_ref, aff_ref):
    h = _norm_mod(x_ref[0], g_ref[...], sh_ref[0], sc_ref[0])
    h_ref[0] = h.astype(h_ref.dtype)
    logits = jnp.dot(h, rw_ref[...], preferred_element_type=F32, precision=lax.Precision.HIGHEST)
    m = jnp.max(logits, axis=-1, keepdims=True)
    e = jnp.exp(logits - m)
    aff_ref[0] = e / jnp.sum(e, axis=-1, keepdims=True)


def norm_router(x, g, shift, scale, router_w, tm):
    B, L, D = x.shape
    E = router_w.shape[1]
    tm = min(tm, L)
    return pl.pallas_call(
        _norm_router_kernel,
        grid=(B, L // tm),
        in_specs=[pl.BlockSpec((1, tm, D), lambda b, i: (b, i, 0)),
                  pl.BlockSpec((1, D), lambda b, i: (0, 0)),
                  pl.BlockSpec((1, 1, D), lambda b, i: (b, 0, 0)),
                  pl.BlockSpec((1, 1, D), lambda b, i: (b, 0, 0)),
                  pl.BlockSpec((D, E), lambda b, i: (0, 0))],
        out_specs=[pl.BlockSpec((1, tm, D), lambda b, i: (b, i, 0)),
                   pl.BlockSpec((1, tm, E), lambda b, i: (b, i, 0))],
        out_shape=[jax.ShapeDtypeStruct((B, L, D), BF16),
                   jax.ShapeDtypeStruct((B, L, E), F32)],
        compiler_params=_cparams("arbitrary", "arbitrary"),
        name="norm_router",
    )(x, g.reshape(1, D), shift.reshape(B, 1, D), scale.reshape(B, 1, D), router_w)


def _proj_residual_kernel(y_ref, w_ref, x_ref, gate_ref, o_ref):
    mix = jnp.dot(y_ref[0].astype(BF16), w_ref[...], preferred_element_type=F32)
    o_ref[0] = x_ref[0] + gate_ref[0] * mix


def proj_residual(y, w, x, gate, tm):
    B, L, K = y.shape
    D = w.shape[1]
    tm = min(tm, L)
    return pl.pallas_call(
        _proj_residual_kernel,
        grid=(B, L // tm),
        in_specs=[pl.BlockSpec((1, tm, K), lambda b, i: (b, i, 0)),
                  pl.BlockSpec((K, D), lambda b, i: (0, 0)),
                  pl.BlockSpec((1, tm, D), lambda b, i: (b, i, 0)),
                  pl.BlockSpec((1, 1, D), lambda b, i: (b, 0, 0))],
        out_specs=pl.BlockSpec((1, tm, D), lambda b, i: (b, i, 0)),
        out_shape=jax.ShapeDtypeStruct((B, L, D), F32),
        compiler_params=_cparams("arbitrary", "arbitrary"),
        name="proj_residual",
    )(y, w, x, gate.reshape(B, 1, D))


def _rms_kernel(x_ref, g_ref, o_ref):
    x = x_ref[0]
    ms = jnp.mean(x * x, axis=-1, keepdims=True)
    o_ref[0] = x * lax.rsqrt(ms + RMS_EPS) * g_ref[...]


def rms_norm_final(x, g, tm=1024):
    B, L, D = x.shape
    return pl.pallas_call(
        _rms_kernel,
        grid=(B, L // tm),
        in_specs=[pl.BlockSpec((1, tm, D), lambda b, i: (b, i, 0)),
                  pl.BlockSpec((1, D), lambda b, i: (0, 0))],
        out_specs=pl.BlockSpec((1, tm, D), lambda b, i: (b, i, 0)),
        out_shape=jax.ShapeDtypeStruct((B, L, D), F32),
        compiler_params=_cparams("arbitrary", "arbitrary"),
        name="rms_final",
    )(x, g.reshape(1, D))


def _two_head_rows(q):
    lane = lax.broadcasted_iota(jnp.int32, q.shape, 1)
    zero = jnp.zeros_like(q)
    return jnp.concatenate([jnp.where(lane < HEAD_DIM, q, zero),
                            jnp.where(lane >= HEAD_DIM, q, zero)], axis=0)


def _merge_two_heads(o):
    n = o.shape[0] // 2
    lane = lax.broadcasted_iota(jnp.int32, (n, LANES), 1)
    return jnp.where(lane < HEAD_DIM, o[:n], o[n:])


_CONTRACT_LAST = (((1,), (1,)), ((), ()))


def _na_key_rows(rq):
    n = rq + NA_WIN_ROWS - 1
    return n + n % 2


def _na_kernel(q_ref, k_ref, v_ref, kc_ref, vc_ref, bias_ref, o_ref, *, rows, rq):
    nq = rq * GRID_W
    nkr = _na_key_rows(rq)
    kwin = nkr * GRID_W
    kc = kc_ref[0]
    vc = vc_ref[0]

    def step(i, carry):
        r = i * rq
        start = jnp.clip(r - NA_WIN_ROWS // 2, 0, rows - nkr)
        variant = r - start
        q0 = pl.multiple_of(r * GRID_W, nq)
        k0 = pl.multiple_of(start * GRID_W, GRID_W)
        q2 = _two_head_rows(q_ref[0, pl.ds(q0, nq), :])
        kw = k_ref[0, pl.ds(k0, kwin), :]
        vw = v_ref[0, pl.ds(k0, kwin), :]
        s_loc = lax.dot_general(q2, kw, _CONTRACT_LAST, preferred_element_type=F32)
        s_loc = s_loc * (HEAD_DIM ** -0.5) + bias_ref[0, variant]
        s_ctx = lax.dot_general(q2, kc, _CONTRACT_LAST, preferred_element_type=F32) * (HEAD_DIM ** -0.5)
        m = jnp.maximum(jnp.max(s_loc, axis=-1, keepdims=True), jnp.max(s_ctx, axis=-1, keepdims=True))
        p_loc = jnp.exp(s_loc - m)
        p_ctx = jnp.exp(s_ctx - m)
        denom = jnp.sum(p_loc, axis=-1, keepdims=True) + jnp.sum(p_ctx, axis=-1, keepdims=True)
        o = (jnp.dot(p_loc.astype(BF16), vw, preferred_element_type=F32)
             + jnp.dot(p_ctx.astype(BF16), vc, preferred_element_type=F32))
        o = o / denom
        o_ref[0, pl.ds(q0, nq), :] = _merge_two_heads(o).astype(o_ref.dtype)
        return carry

    lax.fori_loop(0, rows // rq, step, 0)


def _na_bias_tables(rpb, rows, rq):
    W = GRID_W
    nkr = _na_key_rows(rq)
    n_var = nkr - rq + 1
    cols = np.arange(W)
    col_start = np.clip(cols - NA_WIN_COLS // 2, 0, W - NA_WIN_COLS)
    ro = np.zeros((n_var, rq * W, nkr * W), np.int32)
    co = np.zeros_like(ro)
    ok = np.zeros(ro.shape, bool)
    for var in range(n_var):
        r = var if var <= NA_WIN_ROWS // 2 else rows - nkr + var
        start = r - var
        for dr in range(rq):
            q_row = r + dr
            if q_row >= rows:
                continue
            rs = int(np.clip(q_row - NA_WIN_ROWS // 2, 0, rows - NA_WIN_ROWS))
            key_row = start + np.arange(nkr)
            row_ok = (key_row >= rs) & (key_row < rs + NA_WIN_ROWS)
            row_off = np.clip(key_row - q_row + NA_WIN_ROWS - 1, 0, 2 * NA_WIN_ROWS - 2)
            col_ok = (cols[None, :] >= col_start[:, None]) & (cols[None, :] < col_start[:, None] + NA_WIN_COLS)
            col_off = np.clip(cols[None, :] - cols[:, None] + NA_WIN_COLS - 1, 0, 2 * NA_WIN_COLS - 2)
            sl = slice(dr * W, (dr + 1) * W)
            ro[var, sl] = np.broadcast_to(row_off[None, :, None], (W, nkr, W)).reshape(W, nkr * W)
            co[var, sl] = np.broadcast_to(col_off[:, None, :], (W, nkr, W)).reshape(W, nkr * W)
            ok[var, sl] = (row_ok[None, :, None] & col_ok[:, None, :]).reshape(W, nkr * W)
    tab = rpb.astype(F32)[:, ro, co]
    tab = jnp.where(ok[None], tab, NEG_INF)
    tab = tab.reshape(HEAD_PAIRS, 2, n_var, rq * W, nkr * W).transpose(0, 2, 1, 3, 4)
    return tab.reshape(HEAD_PAIRS, n_var, 2 * rq * W, nkr * W)


def na_attention(p, pc, kc_off, rpb):
    B, L, _ = p.shape
    Lc = pc.shape[1]
    rows = L // GRID_W
    rq = NA_Q_ROWS
    bias = _na_bias_tables(rpb, rows, rq)
    nkr = _na_key_rows(rq)
    kern = functools.partial(_na_kernel, rows=rows, rq=rq)
    return pl.pallas_call(
        kern,
        grid=(B, HEAD_PAIRS),
        in_specs=[pl.BlockSpec((1, L, LANES), lambda b, j: (b, 0, j)),
                  pl.BlockSpec((1, L, LANES), lambda b, j: (b, 0, HEAD_PAIRS + j)),
                  pl.BlockSpec((1, L, LANES), lambda b, j: (b, 0, 2 * HEAD_PAIRS + j)),
                  pl.BlockSpec((1, Lc, LANES), lambda b, j: (b, 0, kc_off + j)),
                  pl.BlockSpec((1, Lc, LANES), lambda b, j: (b, 0, kc_off + HEAD_PAIRS + j)),
                  pl.BlockSpec((1, nkr - rq + 1, 2 * rq * GRID_W, nkr * GRID_W), lambda b, j: (j, 0, 0, 0))],
        out_specs=pl.BlockSpec((1, L, LANES), lambda b, j: (b, 0, j)),
        out_shape=jax.ShapeDtypeStruct((B, L, NA_WIDTH), BF16),
        compiler_params=_cparams("arbitrary", "arbitrary"),
        name="na_attention",
    )(p, p, p, pc, pc, bias)


def _ctx_attn_kernel(q_ref, k_ref, v_ref, o_ref):
    q2 = _two_head_rows(q_ref[0])
    s = lax.dot_general(q2, k_ref[0], _CONTRACT_LAST, preferred_element_type=F32) * (HEAD_DIM ** -0.5)
    m = jnp.max(s, axis=-1, keepdims=True)
    e = jnp.exp(s - m)
    o = jnp.dot(e.astype(BF16), v_ref[0], preferred_element_type=F32) / jnp.sum(e, axis=-1, keepdims=True)
    o_ref[0] = _merge_two_heads(o).astype(o_ref.dtype)


def ctx_attention(pc):
    B, Lc, _ = pc.shape
    return pl.pallas_call(
        _ctx_attn_kernel,
        grid=(B, HEAD_PAIRS),
        in_specs=[pl.BlockSpec((1, Lc, LANES), lambda b, j: (b, 0, j)),
                  pl.BlockSpec((1, Lc, LANES), lambda b, j: (b, 0, HEAD_PAIRS + j)),
                  pl.BlockSpec((1, Lc, LANES), lambda b, j: (b, 0, 2 * HEAD_PAIRS + j))],
        out_specs=pl.BlockSpec((1, Lc, LANES), lambda b, j: (b, 0, j)),
        out_shape=jax.ShapeDtypeStruct((B, Lc, NA_WIDTH), BF16),
        compiler_params=_cparams("arbitrary", "arbitrary"),
        name="ctx_attention",
    )(pc, pc, pc)


def _expert_ffn_kernel(x_ref, gate_ref, wg_ref, wu_ref, wd_ref, o_ref, *, fchunk):
    x = x_ref[0, 0]
    F = wg_ref.shape[2]
    acc = jnp.zeros(o_ref.shape[2:], F32)
    for f0 in range(0, F, fchunk):
        g = jnp.dot(x, wg_ref[0, :, f0:f0 + fchunk], preferred_element_type=F32)
        u = jnp.dot(x, wu_ref[0, :, f0:f0 + fchunk], preferred_element_type=F32)
        hid = (g * jax.nn.sigmoid(g) * u).astype(BF16)
        acc = acc + jnp.dot(hid, wd_ref[0, f0:f0 + fchunk, :], preferred_element_type=F32)
    o_ref[0, 0] = acc * gate_ref[0, 0]


def expert_ffn(xe, gate, wg, wu, wd, fchunk=512):
    E, B, cap, D = xe.shape
    F = wg.shape[2]
    kern = functools.partial(_expert_ffn_kernel, fchunk=fchunk)
    return pl.pallas_call(
        kern,
        grid=(E, B),
        in_specs=[pl.BlockSpec((1, 1, cap, D), lambda e, b: (e, b, 0, 0)),
                  pl.BlockSpec((1, 1, cap, 1), lambda e, b: (e, b, 0, 0)),
                  pl.BlockSpec((1, D, F), lambda e, b: (e, 0, 0)),
                  pl.BlockSpec((1, D, F), lambda e, b: (e, 0, 0)),
                  pl.BlockSpec((1, F, D), lambda e, b: (e, 0, 0))],
        out_specs=pl.BlockSpec((1, 1, cap, D), lambda e, b: (e, b, 0, 0)),
        out_shape=jax.ShapeDtypeStruct((E, B, cap, D), F32),
        compiler_params=_cparams("arbitrary", "arbitrary"),
        name="expert_ffn",
    )(xe, gate, wg, wu, wd)


def expert_choice_ffn(x, g, shift, scale, gate_res, router_w, wg, wu, wd, tm):
    B, n, D = x.shape
    E = N_EXPERTS
    cap = EC_CAPACITY_FACTOR * n // E
    h, aff = norm_router(x, g, shift, scale, router_w, tm)
    gate, idx = lax.top_k(jnp.transpose(aff, (2, 0, 1)), cap)
    bidx = jnp.arange(B, dtype=idx.dtype)[None, :, None]
    xe = h[bidx, idx]
    ye = expert_ffn(xe, gate[..., None], wg, wu, wd)
    flat = (idx + bidx * n).reshape(-1)
    moe = jnp.zeros((B * n, D), F32).at[flat].add(ye.reshape(-1, D)).reshape(B, n, D)
    return x + gate_res[:, None, :] * moe


def fourier_mix(u):
    B, L, _ = u.shape
    ug = u.astype(F32).reshape(B, L, FN_GROUPS, FN_GROUP_DIM)
    y = jnp.fft.fft2(ug, axes=(1, 3), norm='ortho').real
    return y.reshape(B, L, FN_WIDTH)


def hyena_filters(L, w1, b1, freq, w2, b2, w3):
    hp = lax.Precision.HIGHEST
    t = jnp.linspace(0.0, 1.0, L, dtype=F32)[:, None]
    w = 2 * math.pi * jnp.arange(L, dtype=F32)[:, None] / L
    bands = jnp.linspace(1e-4, HY_BANDS - 1, HY_BANDS, dtype=F32)[None, :]
    z = jnp.concatenate([t, jnp.cos(bands * w), -jnp.sin(bands * w)], axis=-1)
    hid = jnp.sin(freq * (jnp.dot(z, w1, precision=hp) + b1))
    hid = jnp.sin(freq * (jnp.dot(hid, w2, precision=hp) + b2))
    h = jnp.dot(hid, w3, precision=hp).reshape(L, 2, D_MODEL)
    deltas = jnp.abs(jnp.linspace(HY_MIN_DECAY, HY_MAX_DECAY, D_MODEL, dtype=F32))
    window = jnp.exp(-t * deltas[None, :])
    h = h * window[:, None, :]
    filt = jnp.concatenate([h[:, 0], jnp.zeros((1, D_MODEL), F32), h[1:, 1][::-1]], axis=0)
    return filt / jnp.sum(jnp.abs(filt), axis=0, keepdims=True)


def hyena_core(u, conv_w, conv_b, filt, skip):
    B, L, _ = u.shape
    up = jnp.pad(u, ((0, 0), (1, 1), (0, 0)))
    u = up[:, :-2] * conv_w[0] + up[:, 1:-1] * conv_w[1] + up[:, 2:] * conv_w[2] + conv_b
    x0, x1, v = jnp.split(u, 3, axis=-1)
    vx = v * x1
    y = jnp.fft.irfft(jnp.fft.rfft(vx, n=2 * L, axis=1) * jnp.fft.rfft(filt, n=2 * L, axis=0)[None],
                      n=2 * L, axis=1)[:, :L]
    return x0 * (y + vx * skip)


def kernel(x, c, ctx, c_ctx, mod_w, mod_b, norm_mix_g, norm_ffn_g, mix_w_in, na_rpb, mix_w_out,
           hy_w_in, hy_conv_w, hy_conv_b, hy_f_w1, hy_f_b1, hy_f_freq, hy_f_w2, hy_f_b2, hy_f_w3,
           hy_skip, hy_w_out, router_w, exp_w_gate, exp_w_up, exp_w_down, final_norm_g):
    depth = mod_w.shape[0]
    B, L, D = x.shape
    Lc = ctx.shape[1]
    last_ctx_read = depth - 1 if (depth - 1) % 2 == 0 else depth - 2
    TM = 512

    cs = ctx
    cond = jnp.concatenate([jax.nn.silu(c), jax.nn.silu(c_ctx)[None], jnp.zeros((8 - (B + 1) % 8, D), F32)], 0)

    for layer in range(depth):
        j = layer // 2
        m_all = small_matmul(cond, mod_w[layer], mod_b[layer])
        sh1, sc1, g1, sh2, sc2, g2 = jnp.split(m_all[:B], 6, axis=-1)
        m_ctx = jnp.broadcast_to(m_all[B:B + 1], (B, 6 * D))
        csh1, csc1, cg1, csh2, csc2, cg2 = jnp.split(m_ctx, 6, axis=-1)
        upd = layer < last_ctx_read
        wg = exp_w_gate[layer].astype(BF16)
        wu = exp_w_up[layer].astype(BF16)
        wd = exp_w_down[layer].astype(BF16)

        if layer % 2 == 0:
            w_in = mix_w_in[j].astype(BF16)
            w_out = mix_w_out[j].astype(BF16)
            if upd:
                pc = norm_proj(cs, norm_mix_g[layer], csh1, csc1, w_in, BF16, TM)
                kc_off = HEAD_PAIRS
                cmix_in = jnp.concatenate([ctx_attention(pc), fourier_mix(pc[..., 3 * NA_WIDTH:]).astype(BF16)], -1)
            else:
                pc = norm_proj(cs, norm_mix_g[layer], csh1, csc1, w_in[:, NA_WIDTH:3 * NA_WIDTH], BF16, TM)
                kc_off = 0
            p = norm_proj(x, norm_mix_g[layer], sh1, sc1, w_in, BF16, TM)
            att = na_attention(p, pc, kc_off, na_rpb[j])
            mix_in = jnp.concatenate([att, fourier_mix(p[..., 3 * NA_WIDTH:]).astype(BF16)], -1)
            x = proj_residual(mix_in, w_out, x, g1, TM)
            if upd:
                cs = proj_residual(cmix_in, w_out, cs, cg1, TM)
        else:
            w_in = hy_w_in[j].astype(BF16)
            w_out = hy_w_out[j].astype(BF16)
            fargs = (hy_f_w1[j], hy_f_b1[j], hy_f_freq[j], hy_f_w2[j], hy_f_b2[j], hy_f_w3[j])
            u = norm_proj(x, norm_mix_g[layer], sh1, sc1, w_in, F32, TM)
            y = hyena_core(u, hy_conv_w[j], hy_conv_b[j], hyena_filters(L, *fargs), hy_skip[j])
            x = proj_residual(y, w_out, x, g1, TM)
            if upd:
                uc = norm_proj(cs, norm_mix_g[layer], csh1, csc1, w_in, F32, TM)
                yc = hyena_core(uc, hy_conv_w[j], hy_conv_b[j], hyena_filters(Lc, *fargs), hy_skip[j])
                cs = proj_residual(yc, w_out, cs, cg1, TM)

        x = expert_choice_ffn(x, norm_ffn_g[layer], sh2, sc2, g2, router_w[layer], wg, wu, wd, TM)
        if upd:
            cs = expert_choice_ffn(cs, norm_ffn_g[layer], csh2, csc2, cg2, router_w[layer], wg, wu, wd, TM)

    return rms_norm_final(x, final_norm_g)
```

```python
import functools
import math

import numpy as np
import jax
import jax.numpy as jnp
from jax import lax
from jax.experimental import pallas as pl
from jax.experimental.pallas import tpu as pltpu

D_MODEL = 1024
GRID_W = 64
HEAD_DIM = 64
NA_HEADS = 12
NA_WIDTH = NA_HEADS * HEAD_DIM
NA_WIN_ROWS = 8
NA_WIN_COLS = 16
FN_GROUPS = 4
FN_GROUP_DIM = 64
FN_WIDTH = FN_GROUPS * FN_GROUP_DIM
HY_SHORT = 3
HY_EMB = 33
HY_BANDS = (HY_EMB - 1) // 2
HY_MIN_DECAY = math.log(1e-2) / 1.5
HY_MAX_DECAY = math.log(1e-2) / 0.3
N_EXPERTS = 16
EC_CAPACITY_FACTOR = 2
RMS_EPS = 1e-6

LANES = 128
HEAD_PAIRS = NA_WIDTH // LANES
NA_Q_ROWS = 2
NEG_INF = -1e30
VMEM_LIMIT = 56 * 1024 * 1024

BF16 = jnp.bfloat16
F32 = jnp.float32


def _cparams(*sem):
    return pltpu.CompilerParams(dimension_semantics=sem, vmem_limit_bytes=VMEM_LIMIT)


def _small_matmul_kernel(a_ref, w_ref, b_ref, o_ref):
    o_ref[...] = jnp.dot(a_ref[...], w_ref[...], preferred_element_type=F32,
                         precision=lax.Precision.HIGHEST) + b_ref[...]


def small_matmul(a, w, b, tn=1536):
    M, K = a.shape
    N = w.shape[1]
    return pl.pallas_call(
        _small_matmul_kernel,
        grid=(N // tn,),
        in_specs=[pl.BlockSpec((M, K), lambda j: (0, 0)),
                  pl.BlockSpec((K, tn), lambda j: (0, j)),
                  pl.BlockSpec((1, tn), lambda j: (0, j))],
        out_specs=pl.BlockSpec((M, tn), lambda j: (0, j)),
        out_shape=jax.ShapeDtypeStruct((M, N), F32),
        compiler_params=_cparams("arbitrary"),
        name="small_matmul",
    )(a, w, b.reshape(1, N))


def _norm_mod(x, g, shift, scale):
    ms = jnp.mean(x * x, axis=-1, keepdims=True)
    y = x * lax.rsqrt(ms + RMS_EPS) * g
    return y * (1.0 + scale) + shift


def _norm_proj_kernel(x_ref, g_ref, sh_ref, sc_ref, w_ref, o_ref):
    h = _norm_mod(x_ref[0], g_ref[...], sh_ref[0], sc_ref[0])
    o_ref[0] = jnp.dot(h.astype(BF16), w_ref[...], preferred_element_type=F32).astype(o_ref.dtype)


def norm_proj(x, g, shift, scale, w, out_dtype, tm):
    B, L, D = x.shape
    N = w.shape[1]
    tm = min(tm, L)
    return pl.pallas_call(
        _norm_proj_kernel,
        grid=(B, L // tm),
        in_specs=[pl.BlockSpec((1, tm, D), lambda b, i: (b, i, 0)),
                  pl.BlockSpec((1, D), lambda b, i: (0, 0)),
                  pl.BlockSpec((1, 1, D), lambda b, i: (b, 0, 0)),
                  pl.BlockSpec((1, 1, D), lambda b, i: (b, 0, 0)),
                  pl.BlockSpec((D, N), lambda b, i: (0, 0))],
        out_specs=pl.BlockSpec((1, tm, N), lambda b, i: (b, i, 0)),
        out_shape=jax.ShapeDtypeStruct((B, L, N), out_dtype),
        compiler_params=_cparams("arbitrary", "arbitrary"),
        name="norm_proj",
    )(x, g.reshape(1, D), shift.reshape(B, 1, D), scale.reshape(B, 1, D), w)


def _norm_router_kernel(x_ref, g_ref, sh_ref, sc_ref, rw_ref, h_ref, aff_ref):
    h = _norm_mod(x_ref[0], g_ref[...], sh_ref[0], sc_ref[0])
    h_ref[0] = h.astype(h_ref.dtype)
    logits = jnp.dot(h, rw_ref[...], preferred_element_type=F32, precision=lax.Precision.HIGHEST)
    m = jnp.max(logits, axis=-1, keepdims=True)
    e = jnp.exp(logits - m)
    aff_ref[0] = e / jnp.sum(e, axis=-1, keepdims=True)


def norm_router(x, g, shift, scale, router_w, tm):
    B, L, D = x.shape
    E = router_w.shape[1]
    tm = min(tm, L)
    return pl.pallas_call(
        _norm_router_kernel,
        grid=(B, L // tm),
        in_specs=[pl.BlockSpec((1, tm, D), lambda b, i: (b, i, 0)),
                  pl.BlockSpec((1, D), lambda b, i: (0, 0)),
                  pl.BlockSpec((1, 1, D), lambda b, i: (b, 0, 0)),
                  pl.BlockSpec((1, 1, D), lambda b, i: (b, 0, 0)),
                  pl.BlockSpec((D, E), lambda b, i: (0, 0))],
        out_specs=[pl.BlockSpec((1, tm, D), lambda b, i: (b, i, 0)),
                   pl.BlockSpec((1, tm, E), lambda b, i: (b, i, 0))],
        out_shape=[jax.ShapeDtypeStruct((B, L, D), BF16),
                   jax.ShapeDtypeStruct((B, L, E), F32)],
        compiler_params=_cparams("arbitrary", "arbitrary"),
        name="norm_router",
    )(x, g.reshape(1, D), shift.reshape(B, 1, D), scale.reshape(B, 1, D), router_w)


def _proj_residual_kernel(y_ref, w_ref, x_ref, gate_ref, o_ref):
    mix = jnp.dot(y_ref[0].astype(BF16), w_ref[...], preferred_element_type=F32)
    o_ref[0] = x_ref[0] + gate_ref[0] * mix


def proj_residual(y, w, x, gate, tm):
    B, L, K = y.shape
    D = w.shape[1]
    tm = min(tm, L)
    return pl.pallas_call(
        _proj_residual_kernel,
        grid=(B, L // tm),
        in_specs=[pl.BlockSpec((1, tm, K), lambda b, i: (b, i, 0)),
                  pl.BlockSpec((K, D), lambda b, i: (0, 0)),
                  pl.BlockSpec((1, tm, D), lambda b, i: (b, i, 0)),
                  pl.BlockSpec((1, 1, D), lambda b, i: (b, 0, 0))],
        out_specs=pl.BlockSpec((1, tm, D), lambda b, i: (b, i, 0)),
        out_shape=jax.ShapeDtypeStruct((B, L, D), F32),
        compiler_params=_cparams("arbitrary", "arbitrary"),
        name="proj_residual",
    )(y, w, x, gate.reshape(B, 1, D))


def _rms_kernel(x_ref, g_ref, o_ref):
    x = x_ref[0]
    ms = jnp.mean(x * x, axis=-1, keepdims=True)
    o_ref[0] = x * lax.rsqrt(ms + RMS_EPS) * g_ref[...]


def rms_norm_final(x, g, tm=1024):
    B, L, D = x.shape
    return pl.pallas_call(
        _rms_kernel,
        grid=(B, L // tm),
        in_specs=[pl.BlockSpec((1, tm, D), lambda b, i: (b, i, 0)),
                  pl.BlockSpec((1, D), lambda b, i: (0, 0))],
        out_specs=pl.BlockSpec((1, tm, D), lambda b, i: (b, i, 0)),
        out_shape=jax.ShapeDtypeStruct((B, L, D), F32),
        compiler_params=_cparams("arbitrary", "arbitrary"),
        name="rms_final",
    )(x, g.reshape(1, D))


def _two_head_rows(q):
    lane = lax.broadcasted_iota(jnp.int32, q.shape, 1)
    zero = jnp.zeros_like(q)
    return jnp.concatenate([jnp.where(lane < HEAD_DIM, q, zero),
                            jnp.where(lane >= HEAD_DIM, q, zero)], axis=0)


def _merge_two_heads(o):
    n = o.shape[0] // 2
    lane = lax.broadcasted_iota(jnp.int32, (n, LANES), 1)
    return jnp.where(lane < HEAD_DIM, o[:n], o[n:])


_CONTRACT_LAST = (((1,), (1,)), ((), ()))


def _na_key_rows(rq):
    n = rq + NA_WIN_ROWS - 1
    return n + n % 2


def _na_kernel(q_ref, k_ref, v_ref, kc_ref, vc_ref, bias_ref, o_ref, *, rows, rq):
    nq = rq * GRID_W
    nkr = _na_key_rows(rq)
    kwin = nkr * GRID_W
    kc = kc_ref[0]
    vc = vc_ref[0]

    def step(i, carry):
        r = i * rq
        start = jnp.clip(r - NA_WIN_ROWS // 2, 0, rows - nkr)
        variant = r - start
        q0 = pl.multiple_of(r * GRID_W, nq)
        k0 = pl.multiple_of(start * GRID_W, GRID_W)
        q2 = _two_head_rows(q_ref[0, pl.ds(q0, nq), :])
        kw = k_ref[0, pl.ds(k0, kwin), :]
        vw = v_ref[0, pl.ds(k0, kwin), :]
        s_loc = lax.dot_general(q2, kw, _CONTRACT_LAST, preferred_element_type=F32)
        s_loc = s_loc * (HEAD_DIM ** -0.5) + bias_ref[0, variant]
        s_ctx = lax.dot_general(q2, kc, _CONTRACT_LAST, preferred_element_type=F32) * (HEAD_DIM ** -0.5)
        m = jnp.maximum(jnp.max(s_loc, axis=-1, keepdims=True), jnp.max(s_ctx, axis=-1, keepdims=True))
        p_loc = jnp.exp(s_loc - m)
        p_ctx = jnp.exp(s_ctx - m)
        denom = jnp.sum(p_loc, axis=-1, keepdims=True) + jnp.sum(p_ctx, axis=-1, keepdims=True)
        o = (jnp.dot(p_loc.astype(BF16), vw, preferred_element_type=F32)
             + jnp.dot(p_ctx.astype(BF16), vc, preferred_element_type=F32))
        o = o / denom
        o_ref[0, pl.ds(q0, nq), :] = _merge_two_heads(o).astype(o_ref.dtype)
        return carry

    lax.fori_loop(0, rows // rq, step, 0)


def _na_bias_tables(rpb, rows, rq):
    W = GRID_W
    nkr = _na_key_rows(rq)
    n_var = nkr - rq + 1
    cols = np.arange(W)
    col_start = np.clip(cols - NA_WIN_COLS // 2, 0, W - NA_WIN_COLS)
    ro = np.zeros((n_var, rq * W, nkr * W), np.int32)
    co = np.zeros_like(ro)
    ok = np.zeros(ro.shape, bool)
    for var in range(n_var):
        r = var if var <= NA_WIN_ROWS // 2 else rows - nkr + var
        start = r - var
        for dr in range(rq):
            q_row = r + dr
            if q_row >= rows:
                continue
            rs = int(np.clip(q_row - NA_WIN_ROWS // 2, 0, rows - NA_WIN_ROWS))
            key_row = start + np.arange(nkr)
            row_ok = (key_row >= rs) & (key_row < rs + NA_WIN_ROWS)
            row_off = np.clip(key_row - q_row + NA_WIN_ROWS - 1, 0, 2 * NA_WIN_ROWS - 2)
            col_ok = (cols[None, :] >= col_start[:, None]) & (cols[None, :] < col_start[:, None] + NA_WIN_COLS)
            col_off = np.clip(cols[None, :] - cols[:, None] + NA_WIN_COLS - 1, 0, 2 * NA_WIN_COLS - 2)
            sl = slice(dr * W, (dr + 1) * W)
            ro[var, sl] = np.broadcast_to(row_off[None, :, None], (W, nkr, W)).reshape(W, nkr * W)
            co[var, sl] = np.broadcast_to(col_off[:, None, :], (W, nkr, W)).reshape(W, nkr * W)
            ok[var, sl] = (row_ok[None, :, None] & col_ok[:, None, :]).reshape(W, nkr * W)
    tab = rpb.astype(F32)[:, ro, co]
    tab = jnp.where(ok[None], tab, NEG_INF)
    tab = tab.reshape(HEAD_PAIRS, 2, n_var, rq * W, nkr * W).transpose(0, 2, 1, 3, 4)
    return tab.reshape(HEAD_PAIRS, n_var, 2 * rq * W, nkr * W)


def na_attention(p, pc, kc_off, rpb):
    B, L, _ = p.shape
    Lc = pc.shape[1]
    rows = L // GRID_W
    rq = NA_Q_ROWS
    bias = _na_bias_tables(rpb, rows, rq)
    nkr = _na_key_rows(rq)
    kern = functools.partial(_na_kernel, rows=rows, rq=rq)
    return pl.pallas_call(
        kern,
        grid=(B, HEAD_PAIRS),
        in_specs=[pl.BlockSpec((1, L, LANES), lambda b, j: (b, 0, j)),
                  pl.BlockSpec((1, L, LANES), lambda b, j: (b, 0, HEAD_PAIRS + j)),
                  pl.BlockSpec((1, L, LANES), lambda b, j: (b, 0, 2 * HEAD_PAIRS + j)),
                  pl.BlockSpec((1, Lc, LANES), lambda b, j: (b, 0, kc_off + j)),
                  pl.BlockSpec((1, Lc, LANES), lambda b, j: (b, 0, kc_off + HEAD_PAIRS + j)),
                  pl.BlockSpec((1, nkr - rq + 1, 2 * rq * GRID_W, nkr * GRID_W), lambda b, j: (j, 0, 0, 0))],
        out_specs=pl.BlockSpec((1, L, LANES), lambda b, j: (b, 0, j)),
        out_shape=jax.ShapeDtypeStruct((B, L, NA_WIDTH), BF16),
        compiler_params=_cparams("arbitrary", "arbitrary"),
        name="na_attention",
    )(p, p, p, pc, pc, bias)


def _ctx_attn_kernel(q_ref, k_ref, v_ref, o_ref):
    q2 = _two_head_rows(q_ref[0])
    s = lax.dot_general(q2, k_ref[0], _CONTRACT_LAST, preferred_element_type=F32) * (HEAD_DIM ** -0.5)
    m = jnp.max(s, axis=-1, keepdims=True)
    e = jnp.exp(s - m)
    o = jnp.dot(e.astype(BF16), v_ref[0], preferred_element_type=F32) / jnp.sum(e, axis=-1, keepdims=True)
    o_ref[0] = _merge_two_heads(o).astype(o_ref.dtype)


def ctx_attention(pc):
    B, Lc, _ = pc.shape
    return pl.pallas_call(
        _ctx_attn_kernel,
        grid=(B, HEAD_PAIRS),
        in_specs=[pl.BlockSpec((1, Lc, LANES), lambda b, j: (b, 0, j)),
                  pl.BlockSpec((1, Lc, LANES), lambda b, j: (b, 0, HEAD_PAIRS + j)),
                  pl.BlockSpec((1, Lc, LANES), lambda b, j: (b, 0, 2 * HEAD_PAIRS + j))],
        out_specs=pl.BlockSpec((1, Lc, LANES), lambda b, j: (b, 0, j)),
        out_shape=jax.ShapeDtypeStruct((B, Lc, NA_WIDTH), BF16),
        compiler_params=_cparams("arbitrary", "arbitrary"),
        name="ctx_attention",
    )(pc, pc, pc)


def _expert_ffn_kernel(x_ref, gate_ref, wg_ref, wu_ref, wd_ref, o_ref, *, fchunk):
    x = x_ref[0, 0]
    F = wg_ref.shape[2]
    acc = jnp.zeros(o_ref.shape[2:], F32)
    for f0 in range(0, F, fchunk):
        g = jnp.dot(x, wg_ref[0, :, f0:f0 + fchunk], preferred_element_type=F32)
        u = jnp.dot(x, wu_ref[0, :, f0:f0 + fchunk], preferred_element_type=F32)
        hid = (g * jax.nn.sigmoid(g) * u).astype(BF16)
        acc = acc + jnp.dot(hid, wd_ref[0, f0:f0 + fchunk, :], preferred_element_type=F32)
    o_ref[0, 0] = acc * gate_ref[0, 0]


def expert_ffn(xe, gate, wg, wu, wd, fchunk=512):
    E, B, cap, D = xe.shape
    F = wg.shape[2]
    kern = functools.partial(_expert_ffn_kernel, fchunk=fchunk)
    return pl.pallas_call(
        kern,
        grid=(E, B),
        in_specs=[pl.BlockSpec((1, 1, cap, D), lambda e, b: (e, b, 0, 0)),
                  pl.BlockSpec((1, 1, cap, 1), lambda e, b: (e, b, 0, 0)),
                  pl.BlockSpec((1, D, F), lambda e, b: (e, 0, 0)),
                  pl.BlockSpec((1, D, F), lambda e, b: (e, 0, 0)),
                  pl.BlockSpec((1, F, D), lambda e, b: (e, 0, 0))],
        out_specs=pl.BlockSpec((1, 1, cap, D), lambda e, b: (e, b, 0, 0)),
        out_shape=jax.ShapeDtypeStruct((E, B, cap, D), F32),
        compiler_params=_cparams("arbitrary", "arbitrary"),
        name="expert_ffn",
    )(xe, gate, wg, wu, wd)


def expert_choice_ffn(x, g, shift, scale, gate_res, router_w, wg, wu, wd, tm):
    B, n, D = x.shape
    E = N_EXPERTS
    cap = EC_CAPACITY_FACTOR * n // E
    h, aff = norm_router(x, g, shift, scale, router_w, tm)
    gate, idx = lax.top_k(jnp.transpose(aff, (2, 0, 1)), cap)
    bidx = jnp.arange(B, dtype=idx.dtype)[None, :, None]
    xe = h[bidx, idx]
    ye = expert_ffn(xe, gate[..., None], wg, wu, wd)
    flat = (idx + bidx * n).reshape(-1)
    moe = jnp.zeros((B * n, D), F32).at[flat].add(ye.reshape(-1, D)).reshape(B, n, D)
    return x + gate_res[:, None, :] * moe


DFT_MINOR = 128
DFT_PITCH_PAD = 8
DFT_QB = 16


def _dft_mats(n_major, a_used):
    N = n_major * DFT_MINOR
    q = np.arange(n_major)
    a = np.arange(a_used)
    b = np.arange(DFT_MINOR)
    ang1 = -2 * np.pi * np.outer(q, a) / n_major
    f1 = np.concatenate([np.cos(ang1), np.sin(ang1)], 0)
    ang2 = -2 * np.pi * np.outer(b, b) / DFT_MINOR
    c2, s2 = np.cos(ang2), np.sin(ang2)
    f2 = np.block([[c2, -s2], [s2, c2]])
    f2c = np.block([[c2, s2], [-s2, c2]])
    lane1 = np.ones((1, 1, LANES))

    def tw(hi, lo):
        ang = -2 * np.pi * np.outer(hi, lo) / N
        return (np.cos(ang)[:, :, None] * lane1, np.sin(ang)[:, :, None] * lane1)

    t1r, t1i = tw(8 * np.arange(DFT_MINOR // 8), q)
    t0r, t0i = tw(np.arange(8), q)
    u1r, u1i = tw(8 * np.arange(n_major // 8), b)
    u0r, u0i = tw(np.arange(8), b)
    f32 = lambda *xs: [np.asarray(x, np.float32) for x in xs]
    return dict(f1=f1, f2=f2, f2c=f2c, tw_fwd=f32(t1r, t1i, t0r, t0i), tw_inv=f32(u1r, u1i, u0r, u0i))


def _cmul(ar, ai, br, bi):
    return ar * br - ai * bi, ar * bi + ai * br


def _dft_fwd_kernel(*refs, n_major, a_used, mode, precision):
    x_ref, f1_ref, f2_ref, t1r, t1i, t0r, t0i = refs[:7]
    rest = refs[7:]
    if mode == "mul":
        hr_ref, hi_ref, or_ref, oi_ref, sr, si = rest
    elif mode == "chanmix":
        mc_ref, ms_ref, or_ref, sr, si = rest
    else:
        or_ref, oi_ref, sr, si = rest
    pitch = n_major + DFT_PITCH_PAD
    qb = pl.program_id(2)
    mm_dtype = f1_ref.dtype

    @pl.when(qb == 0)
    def _stage1():
        def slab(b, carry):
            y = jnp.dot(f1_ref[...], x_ref[0, b].astype(mm_dtype), preferred_element_type=F32,
                        precision=precision)
            twr, twi = _cmul(t1r[b // 8], t1i[b // 8], t0r[b % 8], t0i[b % 8])
            yr, yi = _cmul(y[:n_major], y[n_major:], twr, twi)
            row = pl.multiple_of(b * pitch, 8)
            sr[pl.ds(row, n_major), :] = yr
            si[pl.ds(row, n_major), :] = yi
            return carry
        lax.fori_loop(0, DFT_MINOR, slab, 0)

    def freq(j, carry):
        q = qb * DFT_QB + j
        g = jnp.concatenate([sr[pl.ds(q, DFT_MINOR, stride=pitch), :],
                             si[pl.ds(q, DFT_MINOR, stride=pitch), :]], axis=0).astype(mm_dtype)
        xf = jnp.dot(f2_ref[...], g, preferred_element_type=F32, precision=precision)
        xr, xi = xf[:DFT_MINOR], xf[DFT_MINOR:]
        if mode == "mul":
            xr, xi = _cmul(xr, xi, hr_ref[0, j], hi_ref[0, j])
            or_ref[0, j] = xr.astype(or_ref.dtype)
            oi_ref[0, j] = xi.astype(oi_ref.dtype)
        elif mode == "chanmix":
            or_ref[0, j] = (jnp.dot(xr.astype(mm_dtype), mc_ref[...], preferred_element_type=F32)
                            + jnp.dot(xi.astype(mm_dtype), ms_ref[...], preferred_element_type=F32))
        else:
            or_ref[0, j] = xr
            oi_ref[0, j] = xi
        return carry
    lax.fori_loop(0, DFT_QB, freq, 0)


def dft_forward(xp, n_major, mode="spectrum", h=None, chan=None, precise=False):
    Bb, _, a_used, C = xp.shape
    mats = _dft_mats(n_major, a_used)
    mm_dtype = F32 if precise else BF16
    precision = lax.Precision.HIGHEST if precise else None
    pitch = n_major + DFT_PITCH_PAD
    const = lambda shape: pl.BlockSpec(shape, lambda bb, cb, qb: (0,) * len(shape))
    spec_blk = pl.BlockSpec((1, DFT_QB, DFT_MINOR, LANES), lambda bb, cb, qb: (bb, qb, 0, cb))
    args = [xp, jnp.asarray(mats["f1"], mm_dtype), jnp.asarray(mats["f2"], mm_dtype)] + mats["tw_fwd"]
    in_specs = [pl.BlockSpec((1, DFT_MINOR, a_used, LANES), lambda bb, cb, qb: (bb, 0, 0, cb)),
                const((2 * n_major, a_used)), const((2 * DFT_MINOR, 2 * DFT_MINOR)),
                const((DFT_MINOR // 8, n_major, LANES)), const((DFT_MINOR // 8, n_major, LANES)),
                const((8, n_major, LANES)), const((8, n_major, LANES))]
    spec_shape = (Bb, n_major, DFT_MINOR, C)
    if mode == "mul":
        args += [h[0], h[1]]
        hblk = pl.BlockSpec((1, DFT_QB, DFT_MINOR, LANES), lambda bb, cb, qb: (0, qb, 0, cb))
        in_specs += [hblk, hblk]
        out_specs = [spec_blk, spec_blk]
        out_shape = [jax.ShapeDtypeStruct(spec_shape, BF16)] * 2
    elif mode == "chanmix":
        args += [chan[0].astype(mm_dtype), chan[1].astype(mm_dtype)]
        in_specs += [const((LANES, LANES)), const((LANES, LANES))]
        out_specs = spec_blk
        out_shape = jax.ShapeDtypeStruct(spec_shape, F32)
    else:
        out_specs = [spec_blk, spec_blk]
        out_shape = [jax.ShapeDtypeStruct(spec_shape, F32)] * 2
    kern = functools.partial(_dft_fwd_kernel, n_major=n_major, a_used=a_used, mode=mode, precision=precision)
    return pl.pallas_call(
        kern,
        grid=(Bb, C // LANES, n_major // DFT_QB),
        in_specs=in_specs,
        out_specs=out_specs,
        out_shape=out_shape,
        scratch_shapes=[pltpu.VMEM((DFT_MINOR * pitch, LANES), F32)] * 2,
        compiler_params=_cparams("arbitrary", "arbitrary", "arbitrary"),
        name="dft_forward_" + mode,
    )(*args)


def _dft_inv_kernel(gr_ref, gi_ref, f2c_ref, f1c_ref, u1r, u1i, u0r, u0i, o_ref, sr, si, *, n_major, a_out):
    pitch = DFT_MINOR + DFT_PITCH_PAD
    qb = pl.program_id(2)

    def freq(j, carry):
        q = qb * DFT_QB + j
        g = jnp.concatenate([gr_ref[0, j], gi_ref[0, j]], axis=0)
        t = jnp.dot(f2c_ref[...], g, preferred_element_type=F32)
        twr, twi = _cmul(u1r[q // 8], u1i[q // 8], u0r[q % 8], u0i[q % 8])
        tr, ti = _cmul(t[:DFT_MINOR], t[DFT_MINOR:], twr, -twi)
        row = pl.multiple_of(q * pitch, 8)
        sr[pl.ds(row, DFT_MINOR), :] = tr
        si[pl.ds(row, DFT_MINOR), :] = ti
        return carry
    lax.fori_loop(0, DFT_QB, freq, 0)

    @pl.when(qb == pl.num_programs(2) - 1)
    def _stage2():
        def slab(b, carry):
            g = jnp.concatenate([sr[pl.ds(b, n_major, stride=pitch), :],
                                 si[pl.ds(b, n_major, stride=pitch), :]], axis=0).astype(BF16)
            o_ref[0, b] = jnp.dot(f1c_ref[...], g, preferred_element_type=F32)
            return carry
        lax.fori_loop(0, DFT_MINOR, slab, 0)


def dft_inverse_real(gr, gi, a_out):
    Bb, n_major, _, C = gr.shape
    N = n_major * DFT_MINOR
    mats = _dft_mats(n_major, a_out)
    ang = 2 * np.pi * np.outer(np.arange(a_out), np.arange(n_major)) / n_major
    f1c = np.concatenate([np.cos(ang), -np.sin(ang)], 1) / N
    pitch = DFT_MINOR + DFT_PITCH_PAD
    const = lambda shape: pl.BlockSpec(shape, lambda bb, cb, qb: (0,) * len(shape))
    gblk = pl.BlockSpec((1, DFT_QB, DFT_MINOR, LANES), lambda bb, cb, qb: (bb, qb, 0, cb))
    kern = functools.partial(_dft_inv_kernel, n_major=n_major, a_out=a_out)
    return pl.pallas_call(
        kern,
        grid=(Bb, C // LANES, n_major // DFT_QB),
        in_specs=[gblk, gblk, const((2 * DFT_MINOR, 2 * DFT_MINOR)), const((a_out, 2 * n_major)),
                  const((n_major // 8, DFT_MINOR, LANES)), const((n_major // 8, DFT_MINOR, LANES)),
                  const((8, DFT_MINOR, LANES)), const((8, DFT_MINOR, LANES))],
        out_specs=pl.BlockSpec((1, DFT_MINOR, a_out, LANES), lambda bb, cb, qb: (bb, 0, 0, cb)),
        out_shape=jax.ShapeDtypeStruct((Bb, DFT_MINOR, a_out, C), F32),
        scratch_shapes=[pltpu.VMEM((n_major * pitch, LANES), F32)] * 2,
        compiler_params=_cparams("arbitrary", "arbitrary", "arbitrary"),
        name="dft_inverse",
    )(gr, gi, jnp.asarray(mats["f2c"], BF16), jnp.asarray(f1c, BF16), *mats["tw_inv"])


def _to_slabs(x, a_used):
    Bb, _, C = x.shape
    return x.reshape(Bb, a_used, DFT_MINOR, C).transpose(0, 2, 1, 3)


def _from_slabs(xs):
    Bb, m, a, C = xs.shape
    return xs.transpose(0, 2, 1, 3).reshape(Bb, a * m, C)


def _channel_dft_mats(L):
    c = np.arange(FN_GROUP_DIM)
    ang = -2 * np.pi * np.outer(c, c) / FN_GROUP_DIM
    eye = np.eye(LANES // FN_GROUP_DIM)
    norm = 1.0 / math.sqrt(L * FN_GROUP_DIM)
    return (jnp.asarray(np.kron(eye, np.cos(ang)) * norm, F32),
            jnp.asarray(np.kron(eye, -np.sin(ang)) * norm, F32))


def fourier_mix_pallas(u):
    B, L, C = u.shape
    n_major = L // DFT_MINOR
    y = dft_forward(_to_slabs(u, n_major), n_major, mode="chanmix", chan=_channel_dft_mats(L))
    return y.transpose(0, 2, 1, 3).reshape(B, L, C)


def long_conv_pallas(vx, filt):
    B, L, D = vx.shape
    n_major = 2 * L // DFT_MINOR
    h = dft_forward(_to_slabs(filt[None], n_major), n_major, mode="spectrum", precise=True)
    gr, gi = dft_forward(_to_slabs(vx, n_major // 2), n_major, mode="mul", h=h)
    return _from_slabs(dft_inverse_real(gr, gi, n_major // 2))


def fourier_mix(u):
    B, L, _ = u.shape
    if L % (DFT_MINOR * DFT_QB) == 0:
        return fourier_mix_pallas(u.astype(F32))
    ug = u.astype(F32).reshape(B, L, FN_GROUPS, FN_GROUP_DIM)
    y = jnp.fft.fft2(ug, axes=(1, 3), norm='ortho').real
    return y.reshape(B, L, FN_WIDTH)


def hyena_filters(L, w1, b1, freq, w2, b2, w3):
    hp = lax.Precision.HIGHEST
    t = jnp.linspace(0.0, 1.0, L, dtype=F32)[:, None]
    w = 2 * math.pi * jnp.arange(L, dtype=F32)[:, None] / L
    bands = jnp.linspace(1e-4, HY_BANDS - 1, HY_BANDS, dtype=F32)[None, :]
    z = jnp.concatenate([t, jnp.cos(bands * w), -jnp.sin(bands * w)], axis=-1)
    hid = jnp.sin(freq * (jnp.dot(z, w1, precision=hp) + b1))
    hid = jnp.sin(freq * (jnp.dot(hid, w2, precision=hp) + b2))
    h = jnp.dot(hid, w3, precision=hp).reshape(L, 2, D_MODEL)
    deltas = jnp.abs(jnp.linspace(HY_MIN_DECAY, HY_MAX_DECAY, D_MODEL, dtype=F32))
    window = jnp.exp(-t * deltas[None, :])
    h = h * window[:, None, :]
    filt = jnp.concatenate([h[:, 0], jnp.zeros((1, D_MODEL), F32), h[1:, 1][::-1]], axis=0)
    return filt / jnp.sum(jnp.abs(filt), axis=0, keepdims=True)


def hyena_core(u, conv_w, conv_b, filt, skip):
    B, L, _ = u.shape
    up = jnp.pad(u, ((0, 0), (1, 1), (0, 0)))
    u = up[:, :-2] * conv_w[0] + up[:, 1:-1] * conv_w[1] + up[:, 2:] * conv_w[2] + conv_b
    x0, x1, v = jnp.split(u, 3, axis=-1)
    vx = v * x1
    if (2 * L) % (DFT_MINOR * DFT_QB) == 0:
        y = long_conv_pallas(vx, filt)
    else:
        y = jnp.fft.irfft(jnp.fft.rfft(vx, n=2 * L, axis=1) * jnp.fft.rfft(filt, n=2 * L, axis=0)[None],
                          n=2 * L, axis=1)[:, :L]
    return x0 * (y + vx * skip)


def kernel(x, c, ctx, c_ctx, mod_w, mod_b, norm_mix_g, norm_ffn_g, mix_w_in, na_rpb, mix_w_out,
           hy_w_in, hy_conv_w, hy_conv_b, hy_f_w1, hy_f_b1, hy_f_freq, hy_f_w2, hy_f_b2, hy_f_w3,
           hy_skip, hy_w_out, router_w, exp_w_gate, exp_w_up, exp_w_down, final_norm_g):
    depth = mod_w.shape[0]
    B, L, D = x.shape
    Lc = ctx.shape[1]
    last_ctx_read = depth - 1 if (depth - 1) % 2 == 0 else depth - 2
    TM = 512

    cs = ctx
    cond = jnp.concatenate([jax.nn.silu(c), jax.nn.silu(c_ctx)[None], jnp.zeros((8 - (B + 1) % 8, D), F32)], 0)

    for layer in range(depth):
        j = layer // 2
        m_all = small_matmul(cond, mod_w[layer], mod_b[layer])
        sh1, sc1, g1, sh2, sc2, g2 = jnp.split(m_all[:B], 6, axis=-1)
        m_ctx = jnp.broadcast_to(m_all[B:B + 1], (B, 6 * D))
        csh1, csc1, cg1, csh2, csc2, cg2 = jnp.split(m_ctx, 6, axis=-1)
        upd = layer < last_ctx_read
        wg = exp_w_gate[layer].astype(BF16)
        wu = exp_w_up[layer].astype(BF16)
        wd = exp_w_down[layer].astype(BF16)

        if layer % 2 == 0:
            w_in = mix_w_in[j].astype(BF16)
            w_out = mix_w_out[j].astype(BF16)
            if upd:
                pc = norm_proj(cs, norm_mix_g[layer], csh1, csc1, w_in, BF16, TM)
                kc_off = HEAD_PAIRS
                cmix_in = jnp.concatenate([ctx_attention(pc), fourier_mix(pc[..., 3 * NA_WIDTH:]).astype(BF16)], -1)
            else:
                pc = norm_proj(cs, norm_mix_g[layer], csh1, csc1, w_in[:, NA_WIDTH:3 * NA_WIDTH], BF16, TM)
                kc_off = 0
            p = norm_proj(x, norm_mix_g[layer], sh1, sc1, w_in, BF16, TM)
            att = na_attention(p, pc, kc_off, na_rpb[j])
            mix_in = jnp.concatenate([att, fourier_mix(p[..., 3 * NA_WIDTH:]).astype(BF16)], -1)
            x = proj_residual(mix_in, w_out, x, g1, TM)
            if upd:
                cs = proj_residual(cmix_in, w_out, cs, cg1, TM)
        else:
            w_in = hy_w_in[j].astype(BF16)
            w_out = hy_w_out[j].astype(BF16)
            fargs = (hy_f_w1[j], hy_f_b1[j], hy_f_freq[j], hy_f_w2[j], hy_f_b2[j], hy_f_w3[j])
            u = norm_proj(x, norm_mix_g[layer], sh1, sc1, w_in, F32, TM)
            y = hyena_core(u, hy_conv_w[j], hy_conv_b[j], hyena_filters(L, *fargs), hy_skip[j])
            x = proj_residual(y, w_out, x, g1, TM)
            if upd:
                uc = norm_proj(cs, norm_mix_g[layer], csh1, csc1, w_in, F32, TM)
                yc = hyena_core(uc, hy_conv_w[j], hy_conv_b[j], hyena_filters(Lc, *fargs), hy_skip[j])
                cs = proj_residual(yc, w_out, cs, cg1, TM)

        x = expert_choice_ffn(x, norm_ffn_g[layer], sh2, sc2, g2, router_w[layer], wg, wu, wd, TM)
        if upd:
            cs = expert_choice_ffn(cs, norm_ffn_g[layer], csh2, csc2, cg2, router_w[layer], wg, wu, wd, TM)

    return rms_norm_final(x, final_norm_g)
```

```python
import functools
import math

import numpy as np
import jax
import jax.numpy as jnp
from jax import lax
from jax.experimental import pallas as pl
from jax.experimental.pallas import tpu as pltpu

D_MODEL = 1024
GRID_W = 64
HEAD_DIM = 64
NA_HEADS = 12
NA_WIDTH = NA_HEADS * HEAD_DIM
NA_WIN_ROWS = 8
NA_WIN_COLS = 16
FN_GROUPS = 4
FN_GROUP_DIM = 64
FN_WIDTH = FN_GROUPS * FN_GROUP_DIM
HY_SHORT = 3
HY_EMB = 33
HY_BANDS = (HY_EMB - 1) // 2
HY_MIN_DECAY = math.log(1e-2) / 1.5
HY_MAX_DECAY = math.log(1e-2) / 0.3
N_EXPERTS = 16
EC_CAPACITY_FACTOR = 2
RMS_EPS = 1e-6

LANES = 128
HEAD_PAIRS = NA_WIDTH // LANES
NA_Q_ROWS = 2
NEG_INF = -1e30
VMEM_LIMIT = 56 * 1024 * 1024

BF16 = jnp.bfloat16
F32 = jnp.float32


def _cparams(*sem):
    return pltpu.CompilerParams(dimension_semantics=sem, vmem_limit_bytes=VMEM_LIMIT)


def _small_matmul_kernel(a_ref, w_ref, b_ref, o_ref):
    o_ref[...] = jnp.dot(a_ref[...], w_ref[...], preferred_element_type=F32,
                         precision=lax.Precision.HIGHEST) + b_ref[...]


def small_matmul(a, w, b, tn=1536):
    M, K = a.shape
    N = w.shape[1]
    return pl.pallas_call(
        _small_matmul_kernel,
        grid=(N // tn,),
        in_specs=[pl.BlockSpec((M, K), lambda j: (0, 0)),
                  pl.BlockSpec((K, tn), lambda j: (0, j)),
                  pl.BlockSpec((1, tn), lambda j: (0, j))],
        out_specs=pl.BlockSpec((M, tn), lambda j: (0, j)),
        out_shape=jax.ShapeDtypeStruct((M, N), F32),
        compiler_params=_cparams("arbitrary"),
        name="small_matmul",
    )(a, w, b.reshape(1, N))


def _norm_mod(x, g, shift, scale):
    ms = jnp.mean(x * x, axis=-1, keepdims=True)
    y = x * lax.rsqrt(ms + RMS_EPS) * g
    return y * (1.0 + scale) + shift


def _norm_proj_kernel(x_ref, g_ref, sh_ref, sc_ref, w_ref, o_ref):
    h = _norm_mod(x_ref[0], g_ref[...], sh_ref[0], sc_ref[0])
    o_ref[0] = jnp.dot(h.astype(BF16), w_ref[...], preferred_element_type=F32).astype(o_ref.dtype)


def norm_proj(x, g, shift, scale, w, out_dtype, tm):
    B, L, D = x.shape
    N = w.shape[1]
    tm = min(tm, L)
    return pl.pallas_call(
        _norm_proj_kernel,
        grid=(B, L // tm),
        in_specs=[pl.BlockSpec((1, tm, D), lambda b, i: (b, i, 0)),
                  pl.BlockSpec((1, D), lambda b, i: (0, 0)),
                  pl.BlockSpec((1, 1, D), lambda b, i: (b, 0, 0)),
                  pl.BlockSpec((1, 1, D), lambda b, i: (b, 0, 0)),
                  pl.BlockSpec((D, N), lambda b, i: (0, 0))],
        out_specs=pl.BlockSpec((1, tm, N), lambda b, i: (b, i, 0)),
        out_shape=jax.ShapeDtypeStruct((B, L, N), out_dtype),
        compiler_params=_cparams("arbitrary", "arbitrary"),
        name="norm_proj",
    )(x, g.reshape(1, D), shift.reshape(B, 1, D), scale.reshape(B, 1, D), w)


def _norm_router_kernel(x_ref, g_ref, sh_ref, sc_ref, rw_ref, h_ref, aff_ref):
    h = _norm_mod(x_ref[0], g_ref[...], sh_ref[0], sc_ref[0])
    h_ref[0] = h.astype(h_ref.dtype)
    logits = jnp.dot(h, rw_ref[...], preferred_element_type=F32, precision=lax.Precision.HIGHEST)
    m = jnp.max(logits, axis=-1, keepdims=True)
    e = jnp.exp(logits - m)
    aff_ref[0] = e / jnp.sum(e, axis=-1, keepdims=True)


def norm_router(x, g, shift, scale, router_w, tm, h_dtype):
    B, L, D = x.shape
    E = router_w.shape[1]
    tm = min(tm, L)
    return pl.pallas_call(
        _norm_router_kernel,
        grid=(B, L // tm),
        in_specs=[pl.BlockSpec((1, tm, D), lambda b, i: (b, i, 0)),
                  pl.BlockSpec((1, D), lambda b, i: (0, 0)),
                  pl.BlockSpec((1, 1, D), lambda b, i: (b, 0, 0)),
                  pl.BlockSpec((1, 1, D), lambda b, i: (b, 0, 0)),
                  pl.BlockSpec((D, E), lambda b, i: (0, 0))],
        out_specs=[pl.BlockSpec((1, tm, D), lambda b, i: (b, i, 0)),
                   pl.BlockSpec((1, tm, E), lambda b, i: (b, i, 0))],
        out_shape=[jax.ShapeDtypeStruct((B, L, D), h_dtype),
                   jax.ShapeDtypeStruct((B, L, E), F32)],
        compiler_params=_cparams("arbitrary", "arbitrary"),
        name="norm_router",
    )(x, g.reshape(1, D), shift.reshape(B, 1, D), scale.reshape(B, 1, D), router_w)


def _proj_residual_kernel(y_ref, w_ref, x_ref, gate_ref, o_ref):
    mix = jnp.dot(y_ref[0].astype(BF16), w_ref[...], preferred_element_type=F32)
    o_ref[0] = x_ref[0] + gate_ref[0] * mix


def proj_residual(y, w, x, gate, tm):
    B, L, K = y.shape
    D = w.shape[1]
    tm = min(tm, L)
    return pl.pallas_call(
        _proj_residual_kernel,
        grid=(B, L // tm),
        in_specs=[pl.BlockSpec((1, tm, K), lambda b, i: (b, i, 0)),
                  pl.BlockSpec((K, D), lambda b, i: (0, 0)),
                  pl.BlockSpec((1, tm, D), lambda b, i: (b, i, 0)),
                  pl.BlockSpec((1, 1, D), lambda b, i: (b, 0, 0))],
        out_specs=pl.BlockSpec((1, tm, D), lambda b, i: (b, i, 0)),
        out_shape=jax.ShapeDtypeStruct((B, L, D), F32),
        compiler_params=_cparams("arbitrary", "arbitrary"),
        name="proj_residual",
    )(y, w, x, gate.reshape(B, 1, D))


def _rms_kernel(x_ref, g_ref, o_ref):
    x = x_ref[0]
    ms = jnp.mean(x * x, axis=-1, keepdims=True)
    o_ref[0] = x * lax.rsqrt(ms + RMS_EPS) * g_ref[...]


def rms_norm_final(x, g, tm=1024):
    B, L, D = x.shape
    return pl.pallas_call(
        _rms_kernel,
        grid=(B, L // tm),
        in_specs=[pl.BlockSpec((1, tm, D), lambda b, i: (b, i, 0)),
                  pl.BlockSpec((1, D), lambda b, i: (0, 0))],
        out_specs=pl.BlockSpec((1, tm, D), lambda b, i: (b, i, 0)),
        out_shape=jax.ShapeDtypeStruct((B, L, D), F32),
        compiler_params=_cparams("arbitrary", "arbitrary"),
        name="rms_final",
    )(x, g.reshape(1, D))


def _two_head_rows(q):
    lane = lax.broadcasted_iota(jnp.int32, q.shape, 1)
    zero = jnp.zeros_like(q)
    return jnp.concatenate([jnp.where(lane < HEAD_DIM, q, zero),
                            jnp.where(lane >= HEAD_DIM, q, zero)], axis=0)


def _merge_two_heads(o):
    n = o.shape[0] // 2
    lane = lax.broadcasted_iota(jnp.int32, (n, LANES), 1)
    return jnp.where(lane < HEAD_DIM, o[:n], o[n:])


_CONTRACT_LAST = (((1,), (1,)), ((), ()))


def _na_key_rows(rq):
    n = rq + NA_WIN_ROWS - 1
    return n + n % 2


def _na_kernel(q_ref, k_ref, v_ref, kc_ref, vc_ref, bias_ref, o_ref, *, rows, rq):
    nq = rq * GRID_W
    nkr = _na_key_rows(rq)
    kwin = nkr * GRID_W
    kc = kc_ref[0]
    vc = vc_ref[0]

    def step(i, carry):
        r = i * rq
        start = jnp.clip(r - NA_WIN_ROWS // 2, 0, rows - nkr)
        variant = r - start
        q0 = pl.multiple_of(r * GRID_W, nq)
        k0 = pl.multiple_of(start * GRID_W, GRID_W)
        q2 = _two_head_rows(q_ref[0, pl.ds(q0, nq), :])
        kw = k_ref[0, pl.ds(k0, kwin), :]
        vw = v_ref[0, pl.ds(k0, kwin), :]
        s_loc = lax.dot_general(q2, kw, _CONTRACT_LAST, preferred_element_type=F32)
        s_loc = s_loc * (HEAD_DIM ** -0.5) + bias_ref[0, variant]
        s_ctx = lax.dot_general(q2, kc, _CONTRACT_LAST, preferred_element_type=F32) * (HEAD_DIM ** -0.5)
        m = jnp.maximum(jnp.max(s_loc, axis=-1, keepdims=True), jnp.max(s_ctx, axis=-1, keepdims=True))
        p_loc = jnp.exp(s_loc - m)
        p_ctx = jnp.exp(s_ctx - m)
        denom = jnp.sum(p_loc, axis=-1, keepdims=True) + jnp.sum(p_ctx, axis=-1, keepdims=True)
        o = (jnp.dot(p_loc.astype(BF16), vw, preferred_element_type=F32)
             + jnp.dot(p_ctx.astype(BF16), vc, preferred_element_type=F32))
        o = o / denom
        o_ref[0, pl.ds(q0, nq), :] = _merge_two_heads(o).astype(o_ref.dtype)
        return carry

    lax.fori_loop(0, rows // rq, step, 0)


def _na_bias_tables(rpb, rows, rq):
    W = GRID_W
    nkr = _na_key_rows(rq)
    n_var = nkr - rq + 1
    cols = np.arange(W)
    col_start = np.clip(cols - NA_WIN_COLS // 2, 0, W - NA_WIN_COLS)
    ro = np.zeros((n_var, rq * W, nkr * W), np.int32)
    co = np.zeros_like(ro)
    ok = np.zeros(ro.shape, bool)
    for var in range(n_var):
        r = var if var <= NA_WIN_ROWS // 2 else rows - nkr + var
        start = r - var
        for dr in range(rq):
            q_row = r + dr
            if q_row >= rows:
                continue
            rs = int(np.clip(q_row - NA_WIN_ROWS // 2, 0, rows - NA_WIN_ROWS))
            key_row = start + np.arange(nkr)
            row_ok = (key_row >= rs) & (key_row < rs + NA_WIN_ROWS)
            row_off = np.clip(key_row - q_row + NA_WIN_ROWS - 1, 0, 2 * NA_WIN_ROWS - 2)
            col_ok = (cols[None, :] >= col_start[:, None]) & (cols[None, :] < col_start[:, None] + NA_WIN_COLS)
            col_off = np.clip(cols[None, :] - cols[:, None] + NA_WIN_COLS - 1, 0, 2 * NA_WIN_COLS - 2)
            sl = slice(dr * W, (dr + 1) * W)
            ro[var, sl] = np.broadcast_to(row_off[None, :, None], (W, nkr, W)).reshape(W, nkr * W)
            co[var, sl] = np.broadcast_to(col_off[:, None, :], (W, nkr, W)).reshape(W, nkr * W)
            ok[var, sl] = (row_ok[None, :, None] & col_ok[:, None, :]).reshape(W, nkr * W)
    tab = rpb.astype(F32)[:, ro, co]
    tab = jnp.where(ok[None], tab, NEG_INF)
    tab = tab.reshape(HEAD_PAIRS, 2, n_var, rq * W, nkr * W).transpose(0, 2, 1, 3, 4)
    return tab.reshape(HEAD_PAIRS, n_var, 2 * rq * W, nkr * W)


def na_attention(p, pc, kc_off, rpb):
    B, L, _ = p.shape
    Lc = pc.shape[1]
    rows = L // GRID_W
    rq = NA_Q_ROWS
    bias = _na_bias_tables(rpb, rows, rq)
    nkr = _na_key_rows(rq)
    kern = functools.partial(_na_kernel, rows=rows, rq=rq)
    return pl.pallas_call(
        kern,
        grid=(B, HEAD_PAIRS),
        in_specs=[pl.BlockSpec((1, L, LANES), lambda b, j: (b, 0, j)),
                  pl.BlockSpec((1, L, LANES), lambda b, j: (b, 0, HEAD_PAIRS + j)),
                  pl.BlockSpec((1, L, LANES), lambda b, j: (b, 0, 2 * HEAD_PAIRS + j)),
                  pl.BlockSpec((1, Lc, LANES), lambda b, j: (b, 0, kc_off + j)),
                  pl.BlockSpec((1, Lc, LANES), lambda b, j: (b, 0, kc_off + HEAD_PAIRS + j)),
                  pl.BlockSpec((1, nkr - rq + 1, 2 * rq * GRID_W, nkr * GRID_W), lambda b, j: (j, 0, 0, 0))],
        out_specs=pl.BlockSpec((1, L, LANES), lambda b, j: (b, 0, j)),
        out_shape=jax.ShapeDtypeStruct((B, L, NA_WIDTH), BF16),
        compiler_params=_cparams("arbitrary", "arbitrary"),
        name="na_attention",
    )(p, p, p, pc, pc, bias)


def _ctx_attn_kernel(q_ref, k_ref, v_ref, o_ref):
    q2 = _two_head_rows(q_ref[0])
    s = lax.dot_general(q2, k_ref[0], _CONTRACT_LAST, preferred_element_type=F32) * (HEAD_DIM ** -0.5)
    m = jnp.max(s, axis=-1, keepdims=True)
    e = jnp.exp(s - m)
    o = jnp.dot(e.astype(BF16), v_ref[0], preferred_element_type=F32) / jnp.sum(e, axis=-1, keepdims=True)
    o_ref[0] = _merge_two_heads(o).astype(o_ref.dtype)


def ctx_attention(pc):
    B, Lc, _ = pc.shape
    return pl.pallas_call(
        _ctx_attn_kernel,
        grid=(B, HEAD_PAIRS),
        in_specs=[pl.BlockSpec((1, Lc, LANES), lambda b, j: (b, 0, j)),
                  pl.BlockSpec((1, Lc, LANES), lambda b, j: (b, 0, HEAD_PAIRS + j)),
                  pl.BlockSpec((1, Lc, LANES), lambda b, j: (b, 0, 2 * HEAD_PAIRS + j))],
        out_specs=pl.BlockSpec((1, Lc, LANES), lambda b, j: (b, 0, j)),
        out_shape=jax.ShapeDtypeStruct((B, Lc, NA_WIDTH), BF16),
        compiler_params=_cparams("arbitrary", "arbitrary"),
        name="ctx_attention",
    )(pc, pc, pc)


def _expert_ffn_kernel(x_ref, gate_ref, wg_ref, wu_ref, wd_ref, o_ref, *, fchunk):
    o_ref[0, 0] = _swiglu(x_ref[0, 0], wg_ref, wu_ref, wd_ref, fchunk) * gate_ref[0, 0]


def expert_ffn(xe, gate, wg, wu, wd, fchunk=512):
    E, B, cap, D = xe.shape
    F = wg.shape[2]
    kern = functools.partial(_expert_ffn_kernel, fchunk=fchunk)
    return pl.pallas_call(
        kern,
        grid=(E, B),
        in_specs=[pl.BlockSpec((1, 1, cap, D), lambda e, b: (e, b, 0, 0)),
                  pl.BlockSpec((1, 1, cap, 1), lambda e, b: (e, b, 0, 0)),
                  pl.BlockSpec((1, D, F), lambda e, b: (e, 0, 0)),
                  pl.BlockSpec((1, D, F), lambda e, b: (e, 0, 0)),
                  pl.BlockSpec((1, F, D), lambda e, b: (e, 0, 0))],
        out_specs=pl.BlockSpec((1, 1, cap, D), lambda e, b: (e, b, 0, 0)),
        out_shape=jax.ShapeDtypeStruct((E, B, cap, D), F32),
        compiler_params=_cparams("arbitrary", "arbitrary"),
        name="expert_ffn",
    )(xe, gate, wg, wu, wd)


def _swiglu(x, wg_ref, wu_ref, wd_ref, fchunk):
    F = wg_ref.shape[2]
    acc = jnp.zeros((x.shape[0], wd_ref.shape[2]), F32)
    for f0 in range(0, F, fchunk):
        g = jnp.dot(x, wg_ref[0, :, f0:f0 + fchunk], preferred_element_type=F32)
        u = jnp.dot(x, wu_ref[0, :, f0:f0 + fchunk], preferred_element_type=F32)
        hid = (g * jax.nn.sigmoid(g) * u).astype(BF16)
        acc = acc + jnp.dot(hid, wd_ref[0, f0:f0 + fchunk, :], preferred_element_type=F32)
    return acc


def _gather_rows_copy(h_hbm, xbuf, sem, slot, row, j):
    return pltpu.make_async_copy(h_hbm.at[pl.ds(row, 1)], xbuf.at[slot, pl.ds(j, 1)], sem.at[slot])


def _expert_ffn_gather_kernel(rows_ref, rows_next_ref, gate_ref, h_hbm, wg_ref, wu_ref, wd_ref, o_ref,
                              xbuf, sem, *, fchunk):
    cap = xbuf.shape[1]
    step = pl.program_id(0) * pl.num_programs(1) + pl.program_id(1)
    nsteps = pl.num_programs(0) * pl.num_programs(1)
    slot = step % 2

    def issue(ref, to_slot):
        def body(j, carry):
            _gather_rows_copy(h_hbm, xbuf, sem, to_slot, ref[0, 0, j], j).start()
            return carry
        lax.fori_loop(0, cap, body, 0)

    @pl.when(step == 0)
    def _():
        issue(rows_ref, 0)

    @pl.when(step + 1 < nsteps)
    def _():
        issue(rows_next_ref, 1 - slot)

    pltpu.make_async_copy(h_hbm.at[pl.ds(0, cap)], xbuf.at[slot], sem.at[slot]).wait()
    y = _swiglu(xbuf[slot].astype(BF16), wg_ref, wu_ref, wd_ref, fchunk)
    o_ref[0, 0] = (y * gate_ref[0, 0]).astype(o_ref.dtype)


def expert_ffn_gather(h_rows, rows, gate, wg, wu, wd, fchunk=512):
    E, B, cap = rows.shape
    D = h_rows.shape[1]
    F = wg.shape[2]

    rows = rows.reshape(E * B, 1, cap)

    def next_group(e, b):
        return (jnp.minimum(e * B + b + 1, E * B - 1), 0, 0)

    kern = functools.partial(_expert_ffn_gather_kernel, fchunk=fchunk)
    return pl.pallas_call(
        kern,
        grid=(E, B),
        in_specs=[pl.BlockSpec((1, 1, cap), lambda e, b: (e * B + b, 0, 0), memory_space=pltpu.SMEM),
                  pl.BlockSpec((1, 1, cap), next_group, memory_space=pltpu.SMEM),
                  pl.BlockSpec((1, 1, cap, 1), lambda e, b: (e, b, 0, 0)),
                  pl.BlockSpec(memory_space=pl.ANY),
                  pl.BlockSpec((1, D, F), lambda e, b: (e, 0, 0)),
                  pl.BlockSpec((1, D, F), lambda e, b: (e, 0, 0)),
                  pl.BlockSpec((1, F, D), lambda e, b: (e, 0, 0))],
        out_specs=pl.BlockSpec((1, 1, cap, D), lambda e, b: (e, b, 0, 0)),
        out_shape=jax.ShapeDtypeStruct((E, B, cap, D), BF16),
        scratch_shapes=[pltpu.VMEM((2, cap, D), F32), pltpu.SemaphoreType.DMA((2,))],
        compiler_params=_cparams("arbitrary", "arbitrary"),
        name="expert_ffn_gather",
    )(rows, rows, gate, h_rows, wg, wu, wd)


COMBINE_TOKENS = 512
COMBINE_WINDOW = 256
COMBINE_CHUNK = LANES


def _window_copy(ye_hbm, win, sem, e, b, w0, slot):
    return pltpu.make_async_copy(ye_hbm.at[e, b, pl.ds(w0, COMBINE_WINDOW)], win.at[slot], sem.at[slot])


def _moe_combine_kernel(starts_ref, tok_ref, x_ref, gate_ref, ye_hbm, o_ref, acc_ref, win, sem):
    b = pl.program_id(0)
    i = pl.program_id(1)
    n_exp = tok_ref.shape[0]
    cap = ye_hbm.shape[2]
    T = x_ref.shape[1]
    tile0 = i * T
    tok_iota = lax.broadcasted_iota(jnp.int32, (T, COMBINE_CHUNK), 0) + tile0

    def first_window(e):
        s = starts_ref[b, e, i]
        return jnp.minimum((s // COMBINE_CHUNK) * COMBINE_CHUNK, cap - COMBINE_WINDOW)

    def add_window(e, w0, slot, first_row=None):
        k = w0 // COMBINE_CHUNK
        hot = []
        for c in range(COMBINE_WINDOW // COMBINE_CHUNK):
            match = tok_iota == tok_ref[e, 0, pl.ds(k + c, 1), :]
            if first_row is not None:
                row = lax.broadcasted_iota(jnp.int32, match.shape, 1) + (w0 + c * COMBINE_CHUNK)
                match = match & (row >= first_row)
            hot.append(match.astype(BF16))
        acc_ref[...] += jnp.dot(jnp.concatenate(hot, axis=1), win[slot], preferred_element_type=F32)

    acc_ref[...] = jnp.zeros_like(acc_ref)
    _window_copy(ye_hbm, win, sem, 0, b, first_window(0), 0).start()

    def per_expert(e, carry):
        slot = e % 2
        w0 = first_window(e)

        @pl.when(e + 1 < n_exp)
        def _():
            _window_copy(ye_hbm, win, sem, e + 1, b, first_window(e + 1), 1 - slot).start()

        _window_copy(ye_hbm, win, sem, e, b, w0, slot).wait()
        add_window(e, w0, slot)

        end = starts_ref[b, e, i + 1]
        n_more = jnp.maximum(end - (w0 + COMBINE_WINDOW) + COMBINE_WINDOW - 1, 0) // COMBINE_WINDOW

        def more(m, c):
            first_row = w0 + (m + 1) * COMBINE_WINDOW
            w = jnp.minimum(first_row, cap - COMBINE_WINDOW)
            cp = _window_copy(ye_hbm, win, sem, e, b, w, 2)
            cp.start()
            cp.wait()
            add_window(e, w, 2, first_row)
            return c
        lax.fori_loop(0, n_more, more, 0)
        return carry

    lax.fori_loop(0, n_exp, per_expert, 0)
    o_ref[0] = x_ref[0] + gate_ref[0] * acc_ref[...]


def moe_combine(x, gate_res, ye, tok):
    B, n, D = x.shape
    E, _, cap, _ = ye.shape
    T = COMBINE_TOKENS
    nt = n // T
    bounds = jnp.arange(nt + 1, dtype=jnp.int32) * T
    starts = jnp.sum(tok[:, :, None, :] < bounds[None, None, :, None], axis=-1, dtype=jnp.int32)
    starts = jnp.transpose(starts, (1, 0, 2))
    tok4 = tok.reshape(E, B, cap // COMBINE_CHUNK, COMBINE_CHUNK)
    grid_spec = pltpu.PrefetchScalarGridSpec(
        num_scalar_prefetch=1,
        grid=(B, nt),
        in_specs=[pl.BlockSpec((E, 1, cap // COMBINE_CHUNK, COMBINE_CHUNK), lambda b, i, s: (0, b, 0, 0)),
                  pl.BlockSpec((1, T, D), lambda b, i, s: (b, i, 0)),
                  pl.BlockSpec((1, 1, D), lambda b, i, s: (b, 0, 0)),
                  pl.BlockSpec(memory_space=pl.ANY)],
        out_specs=pl.BlockSpec((1, T, D), lambda b, i, s: (b, i, 0)),
        scratch_shapes=[pltpu.VMEM((T, D), F32), pltpu.VMEM((3, COMBINE_WINDOW, D), BF16),
                        pltpu.SemaphoreType.DMA((3,))])
    return pl.pallas_call(
        _moe_combine_kernel,
        grid_spec=grid_spec,
        out_shape=jax.ShapeDtypeStruct((B, n, D), F32),
        compiler_params=_cparams("arbitrary", "arbitrary"),
        name="moe_combine",
    )(starts, tok4, x, gate_res.reshape(B, 1, D), ye)


def expert_choice_ffn(x, g, shift, scale, gate_res, router_w, wg, wu, wd, tm):
    B, n, D = x.shape
    E = N_EXPERTS
    cap = EC_CAPACITY_FACTOR * n // E
    fused = cap % COMBINE_WINDOW == 0 and cap > COMBINE_WINDOW and n % COMBINE_TOKENS == 0
    h, aff = norm_router(x, g, shift, scale, router_w, tm, F32 if fused else BF16)
    gate, idx = lax.top_k(jnp.transpose(aff, (2, 0, 1)), cap)
    bidx = jnp.arange(B, dtype=idx.dtype)[None, :, None]
    if fused:
        idx, gate = lax.sort((idx, gate), dimension=2, num_keys=1)
        ye = expert_ffn_gather(h.reshape(B * n, D), idx + bidx * n, gate[..., None], wg, wu, wd)
        return moe_combine(x, gate_res, ye, idx)
    xe = h[bidx, idx]
    ye = expert_ffn(xe, gate[..., None], wg, wu, wd)
    flat = (idx + bidx * n).reshape(-1)
    moe = jnp.zeros((B * n, D), F32).at[flat].add(ye.reshape(-1, D)).reshape(B, n, D)
    return x + gate_res[:, None, :] * moe


DFT_MINOR = 128
DFT_PITCH_PAD = 8
DFT_QB = 16


def _dft_mats(n_major, a_used):
    N = n_major * DFT_MINOR
    q = np.arange(n_major)
    a = np.arange(a_used)
    b = np.arange(DFT_MINOR)
    ang1 = -2 * np.pi * np.outer(q, a) / n_major
    f1 = np.concatenate([np.cos(ang1), np.sin(ang1)], 0)
    ang2 = -2 * np.pi * np.outer(b, b) / DFT_MINOR
    c2, s2 = np.cos(ang2), np.sin(ang2)
    f2 = np.block([[c2, -s2], [s2, c2]])
    f2c = np.block([[c2, s2], [-s2, c2]])
    lane1 = np.ones((1, 1, LANES))

    def tw(hi, lo):
        ang = -2 * np.pi * np.outer(hi, lo) / N
        return (np.cos(ang)[:, :, None] * lane1, np.sin(ang)[:, :, None] * lane1)

    t1r, t1i = tw(8 * np.arange(DFT_MINOR // 8), q)
    t0r, t0i = tw(np.arange(8), q)
    u1r, u1i = tw(8 * np.arange(n_major // 8), b)
    u0r, u0i = tw(np.arange(8), b)
    f32 = lambda *xs: [np.asarray(x, np.float32) for x in xs]
    return dict(f1=f1, f2=f2, f2c=f2c, tw_fwd=f32(t1r, t1i, t0r, t0i), tw_inv=f32(u1r, u1i, u0r, u0i))


def _cmul(ar, ai, br, bi):
    return ar * br - ai * bi, ar * bi + ai * br


def _dft_fwd_kernel(*refs, n_major, a_used, mode, precision):
    x_ref, f1_ref, f2_ref, t1r, t1i, t0r, t0i = refs[:7]
    rest = refs[7:]
    if mode == "mul":
        hr_ref, hi_ref, or_ref, oi_ref, sr, si = rest
    elif mode == "chanmix":
        mc_ref, ms_ref, or_ref, sr, si = rest
    else:
        or_ref, oi_ref, sr, si = rest
    pitch = n_major + DFT_PITCH_PAD
    qb = pl.program_id(2)
    mm_dtype = f1_ref.dtype

    @pl.when(qb == 0)
    def _stage1():
        def slab(b, carry):
            y = jnp.dot(f1_ref[...], x_ref[0, b].astype(mm_dtype), preferred_element_type=F32,
                        precision=precision)
            twr, twi = _cmul(t1r[b // 8], t1i[b // 8], t0r[b % 8], t0i[b % 8])
            yr, yi = _cmul(y[:n_major], y[n_major:], twr, twi)
            row = pl.multiple_of(b * pitch, 8)
            sr[pl.ds(row, n_major), :] = yr
            si[pl.ds(row, n_major), :] = yi
            return carry
        lax.fori_loop(0, DFT_MINOR, slab, 0)

    def freq(j, carry):
        q = qb * DFT_QB + j
        g = jnp.concatenate([sr[pl.ds(q, DFT_MINOR, stride=pitch), :],
                             si[pl.ds(q, DFT_MINOR, stride=pitch), :]], axis=0).astype(mm_dtype)
        xf = jnp.dot(f2_ref[...], g, preferred_element_type=F32, precision=precision)
        xr, xi = xf[:DFT_MINOR], xf[DFT_MINOR:]
        if mode == "mul":
            xr, xi = _cmul(xr, xi, hr_ref[0, j], hi_ref[0, j])
            or_ref[0, j] = xr.astype(or_ref.dtype)
            oi_ref[0, j] = xi.astype(oi_ref.dtype)
        elif mode == "chanmix":
            or_ref[0, j] = (jnp.dot(xr.astype(mm_dtype), mc_ref[...], preferred_element_type=F32)
                            + jnp.dot(xi.astype(mm_dtype), ms_ref[...], preferred_element_type=F32))
        else:
            or_ref[0, j] = xr
            oi_ref[0, j] = xi
        return carry
    lax.fori_loop(0, DFT_QB, freq, 0)


def dft_forward(xp, n_major, mode="spectrum", h=None, chan=None, precise=False):
    Bb, _, a_used, C = xp.shape
    mats = _dft_mats(n_major, a_used)
    mm_dtype = F32 if precise else BF16
    precision = lax.Precision.HIGHEST if precise else None
    pitch = n_major + DFT_PITCH_PAD
    const = lambda shape: pl.BlockSpec(shape, lambda bb, cb, qb: (0,) * len(shape))
    spec_blk = pl.BlockSpec((1, DFT_QB, DFT_MINOR, LANES), lambda bb, cb, qb: (bb, qb, 0, cb))
    args = [xp, jnp.asarray(mats["f1"], mm_dtype), jnp.asarray(mats["f2"], mm_dtype)] + mats["tw_fwd"]
    in_specs = [pl.BlockSpec((1, DFT_MINOR, a_used, LANES), lambda bb, cb, qb: (bb, 0, 0, cb)),
                const((2 * n_major, a_used)), const((2 * DFT_MINOR, 2 * DFT_MINOR)),
                const((DFT_MINOR // 8, n_major, LANES)), const((DFT_MINOR // 8, n_major, LANES)),
                const((8, n_major, LANES)), const((8, n_major, LANES))]
    spec_shape = (Bb, n_major, DFT_MINOR, C)
    if mode == "mul":
        args += [h[0], h[1]]
        hblk = pl.BlockSpec((1, DFT_QB, DFT_MINOR, LANES), lambda bb, cb, qb: (0, qb, 0, cb))
        in_specs += [hblk, hblk]
        out_specs = [spec_blk, spec_blk]
        out_shape = [jax.ShapeDtypeStruct(spec_shape, BF16)] * 2
    elif mode == "chanmix":
        args += [chan[0].astype(mm_dtype), chan[1].astype(mm_dtype)]
        in_specs += [const((LANES, LANES)), const((LANES, LANES))]
        out_specs = spec_blk
        out_shape = jax.ShapeDtypeStruct(spec_shape, F32)
    else:
        out_specs = [spec_blk, spec_blk]
        out_shape = [jax.ShapeDtypeStruct(spec_shape, F32)] * 2
    kern = functools.partial(_dft_fwd_kernel, n_major=n_major, a_used=a_used, mode=mode, precision=precision)
    return pl.pallas_call(
        kern,
        grid=(Bb, C // LANES, n_major // DFT_QB),
        in_specs=in_specs,
        out_specs=out_specs,
        out_shape=out_shape,
        scratch_shapes=[pltpu.VMEM((DFT_MINOR * pitch, LANES), F32)] * 2,
        compiler_params=_cparams("arbitrary", "arbitrary", "arbitrary"),
        name="dft_forward_" + mode,
    )(*args)


def _dft_inv_kernel(gr_ref, gi_ref, f2c_ref, f1c_ref, u1r, u1i, u0r, u0i, o_ref, sr, si, *, n_major, a_out):
    pitch = DFT_MINOR + DFT_PITCH_PAD
    qb = pl.program_id(2)

    def freq(j, carry):
        q = qb * DFT_QB + j
        g = jnp.concatenate([gr_ref[0, j], gi_ref[0, j]], axis=0)
        t = jnp.dot(f2c_ref[...], g, preferred_element_type=F32)
        twr, twi = _cmul(u1r[q // 8], u1i[q // 8], u0r[q % 8], u0i[q % 8])
        tr, ti = _cmul(t[:DFT_MINOR], t[DFT_MINOR:], twr, -twi)
        row = pl.multiple_of(q * pitch, 8)
        sr[pl.ds(row, DFT_MINOR), :] = tr
        si[pl.ds(row, DFT_MINOR), :] = ti
        return carry
    lax.fori_loop(0, DFT_QB, freq, 0)

    @pl.when(qb == pl.num_programs(2) - 1)
    def _stage2():
        def slab(b, carry):
            g = jnp.concatenate([sr[pl.ds(b, n_major, stride=pitch), :],
                                 si[pl.ds(b, n_major, stride=pitch), :]], axis=0).astype(BF16)
            o_ref[0, b] = jnp.dot(f1c_ref[...], g, preferred_element_type=F32)
            return carry
        lax.fori_loop(0, DFT_MINOR, slab, 0)


def dft_inverse_real(gr, gi, a_out):
    Bb, n_major, _, C = gr.shape
    N = n_major * DFT_MINOR
    mats = _dft_mats(n_major, a_out)
    ang = 2 * np.pi * np.outer(np.arange(a_out), np.arange(n_major)) / n_major
    f1c = np.concatenate([np.cos(ang), -np.sin(ang)], 1) / N
    pitch = DFT_MINOR + DFT_PITCH_PAD
    const = lambda shape: pl.BlockSpec(shape, lambda bb, cb, qb: (0,) * len(shape))
    gblk = pl.BlockSpec((1, DFT_QB, DFT_MINOR, LANES), lambda bb, cb, qb: (bb, qb, 0, cb))
    kern = functools.partial(_dft_inv_kernel, n_major=n_major, a_out=a_out)
    return pl.pallas_call(
        kern,
        grid=(Bb, C // LANES, n_major // DFT_QB),
        in_specs=[gblk, gblk, const((2 * DFT_MINOR, 2 * DFT_MINOR)), const((a_out, 2 * n_major)),
                  const((n_major // 8, DFT_MINOR, LANES)), const((n_major // 8, DFT_MINOR, LANES)),
                  const((8, DFT_MINOR, LANES)), const((8, DFT_MINOR, LANES))],
        out_specs=pl.BlockSpec((1, DFT_MINOR, a_out, LANES), lambda bb, cb, qb: (bb, 0, 0, cb)),
        out_shape=jax.ShapeDtypeStruct((Bb, DFT_MINOR, a_out, C), F32),
        scratch_shapes=[pltpu.VMEM((n_major * pitch, LANES), F32)] * 2,
        compiler_params=_cparams("arbitrary", "arbitrary", "arbitrary"),
        name="dft_inverse",
    )(gr, gi, jnp.asarray(mats["f2c"], BF16), jnp.asarray(f1c, BF16), *mats["tw_inv"])


def _to_slabs(x, a_used):
    Bb, _, C = x.shape
    return x.reshape(Bb, a_used, DFT_MINOR, C).transpose(0, 2, 1, 3)


def _from_slabs(xs):
    Bb, m, a, C = xs.shape
    return xs.transpose(0, 2, 1, 3).reshape(Bb, a * m, C)


def _channel_dft_mats(L):
    c = np.arange(FN_GROUP_DIM)
    ang = -2 * np.pi * np.outer(c, c) / FN_GROUP_DIM
    eye = np.eye(LANES // FN_GROUP_DIM)
    norm = 1.0 / math.sqrt(L * FN_GROUP_DIM)
    return (jnp.asarray(np.kron(eye, np.cos(ang)) * norm, F32),
            jnp.asarray(np.kron(eye, -np.sin(ang)) * norm, F32))


def fourier_mix_pallas(u):
    B, L, C = u.shape
    n_major = L // DFT_MINOR
    y = dft_forward(_to_slabs(u, n_major), n_major, mode="chanmix", chan=_channel_dft_mats(L))
    return y.transpose(0, 2, 1, 3).reshape(B, L, C)


def long_conv_pallas(vx, filt):
    B, L, D = vx.shape
    n_major = 2 * L // DFT_MINOR
    h = dft_forward(_to_slabs(filt[None], n_major), n_major, mode="spectrum", precise=True)
    gr, gi = dft_forward(_to_slabs(vx, n_major // 2), n_major, mode="mul", h=h)
    return _from_slabs(dft_inverse_real(gr, gi, n_major // 2))


def fourier_mix(u):
    B, L, _ = u.shape
    if L % (DFT_MINOR * DFT_QB) == 0:
        return fourier_mix_pallas(u.astype(F32))
    ug = u.astype(F32).reshape(B, L, FN_GROUPS, FN_GROUP_DIM)
    y = jnp.fft.fft2(ug, axes=(1, 3), norm='ortho').real
    return y.reshape(B, L, FN_WIDTH)


def hyena_filters(L, w1, b1, freq, w2, b2, w3):
    hp = lax.Precision.HIGHEST
    t = jnp.linspace(0.0, 1.0, L, dtype=F32)[:, None]
    w = 2 * math.pi * jnp.arange(L, dtype=F32)[:, None] / L
    bands = jnp.linspace(1e-4, HY_BANDS - 1, HY_BANDS, dtype=F32)[None, :]
    z = jnp.concatenate([t, jnp.cos(bands * w), -jnp.sin(bands * w)], axis=-1)
    hid = jnp.sin(freq * (jnp.dot(z, w1, precision=hp) + b1))
    hid = jnp.sin(freq * (jnp.dot(hid, w2, precision=hp) + b2))
    h = jnp.dot(hid, w3, precision=hp).reshape(L, 2, D_MODEL)
    deltas = jnp.abs(jnp.linspace(HY_MIN_DECAY, HY_MAX_DECAY, D_MODEL, dtype=F32))
    window = jnp.exp(-t * deltas[None, :])
    h = h * window[:, None, :]
    filt = jnp.concatenate([h[:, 0], jnp.zeros((1, D_MODEL), F32), h[1:, 1][::-1]], axis=0)
    return filt / jnp.sum(jnp.abs(filt), axis=0, keepdims=True)


def hyena_core(u, conv_w, conv_b, filt, skip):
    B, L, _ = u.shape
    up = jnp.pad(u, ((0, 0), (1, 1), (0, 0)))
    u = up[:, :-2] * conv_w[0] + up[:, 1:-1] * conv_w[1] + up[:, 2:] * conv_w[2] + conv_b
    x0, x1, v = jnp.split(u, 3, axis=-1)
    vx = v * x1
    if (2 * L) % (DFT_MINOR * DFT_QB) == 0:
        y = long_conv_pallas(vx, filt)
    else:
        y = jnp.fft.irfft(jnp.fft.rfft(vx, n=2 * L, axis=1) * jnp.fft.rfft(filt, n=2 * L, axis=0)[None],
                          n=2 * L, axis=1)[:, :L]
    return x0 * (y + vx * skip)


def kernel(x, c, ctx, c_ctx, mod_w, mod_b, norm_mix_g, norm_ffn_g, mix_w_in, na_rpb, mix_w_out,
           hy_w_in, hy_conv_w, hy_conv_b, hy_f_w1, hy_f_b1, hy_f_freq, hy_f_w2, hy_f_b2, hy_f_w3,
           hy_skip, hy_w_out, router_w, exp_w_gate, exp_w_up, exp_w_down, final_norm_g):
    depth = mod_w.shape[0]
    B, L, D = x.shape
    Lc = ctx.shape[1]
    last_ctx_read = depth - 1 if (depth - 1) % 2 == 0 else depth - 2
    TM = 512

    cs = ctx
    cond = jnp.concatenate([jax.nn.silu(c), jax.nn.silu(c_ctx)[None], jnp.zeros((8 - (B + 1) % 8, D), F32)], 0)

    for layer in range(depth):
        j = layer // 2
        m_all = small_matmul(cond, mod_w[layer], mod_b[layer])
        sh1, sc1, g1, sh2, sc2, g2 = jnp.split(m_all[:B], 6, axis=-1)
        m_ctx = jnp.broadcast_to(m_all[B:B + 1], (B, 6 * D))
        csh1, csc1, cg1, csh2, csc2, cg2 = jnp.split(m_ctx, 6, axis=-1)
        upd = layer < last_ctx_read
        wg = exp_w_gate[layer].astype(BF16)
        wu = exp_w_up[layer].astype(BF16)
        wd = exp_w_down[layer].astype(BF16)

        if layer % 2 == 0:
            w_in = mix_w_in[j].astype(BF16)
            w_out = mix_w_out[j].astype(BF16)
            if upd:
                pc = norm_proj(cs, norm_mix_g[layer], csh1, csc1, w_in, BF16, TM)
                kc_off = HEAD_PAIRS
                cmix_in = jnp.concatenate([ctx_attention(pc), fourier_mix(pc[..., 3 * NA_WIDTH:]).astype(BF16)], -1)
            else:
                pc = norm_proj(cs, norm_mix_g[layer], csh1, csc1, w_in[:, NA_WIDTH:3 * NA_WIDTH], BF16, TM)
                kc_off = 0
            p = norm_proj(x, norm_mix_g[layer], sh1, sc1, w_in, BF16, TM)
            att = na_attention(p, pc, kc_off, na_rpb[j])
            mix_in = jnp.concatenate([att, fourier_mix(p[..., 3 * NA_WIDTH:]).astype(BF16)], -1)
            x = proj_residual(mix_in, w_out, x, g1, TM)
            if upd:
                cs = proj_residual(cmix_in, w_out, cs, cg1, TM)
        else:
            w_in = hy_w_in[j].astype(BF16)
            w_out = hy_w_out[j].astype(BF16)
            fargs = (hy_f_w1[j], hy_f_b1[j], hy_f_freq[j], hy_f_w2[j], hy_f_b2[j], hy_f_w3[j])
            u = norm_proj(x, norm_mix_g[layer], sh1, sc1, w_in, F32, TM)
            y = hyena_core(u, hy_conv_w[j], hy_conv_b[j], hyena_filters(L, *fargs), hy_skip[j])
            x = proj_residual(y, w_out, x, g1, TM)
            if upd:
                uc = norm_proj(cs, norm_mix_g[layer], csh1, csc1, w_in, F32, TM)
                yc = hyena_core(uc, hy_conv_w[j], hy_conv_b[j], hyena_filters(Lc, *fargs), hy_skip[j])
                cs = proj_residual(yc, w_out, cs, cg1, TM)

        x = expert_choice_ffn(x, norm_ffn_g[layer], sh2, sc2, g2, router_w[layer], wg, wu, wd, TM)
        if upd:
            cs = expert_choice_ffn(cs, norm_ffn_g[layer], csh2, csc2, cg2, router_w[layer], wg, wu, wd, TM)

    return rms_norm_final(x, final_norm_g)
```

```python
import functools
import math

import numpy as np
import jax
import jax.numpy as jnp
from jax import lax
from jax.experimental import pallas as pl
from jax.experimental.pallas import tpu as pltpu

D_MODEL = 1024
GRID_W = 64
HEAD_DIM = 64
NA_HEADS = 12
NA_WIDTH = NA_HEADS * HEAD_DIM
NA_WIN_ROWS = 8
NA_WIN_COLS = 16
FN_GROUPS = 4
FN_GROUP_DIM = 64
FN_WIDTH = FN_GROUPS * FN_GROUP_DIM
HY_SHORT = 3
HY_EMB = 33
HY_BANDS = (HY_EMB - 1) // 2
HY_MIN_DECAY = math.log(1e-2) / 1.5
HY_MAX_DECAY = math.log(1e-2) / 0.3
N_EXPERTS = 16
EC_CAPACITY_FACTOR = 2
RMS_EPS = 1e-6

LANES = 128
HEAD_PAIRS = NA_WIDTH // LANES
NA_Q_ROWS = 2
NEG_INF = -1e30
VMEM_LIMIT = 56 * 1024 * 1024

BF16 = jnp.bfloat16
F32 = jnp.float32


def _cparams(*sem):
    return pltpu.CompilerParams(dimension_semantics=sem, vmem_limit_bytes=VMEM_LIMIT)


def _small_matmul_kernel(a_ref, w_ref, b_ref, o_ref):
    o_ref[...] = jnp.dot(a_ref[...], w_ref[...], preferred_element_type=F32,
                         precision=lax.Precision.HIGHEST) + b_ref[...]


def small_matmul(a, w, b, tn=1536):
    M, K = a.shape
    N = w.shape[1]
    return pl.pallas_call(
        _small_matmul_kernel,
        grid=(N // tn,),
        in_specs=[pl.BlockSpec((M, K), lambda j: (0, 0)),
                  pl.BlockSpec((K, tn), lambda j: (0, j)),
                  pl.BlockSpec((1, tn), lambda j: (0, j))],
        out_specs=pl.BlockSpec((M, tn), lambda j: (0, j)),
        out_shape=jax.ShapeDtypeStruct((M, N), F32),
        compiler_params=_cparams("arbitrary"),
        name="small_matmul",
    )(a, w, b.reshape(1, N))


def _norm_mod(x, g, shift, scale):
    ms = jnp.mean(x * x, axis=-1, keepdims=True)
    y = x * lax.rsqrt(ms + RMS_EPS) * g
    return y * (1.0 + scale) + shift


def _norm_proj_kernel(x_ref, g_ref, sh_ref, sc_ref, w_ref, o_ref):
    h = _norm_mod(x_ref[0], g_ref[...], sh_ref[0], sc_ref[0])
    o_ref[0] = jnp.dot(h.astype(BF16), w_ref[...], preferred_element_type=F32).astype(o_ref.dtype)


def norm_proj(x, g, shift, scale, w, out_dtype, tm):
    B, L, D = x.shape
    N = w.shape[1]
    tm = min(tm, L)
    return pl.pallas_call(
        _norm_proj_kernel,
        grid=(B, L // tm),
        in_specs=[pl.BlockSpec((1, tm, D), lambda b, i: (b, i, 0)),
                  pl.BlockSpec((1, D), lambda b, i: (0, 0)),
                  pl.BlockSpec((1, 1, D), lambda b, i: (b, 0, 0)),
                  pl.BlockSpec((1, 1, D), lambda b, i: (b, 0, 0)),
                  pl.BlockSpec((D, N), lambda b, i: (0, 0))],
        out_specs=pl.BlockSpec((1, tm, N), lambda b, i: (b, i, 0)),
        out_shape=jax.ShapeDtypeStruct((B, L, N), out_dtype),
        compiler_params=_cparams("arbitrary", "arbitrary"),
        name="norm_proj",
    )(x, g.reshape(1, D), shift.reshape(B, 1, D), scale.reshape(B, 1, D), w)


def _norm_router_kernel(x_ref, g_ref, sh_ref, sc_ref, rw_ref, h_ref, aff_ref):
    h = _norm_mod(x_ref[0], g_ref[...], sh_ref[0], sc_ref[0])
    h_ref[0] = h.astype(h_ref.dtype)
    logits = jnp.dot(h, rw_ref[...], preferred_element_type=F32, precision=lax.Precision.HIGHEST)
    m = jnp.max(logits, axis=-1, keepdims=True)
    e = jnp.exp(logits - m)
    aff_ref[0] = e / jnp.sum(e, axis=-1, keepdims=True)


def norm_router(x, g, shift, scale, router_w, tm, h_dtype):
    B, L, D = x.shape
    E = router_w.shape[1]
    tm = min(tm, L)
    return pl.pallas_call(
        _norm_router_kernel,
        grid=(B, L // tm),
        in_specs=[pl.BlockSpec((1, tm, D), lambda b, i: (b, i, 0)),
                  pl.BlockSpec((1, D), lambda b, i: (0, 0)),
                  pl.BlockSpec((1, 1, D), lambda b, i: (b, 0, 0)),
                  pl.BlockSpec((1, 1, D), lambda b, i: (b, 0, 0)),
                  pl.BlockSpec((D, E), lambda b, i: (0, 0))],
        out_specs=[pl.BlockSpec((1, tm, D), lambda b, i: (b, i, 0)),
                   pl.BlockSpec((1, tm, E), lambda b, i: (b, i, 0))],
        out_shape=[jax.ShapeDtypeStruct((B, L, D), h_dtype),
                   jax.ShapeDtypeStruct((B, L, E), F32)],
        compiler_params=_cparams("arbitrary", "arbitrary"),
        name="norm_router",
    )(x, g.reshape(1, D), shift.reshape(B, 1, D), scale.reshape(B, 1, D), router_w)


def _proj_residual_kernel(y_ref, w_ref, x_ref, gate_ref, o_ref):
    mix = jnp.dot(y_ref[0].astype(BF16), w_ref[...], preferred_element_type=F32)
    o_ref[0] = x_ref[0] + gate_ref[0] * mix


def proj_residual(y, w, x, gate, tm):
    B, L, K = y.shape
    D = w.shape[1]
    tm = min(tm, L)
    return pl.pallas_call(
        _proj_residual_kernel,
        grid=(B, L // tm),
        in_specs=[pl.BlockSpec((1, tm, K), lambda b, i: (b, i, 0)),
                  pl.BlockSpec((K, D), lambda b, i: (0, 0)),
                  pl.BlockSpec((1, tm, D), lambda b, i: (b, i, 0)),
                  pl.BlockSpec((1, 1, D), lambda b, i: (b, 0, 0))],
        out_specs=pl.BlockSpec((1, tm, D), lambda b, i: (b, i, 0)),
        out_shape=jax.ShapeDtypeStruct((B, L, D), F32),
        compiler_params=_cparams("arbitrary", "arbitrary"),
        name="proj_residual",
    )(y, w, x, gate.reshape(B, 1, D))


def _rms_kernel(x_ref, g_ref, o_ref):
    x = x_ref[0]
    ms = jnp.mean(x * x, axis=-1, keepdims=True)
    o_ref[0] = x * lax.rsqrt(ms + RMS_EPS) * g_ref[...]


def rms_norm_final(x, g, tm=1024):
    B, L, D = x.shape
    return pl.pallas_call(
        _rms_kernel,
        grid=(B, L // tm),
        in_specs=[pl.BlockSpec((1, tm, D), lambda b, i: (b, i, 0)),
                  pl.BlockSpec((1, D), lambda b, i: (0, 0))],
        out_specs=pl.BlockSpec((1, tm, D), lambda b, i: (b, i, 0)),
        out_shape=jax.ShapeDtypeStruct((B, L, D), F32),
        compiler_params=_cparams("arbitrary", "arbitrary"),
        name="rms_final",
    )(x, g.reshape(1, D))


def _two_head_rows(q):
    lane = lax.broadcasted_iota(jnp.int32, q.shape, 1)
    zero = jnp.zeros_like(q)
    return jnp.concatenate([jnp.where(lane < HEAD_DIM, q, zero),
                            jnp.where(lane >= HEAD_DIM, q, zero)], axis=0)


def _merge_two_heads(o):
    n = o.shape[0] // 2
    lane = lax.broadcasted_iota(jnp.int32, (n, LANES), 1)
    return jnp.where(lane < HEAD_DIM, o[:n], o[n:])


_CONTRACT_LAST = (((1,), (1,)), ((), ()))


def _na_key_rows(rq):
    n = rq + NA_WIN_ROWS - 1
    return n + n % 2


def _na_kernel(q_ref, k_ref, v_ref, kc_ref, vc_ref, bias_ref, o_ref, *, rows, rq):
    nq = rq * GRID_W
    nkr = _na_key_rows(rq)
    kwin = nkr * GRID_W
    kc = kc_ref[0]
    vc = vc_ref[0]

    def step(i, carry):
        r = i * rq
        start = jnp.clip(r - NA_WIN_ROWS // 2, 0, rows - nkr)
        variant = r - start
        q0 = pl.multiple_of(r * GRID_W, nq)
        k0 = pl.multiple_of(start * GRID_W, GRID_W)
        q2 = _two_head_rows(q_ref[0, pl.ds(q0, nq), :])
        kw = k_ref[0, pl.ds(k0, kwin), :]
        vw = v_ref[0, pl.ds(k0, kwin), :]
        s_loc = lax.dot_general(q2, kw, _CONTRACT_LAST, preferred_element_type=F32)
        s_loc = s_loc * (HEAD_DIM ** -0.5) + bias_ref[0, variant]
        s_ctx = lax.dot_general(q2, kc, _CONTRACT_LAST, preferred_element_type=F32) * (HEAD_DIM ** -0.5)
        m = jnp.maximum(jnp.max(s_loc, axis=-1, keepdims=True), jnp.max(s_ctx, axis=-1, keepdims=True))
        p_loc = jnp.exp(s_loc - m)
        p_ctx = jnp.exp(s_ctx - m)
        denom = jnp.sum(p_loc, axis=-1, keepdims=True) + jnp.sum(p_ctx, axis=-1, keepdims=True)
        o = (jnp.dot(p_loc.astype(BF16), vw, preferred_element_type=F32)
             + jnp.dot(p_ctx.astype(BF16), vc, preferred_element_type=F32))
        o = o / denom
        o_ref[0, pl.ds(q0, nq), :] = _merge_two_heads(o).astype(o_ref.dtype)
        return carry

    lax.fori_loop(0, rows // rq, step, 0)


def _na_bias_tables(rpb, rows, rq):
    W = GRID_W
    nkr = _na_key_rows(rq)
    n_var = nkr - rq + 1
    n_ro, n_co = 2 * NA_WIN_ROWS - 1, 2 * NA_WIN_COLS - 1
    cols = np.arange(W)
    col_start = np.clip(cols - NA_WIN_COLS // 2, 0, W - NA_WIN_COLS)
    row_sel = np.zeros((n_var, rq, nkr, n_ro), np.float32)
    for var in range(n_var):
        r = var if var <= NA_WIN_ROWS // 2 else rows - nkr + var
        start = r - var
        for dr in range(rq):
            q_row = r + dr
            if q_row >= rows:
                continue
            rs = int(np.clip(q_row - NA_WIN_ROWS // 2, 0, rows - NA_WIN_ROWS))
            for kr in range(nkr):
                key_row = start + kr
                if rs <= key_row < rs + NA_WIN_ROWS:
                    row_sel[var, dr, kr, key_row - q_row + NA_WIN_ROWS - 1] = 1.0
    col_sel = np.zeros((n_co, W, W), np.float32)
    for q in range(W):
        for kc in range(col_start[q], col_start[q] + NA_WIN_COLS):
            col_sel[kc - q + NA_WIN_COLS - 1, q, kc] = 1.0
    inside = (row_sel.sum(-1)[:, :, :, None, None] * col_sel.sum(0)[None, None, None]) > 0
    hp = lax.Precision.HIGHEST
    by_row = jnp.einsum('vdko,hoc->hvdkc', row_sel, rpb.astype(F32), precision=hp)
    tab = jnp.einsum('hvdkc,cqn->hvdkqn', by_row, col_sel, precision=hp)
    tab = jnp.where(inside[None], tab, NEG_INF)
    tab = tab.reshape(HEAD_PAIRS, 2, n_var, rq, nkr, W, W).transpose(0, 2, 1, 3, 5, 4, 6)
    return tab.reshape(HEAD_PAIRS, n_var, 2 * rq * W, nkr * W)


def na_attention(p, pc, kc_off, rpb):
    B, L, _ = p.shape
    Lc = pc.shape[1]
    rows = L // GRID_W
    rq = NA_Q_ROWS
    bias = _na_bias_tables(rpb, rows, rq)
    nkr = _na_key_rows(rq)
    kern = functools.partial(_na_kernel, rows=rows, rq=rq)
    return pl.pallas_call(
        kern,
        grid=(B, HEAD_PAIRS),
        in_specs=[pl.BlockSpec((1, L, LANES), lambda b, j: (b, 0, j)),
                  pl.BlockSpec((1, L, LANES), lambda b, j: (b, 0, HEAD_PAIRS + j)),
                  pl.BlockSpec((1, L, LANES), lambda b, j: (b, 0, 2 * HEAD_PAIRS + j)),
                  pl.BlockSpec((1, Lc, LANES), lambda b, j: (b, 0, kc_off + j)),
                  pl.BlockSpec((1, Lc, LANES), lambda b, j: (b, 0, kc_off + HEAD_PAIRS + j)),
                  pl.BlockSpec((1, nkr - rq + 1, 2 * rq * GRID_W, nkr * GRID_W), lambda b, j: (j, 0, 0, 0))],
        out_specs=pl.BlockSpec((1, L, LANES), lambda b, j: (b, 0, j)),
        out_shape=jax.ShapeDtypeStruct((B, L, NA_WIDTH), BF16),
        compiler_params=_cparams("arbitrary", "arbitrary"),
        name="na_attention",
    )(p, p, p, pc, pc, bias)


def _ctx_attn_kernel(q_ref, k_ref, v_ref, o_ref):
    q2 = _two_head_rows(q_ref[0])
    s = lax.dot_general(q2, k_ref[0], _CONTRACT_LAST, preferred_element_type=F32) * (HEAD_DIM ** -0.5)
    m = jnp.max(s, axis=-1, keepdims=True)
    e = jnp.exp(s - m)
    o = jnp.dot(e.astype(BF16), v_ref[0], preferred_element_type=F32) / jnp.sum(e, axis=-1, keepdims=True)
    o_ref[0] = _merge_two_heads(o).astype(o_ref.dtype)


def ctx_attention(pc):
    B, Lc, _ = pc.shape
    return pl.pallas_call(
        _ctx_attn_kernel,
        grid=(B, HEAD_PAIRS),
        in_specs=[pl.BlockSpec((1, Lc, LANES), lambda b, j: (b, 0, j)),
                  pl.BlockSpec((1, Lc, LANES), lambda b, j: (b, 0, HEAD_PAIRS + j)),
                  pl.BlockSpec((1, Lc, LANES), lambda b, j: (b, 0, 2 * HEAD_PAIRS + j))],
        out_specs=pl.BlockSpec((1, Lc, LANES), lambda b, j: (b, 0, j)),
        out_shape=jax.ShapeDtypeStruct((B, Lc, NA_WIDTH), BF16),
        compiler_params=_cparams("arbitrary", "arbitrary"),
        name="ctx_attention",
    )(pc, pc, pc)


def _expert_ffn_kernel(x_ref, gate_ref, wg_ref, wu_ref, wd_ref, o_ref, *, fchunk):
    o_ref[0, 0] = _swiglu(x_ref[0, 0], wg_ref, wu_ref, wd_ref, fchunk) * gate_ref[0, 0]


def expert_ffn(xe, gate, wg, wu, wd, fchunk=512):
    E, B, cap, D = xe.shape
    F = wg.shape[2]
    kern = functools.partial(_expert_ffn_kernel, fchunk=fchunk)
    return pl.pallas_call(
        kern,
        grid=(E, B),
        in_specs=[pl.BlockSpec((1, 1, cap, D), lambda e, b: (e, b, 0, 0)),
                  pl.BlockSpec((1, 1, cap, 1), lambda e, b: (e, b, 0, 0)),
                  pl.BlockSpec((1, D, F), lambda e, b: (e, 0, 0)),
                  pl.BlockSpec((1, D, F), lambda e, b: (e, 0, 0)),
                  pl.BlockSpec((1, F, D), lambda e, b: (e, 0, 0))],
        out_specs=pl.BlockSpec((1, 1, cap, D), lambda e, b: (e, b, 0, 0)),
        out_shape=jax.ShapeDtypeStruct((E, B, cap, D), F32),
        compiler_params=_cparams("arbitrary", "arbitrary"),
        name="expert_ffn",
    )(xe, gate, wg, wu, wd)


def _swiglu(x, wg_ref, wu_ref, wd_ref, fchunk):
    F = wg_ref.shape[2]
    acc = jnp.zeros((x.shape[0], wd_ref.shape[2]), F32)
    for f0 in range(0, F, fchunk):
        g = jnp.dot(x, wg_ref[0, :, f0:f0 + fchunk], preferred_element_type=F32)
        u = jnp.dot(x, wu_ref[0, :, f0:f0 + fchunk], preferred_element_type=F32)
        hid = (g * jax.nn.sigmoid(g) * u).astype(BF16)
        acc = acc + jnp.dot(hid, wd_ref[0, f0:f0 + fchunk, :], preferred_element_type=F32)
    return acc


def _gather_rows_copy(h_hbm, xbuf, sem, slot, row, j):
    return pltpu.make_async_copy(h_hbm.at[pl.ds(row, 1)], xbuf.at[slot, pl.ds(j, 1)], sem.at[slot])


def _expert_ffn_gather_kernel(rows_ref, rows_next_ref, gate_ref, h_hbm, wg_ref, wu_ref, wd_ref, o_ref,
                              xbuf, sem, *, fchunk):
    cap = xbuf.shape[1]
    step = pl.program_id(0) * pl.num_programs(1) + pl.program_id(1)
    nsteps = pl.num_programs(0) * pl.num_programs(1)
    slot = step % 2

    def drain(s):
        pltpu.make_async_copy(h_hbm.at[pl.ds(0, cap)], xbuf.at[s], sem.at[s]).wait()

    @pl.when(step == 0)
    def _():
        def body(j, carry):
            _gather_rows_copy(h_hbm, xbuf, sem, 0, rows_ref[0, 0, j], j).start()
            return carry
        lax.fori_loop(0, cap, body, 0)

    drain(slot)
    x = xbuf[slot].astype(BF16)
    for j in range(cap):
        _gather_rows_copy(h_hbm, xbuf, sem, 1 - slot, rows_next_ref[0, 0, j], j).start()
    y = _swiglu(x, wg_ref, wu_ref, wd_ref, fchunk)
    o_ref[0, 0] = (y * gate_ref[0, 0]).astype(o_ref.dtype)

    @pl.when(step == nsteps - 1)
    def _():
        drain(1 - slot)


def expert_ffn_gather(h_rows, rows, gate, wg, wu, wd, fchunk=512):
    E, B, cap = rows.shape
    D = h_rows.shape[1]
    F = wg.shape[2]

    rows = rows.reshape(E * B, 1, cap)

    def next_group(e, b):
        return (jnp.minimum(e * B + b + 1, E * B - 1), 0, 0)

    kern = functools.partial(_expert_ffn_gather_kernel, fchunk=fchunk)
    return pl.pallas_call(
        kern,
        grid=(E, B),
        in_specs=[pl.BlockSpec((1, 1, cap), lambda e, b: (e * B + b, 0, 0), memory_space=pltpu.SMEM),
                  pl.BlockSpec((1, 1, cap), next_group, memory_space=pltpu.SMEM),
                  pl.BlockSpec((1, 1, cap, 1), lambda e, b: (e, b, 0, 0)),
                  pl.BlockSpec(memory_space=pl.ANY),
                  pl.BlockSpec((1, D, F), lambda e, b: (e, 0, 0)),
                  pl.BlockSpec((1, D, F), lambda e, b: (e, 0, 0)),
                  pl.BlockSpec((1, F, D), lambda e, b: (e, 0, 0))],
        out_specs=pl.BlockSpec((1, 1, cap, D), lambda e, b: (e, b, 0, 0)),
        out_shape=jax.ShapeDtypeStruct((E, B, cap, D), BF16),
        scratch_shapes=[pltpu.VMEM((2, cap, D), F32), pltpu.SemaphoreType.DMA((2,))],
        compiler_params=_cparams("arbitrary", "arbitrary"),
        name="expert_ffn_gather",
    )(rows, rows, gate, h_rows, wg, wu, wd)


COMBINE_TOKENS = 512
COMBINE_WINDOW = 256
COMBINE_CHUNK = LANES


def _window_copy(ye_hbm, win, sem, e, b, w0, slot):
    return pltpu.make_async_copy(ye_hbm.at[e, b, pl.ds(w0, COMBINE_WINDOW)], win.at[slot], sem.at[slot])


def _moe_combine_kernel(starts_ref, tok_ref, x_ref, gate_ref, ye_hbm, o_ref, acc_ref, win, sem):
    b = pl.program_id(0)
    i = pl.program_id(1)
    n_exp = tok_ref.shape[0]
    cap = ye_hbm.shape[2]
    T = x_ref.shape[1]
    tile0 = i * T
    tok_iota = lax.broadcasted_iota(jnp.int32, (T, COMBINE_CHUNK), 0) + tile0

    def first_window(e):
        s = starts_ref[b, e, i]
        return jnp.minimum((s // COMBINE_CHUNK) * COMBINE_CHUNK, cap - COMBINE_WINDOW)

    def one_hot(e, w0, first_row=None):
        k = w0 // COMBINE_CHUNK
        hot = []
        for c in range(COMBINE_WINDOW // COMBINE_CHUNK):
            match = tok_iota == tok_ref[e, 0, pl.ds(k + c, 1), :]
            if first_row is not None:
                row = lax.broadcasted_iota(jnp.int32, match.shape, 1) + (w0 + c * COMBINE_CHUNK)
                match = match & (row >= first_row)
            hot.append(match.astype(BF16))
        return hot

    firsts = [first_window(e) for e in range(n_exp)]
    for e in range(n_exp):
        _window_copy(ye_hbm, win, sem, e, b, firsts[e], e).start()
    hot = []
    for e in range(n_exp):
        hot += one_hot(e, firsts[e])
    for e in range(n_exp):
        _window_copy(ye_hbm, win, sem, e, b, firsts[e], e).wait()
    rows_all = win[pl.ds(0, n_exp)].reshape(n_exp * COMBINE_WINDOW, win.shape[2])
    acc_ref[...] = jnp.dot(jnp.concatenate(hot, axis=1), rows_all, preferred_element_type=F32)

    def per_expert(e, carry):
        w0 = first_window(e)
        end = starts_ref[b, e, i + 1]
        n_more = jnp.maximum(end - (w0 + COMBINE_WINDOW) + COMBINE_WINDOW - 1, 0) // COMBINE_WINDOW

        def more(m, c):
            first_row = w0 + (m + 1) * COMBINE_WINDOW
            w = jnp.minimum(first_row, cap - COMBINE_WINDOW)
            cp = _window_copy(ye_hbm, win, sem, e, b, w, n_exp)
            cp.start()
            cp.wait()
            sel = jnp.concatenate(one_hot(e, w, first_row), axis=1)
            acc_ref[...] += jnp.dot(sel, win[n_exp], preferred_element_type=F32)
            return c
        lax.fori_loop(0, n_more, more, 0)
        return carry

    lax.fori_loop(0, n_exp, per_expert, 0)
    o_ref[0] = x_ref[0] + gate_ref[0] * acc_ref[...]


def moe_combine(x, gate_res, ye, tok):
    B, n, D = x.shape
    E, _, cap, _ = ye.shape
    T = COMBINE_TOKENS
    nt = n // T
    bounds = jnp.arange(nt + 1, dtype=jnp.int32) * T
    starts = jnp.sum(tok[:, :, None, :] < bounds[None, None, :, None], axis=-1, dtype=jnp.int32)
    starts = jnp.transpose(starts, (1, 0, 2))
    tok4 = tok.reshape(E, B, cap // COMBINE_CHUNK, COMBINE_CHUNK)
    grid_spec = pltpu.PrefetchScalarGridSpec(
        num_scalar_prefetch=1,
        grid=(B, nt),
        in_specs=[pl.BlockSpec((E, 1, cap // COMBINE_CHUNK, COMBINE_CHUNK), lambda b, i, s: (0, b, 0, 0)),
                  pl.BlockSpec((1, T, D), lambda b, i, s: (b, i, 0)),
                  pl.BlockSpec((1, 1, D), lambda b, i, s: (b, 0, 0)),
                  pl.BlockSpec(memory_space=pl.ANY)],
        out_specs=pl.BlockSpec((1, T, D), lambda b, i, s: (b, i, 0)),
        scratch_shapes=[pltpu.VMEM((T, D), F32), pltpu.VMEM((E + 1, COMBINE_WINDOW, D), BF16),
                        pltpu.SemaphoreType.DMA((E + 1,))])
    return pl.pallas_call(
        _moe_combine_kernel,
        grid_spec=grid_spec,
        out_shape=jax.ShapeDtypeStruct((B, n, D), F32),
        compiler_params=_cparams("arbitrary", "arbitrary"),
        name="moe_combine",
    )(starts, tok4, x, gate_res.reshape(B, 1, D), ye)


def expert_choice_ffn(x, g, shift, scale, gate_res, router_w, wg, wu, wd, tm):
    B, n, D = x.shape
    E = N_EXPERTS
    cap = EC_CAPACITY_FACTOR * n // E
    fused = cap % COMBINE_WINDOW == 0 and cap > COMBINE_WINDOW and n % COMBINE_TOKENS == 0
    h, aff = norm_router(x, g, shift, scale, router_w, tm, F32 if fused else BF16)
    gate, idx = lax.top_k(jnp.transpose(aff, (2, 0, 1)), cap)
    bidx = jnp.arange(B, dtype=idx.dtype)[None, :, None]
    if fused:
        idx, gate = lax.sort((idx, gate), dimension=2, num_keys=1)
        ye = expert_ffn_gather(h.reshape(B * n, D), idx + bidx * n, gate[..., None], wg, wu, wd)
        return moe_combine(x, gate_res, ye, idx)
    xe = h[bidx, idx]
    ye = expert_ffn(xe, gate[..., None], wg, wu, wd)
    flat = (idx + bidx * n).reshape(-1)
    moe = jnp.zeros((B * n, D), F32).at[flat].add(ye.reshape(-1, D)).reshape(B, n, D)
    return x + gate_res[:, None, :] * moe


DFT_MINOR = 128
DFT_PITCH_PAD = 8
DFT_QB = 16


def _dft_mats(n_major, a_used):
    N = n_major * DFT_MINOR
    q = np.arange(n_major)
    a = np.arange(a_used)
    b = np.arange(DFT_MINOR)
    ang1 = -2 * np.pi * np.outer(q, a) / n_major
    f1 = np.concatenate([np.cos(ang1), np.sin(ang1)], 0)
    ang2 = -2 * np.pi * np.outer(b, b) / DFT_MINOR
    c2, s2 = np.cos(ang2), np.sin(ang2)
    f2 = np.block([[c2, -s2], [s2, c2]])
    f2c = np.block([[c2, s2], [-s2, c2]])
    lane1 = np.ones((1, 1, LANES))

    def tw(hi, lo):
        ang = -2 * np.pi * np.outer(hi, lo) / N
        return (np.cos(ang)[:, :, None] * lane1, np.sin(ang)[:, :, None] * lane1)

    t1r, t1i = tw(8 * np.arange(DFT_MINOR // 8), q)
    t0r, t0i = tw(np.arange(8), q)
    u1r, u1i = tw(8 * np.arange(n_major // 8), b)
    u0r, u0i = tw(np.arange(8), b)
    f32 = lambda *xs: [np.asarray(x, np.float32) for x in xs]
    return dict(f1=f1, f2=f2, f2c=f2c, tw_fwd=f32(t1r, t1i, t0r, t0i), tw_inv=f32(u1r, u1i, u0r, u0i))


def _cmul(ar, ai, br, bi):
    return ar * br - ai * bi, ar * bi + ai * br


def _dft_fwd_kernel(*refs, n_major, a_used, mode, precision):
    x_ref, f1_ref, f2_ref, t1r, t1i, t0r, t0i = refs[:7]
    rest = refs[7:]
    if mode == "mul":
        hr_ref, hi_ref, or_ref, oi_ref, sr, si = rest
    elif mode == "chanmix":
        mc_ref, ms_ref, or_ref, sr, si = rest
    else:
        or_ref, oi_ref, sr, si = rest
    pitch = n_major + DFT_PITCH_PAD
    qb = pl.program_id(2)
    mm_dtype = f1_ref.dtype

    @pl.when(qb == 0)
    def _stage1():
        def slab(b, carry):
            y = jnp.dot(f1_ref[...], x_ref[0, b].astype(mm_dtype), preferred_element_type=F32,
                        precision=precision)
            twr, twi = _cmul(t1r[b // 8], t1i[b // 8], t0r[b % 8], t0i[b % 8])
            yr, yi = _cmul(y[:n_major], y[n_major:], twr, twi)
            row = pl.multiple_of(b * pitch, 8)
            sr[pl.ds(row, n_major), :] = yr
            si[pl.ds(row, n_major), :] = yi
            return carry
        lax.fori_loop(0, DFT_MINOR, slab, 0)

    def freq(j, carry):
        q = qb * DFT_QB + j
        g = jnp.concatenate([sr[pl.ds(q, DFT_MINOR, stride=pitch), :],
                             si[pl.ds(q, DFT_MINOR, stride=pitch), :]], axis=0).astype(mm_dtype)
        xf = jnp.dot(f2_ref[...], g, preferred_element_type=F32, precision=precision)
        xr, xi = xf[:DFT_MINOR], xf[DFT_MINOR:]
        if mode == "mul":
            xr, xi = _cmul(xr, xi, hr_ref[0, j], hi_ref[0, j])
            or_ref[0, j] = xr.astype(or_ref.dtype)
            oi_ref[0, j] = xi.astype(oi_ref.dtype)
        elif mode == "chanmix":
            or_ref[0, j] = (jnp.dot(xr.astype(mm_dtype), mc_ref[...], preferred_element_type=F32)
                            + jnp.dot(xi.astype(mm_dtype), ms_ref[...], preferred_element_type=F32))
        else:
            or_ref[0, j] = xr
            oi_ref[0, j] = xi
        return carry
    lax.fori_loop(0, DFT_QB, freq, 0)


def dft_forward(xp, n_major, mode="spectrum", h=None, chan=None, precise=False):
    Bb, _, a_used, C = xp.shape
    mats = _dft_mats(n_major, a_used)
    mm_dtype = F32 if precise else BF16
    precision = lax.Precision.HIGHEST if precise else None
    pitch = n_major + DFT_PITCH_PAD
    const = lambda shape: pl.BlockSpec(shape, lambda bb, cb, qb: (0,) * len(shape))
    spec_blk = pl.BlockSpec((1, DFT_QB, DFT_MINOR, LANES), lambda bb, cb, qb: (bb, qb, 0, cb))
    args = [xp, jnp.asarray(mats["f1"], mm_dtype), jnp.asarray(mats["f2"], mm_dtype)] + mats["tw_fwd"]
    in_specs = [pl.BlockSpec((1, DFT_MINOR, a_used, LANES), lambda bb, cb, qb: (bb, 0, 0, cb)),
                const((2 * n_major, a_used)), const((2 * DFT_MINOR, 2 * DFT_MINOR)),
                const((DFT_MINOR // 8, n_major, LANES)), const((DFT_MINOR // 8, n_major, LANES)),
                const((8, n_major, LANES)), const((8, n_major, LANES))]
    spec_shape = (Bb, n_major, DFT_MINOR, C)
    if mode == "mul":
        args += [h[0], h[1]]
        hblk = pl.BlockSpec((1, DFT_QB, DFT_MINOR, LANES), lambda bb, cb, qb: (0, qb, 0, cb))
        in_specs += [hblk, hblk]
        out_specs = [spec_blk, spec_blk]
        out_shape = [jax.ShapeDtypeStruct(spec_shape, BF16)] * 2
    elif mode == "chanmix":
        args += [chan[0].astype(mm_dtype), chan[1].astype(mm_dtype)]
        in_specs += [const((LANES, LANES)), const((LANES, LANES))]
        out_specs = spec_blk
        out_shape = jax.ShapeDtypeStruct(spec_shape, F32)
    else:
        out_specs = [spec_blk, spec_blk]
        out_shape = [jax.ShapeDtypeStruct(spec_shape, F32)] * 2
    kern = functools.partial(_dft_fwd_kernel, n_major=n_major, a_used=a_used, mode=mode, precision=precision)
    return pl.pallas_call(
        kern,
        grid=(Bb, C // LANES, n_major // DFT_QB),
        in_specs=in_specs,
        out_specs=out_specs,
        out_shape=out_shape,
        scratch_shapes=[pltpu.VMEM((DFT_MINOR * pitch, LANES), F32)] * 2,
        compiler_params=_cparams("arbitrary", "arbitrary", "arbitrary"),
        name="dft_forward_" + mode,
    )(*args)


def _dft_inv_kernel(gr_ref, gi_ref, f2c_ref, f1c_ref, u1r, u1i, u0r, u0i, o_ref, sr, si, *, n_major, a_out):
    pitch = DFT_MINOR + DFT_PITCH_PAD
    qb = pl.program_id(2)

    def freq(j, carry):
        q = qb * DFT_QB + j
        g = jnp.concatenate([gr_ref[0, j], gi_ref[0, j]], axis=0)
        t = jnp.dot(f2c_ref[...], g, preferred_element_type=F32)
        twr, twi = _cmul(u1r[q // 8], u1i[q // 8], u0r[q % 8], u0i[q % 8])
        tr, ti = _cmul(t[:DFT_MINOR], t[DFT_MINOR:], twr, -twi)
        row = pl.multiple_of(q * pitch, 8)
        sr[pl.ds(row, DFT_MINOR), :] = tr
        si[pl.ds(row, DFT_MINOR), :] = ti
        return carry
    lax.fori_loop(0, DFT_QB, freq, 0)

    @pl.when(qb == pl.num_programs(2) - 1)
    def _stage2():
        def slab(b, carry):
            g = jnp.concatenate([sr[pl.ds(b, n_major, stride=pitch), :],
                                 si[pl.ds(b, n_major, stride=pitch), :]], axis=0).astype(BF16)
            o_ref[0, b] = jnp.dot(f1c_ref[...], g, preferred_element_type=F32)
            return carry
        lax.fori_loop(0, DFT_MINOR, slab, 0)


def dft_inverse_real(gr, gi, a_out):
    Bb, n_major, _, C = gr.shape
    N = n_major * DFT_MINOR
    mats = _dft_mats(n_major, a_out)
    ang = 2 * np.pi * np.outer(np.arange(a_out), np.arange(n_major)) / n_major
    f1c = np.concatenate([np.cos(ang), -np.sin(ang)], 1) / N
    pitch = DFT_MINOR + DFT_PITCH_PAD
    const = lambda shape: pl.BlockSpec(shape, lambda bb, cb, qb: (0,) * len(shape))
    gblk = pl.BlockSpec((1, DFT_QB, DFT_MINOR, LANES), lambda bb, cb, qb: (bb, qb, 0, cb))
    kern = functools.partial(_dft_inv_kernel, n_major=n_major, a_out=a_out)
    return pl.pallas_call(
        kern,
        grid=(Bb, C // LANES, n_major // DFT_QB),
        in_specs=[gblk, gblk, const((2 * DFT_MINOR, 2 * DFT_MINOR)), const((a_out, 2 * n_major)),
                  const((n_major // 8, DFT_MINOR, LANES)), const((n_major // 8, DFT_MINOR, LANES)),
                  const((8, DFT_MINOR, LANES)), const((8, DFT_MINOR, LANES))],
        out_specs=pl.BlockSpec((1, DFT_MINOR, a_out, LANES), lambda bb, cb, qb: (bb, 0, 0, cb)),
        out_shape=jax.ShapeDtypeStruct((Bb, DFT_MINOR, a_out, C), F32),
        scratch_shapes=[pltpu.VMEM((n_major * pitch, LANES), F32)] * 2,
        compiler_params=_cparams("arbitrary", "arbitrary", "arbitrary"),
        name="dft_inverse",
    )(gr, gi, jnp.asarray(mats["f2c"], BF16), jnp.asarray(f1c, BF16), *mats["tw_inv"])


def _to_slabs(x, a_used):
    Bb, _, C = x.shape
    return x.reshape(Bb, a_used, DFT_MINOR, C).transpose(0, 2, 1, 3)


def _from_slabs(xs):
    Bb, m, a, C = xs.shape
    return xs.transpose(0, 2, 1, 3).reshape(Bb, a * m, C)


def _channel_dft_mats(L):
    c = np.arange(FN_GROUP_DIM)
    ang = -2 * np.pi * np.outer(c, c) / FN_GROUP_DIM
    eye = np.eye(LANES // FN_GROUP_DIM)
    norm = 1.0 / math.sqrt(L * FN_GROUP_DIM)
    return (jnp.asarray(np.kron(eye, np.cos(ang)) * norm, F32),
            jnp.asarray(np.kron(eye, -np.sin(ang)) * norm, F32))


def fourier_mix_pallas(u):
    B, L, C = u.shape
    n_major = L // DFT_MINOR
    y = dft_forward(_to_slabs(u, n_major), n_major, mode="chanmix", chan=_channel_dft_mats(L))
    return y.transpose(0, 2, 1, 3).reshape(B, L, C)


def long_conv_pallas(vx, filt):
    B, L, D = vx.shape
    n_major = 2 * L // DFT_MINOR
    h = dft_forward(_to_slabs(filt[None], n_major), n_major, mode="spectrum", precise=True)
    gr, gi = dft_forward(_to_slabs(vx, n_major // 2), n_major, mode="mul", h=h)
    return _from_slabs(dft_inverse_real(gr, gi, n_major // 2))


def fourier_mix(u):
    B, L, _ = u.shape
    if L % (DFT_MINOR * DFT_QB) == 0:
        return fourier_mix_pallas(u.astype(F32))
    ug = u.astype(F32).reshape(B, L, FN_GROUPS, FN_GROUP_DIM)
    y = jnp.fft.fft2(ug, axes=(1, 3), norm='ortho').real
    return y.reshape(B, L, FN_WIDTH)


def hyena_filters(L, w1, b1, freq, w2, b2, w3):
    hp = lax.Precision.HIGHEST
    t = jnp.linspace(0.0, 1.0, L, dtype=F32)[:, None]
    w = 2 * math.pi * jnp.arange(L, dtype=F32)[:, None] / L
    bands = jnp.linspace(1e-4, HY_BANDS - 1, HY_BANDS, dtype=F32)[None, :]
    z = jnp.concatenate([t, jnp.cos(bands * w), -jnp.sin(bands * w)], axis=-1)
    hid = jnp.sin(freq * (jnp.dot(z, w1, precision=hp) + b1))
    hid = jnp.sin(freq * (jnp.dot(hid, w2, precision=hp) + b2))
    h = jnp.dot(hid, w3, precision=hp).reshape(L, 2, D_MODEL)
    deltas = jnp.abs(jnp.linspace(HY_MIN_DECAY, HY_MAX_DECAY, D_MODEL, dtype=F32))
    window = jnp.exp(-t * deltas[None, :])
    h = h * window[:, None, :]
    filt = jnp.concatenate([h[:, 0], jnp.zeros((1, D_MODEL), F32), h[1:, 1][::-1]], axis=0)
    return filt / jnp.sum(jnp.abs(filt), axis=0, keepdims=True)


def hyena_core(u, conv_w, conv_b, filt, skip):
    B, L, _ = u.shape
    up = jnp.pad(u, ((0, 0), (1, 1), (0, 0)))
    u = up[:, :-2] * conv_w[0] + up[:, 1:-1] * conv_w[1] + up[:, 2:] * conv_w[2] + conv_b
    x0, x1, v = jnp.split(u, 3, axis=-1)
    vx = v * x1
    if (2 * L) % (DFT_MINOR * DFT_QB) == 0:
        y = long_conv_pallas(vx, filt)
    else:
        y = jnp.fft.irfft(jnp.fft.rfft(vx, n=2 * L, axis=1) * jnp.fft.rfft(filt, n=2 * L, axis=0)[None],
                          n=2 * L, axis=1)[:, :L]
    return x0 * (y + vx * skip)


def kernel(x, c, ctx, c_ctx, mod_w, mod_b, norm_mix_g, norm_ffn_g, mix_w_in, na_rpb, mix_w_out,
           hy_w_in, hy_conv_w, hy_conv_b, hy_f_w1, hy_f_b1, hy_f_freq, hy_f_w2, hy_f_b2, hy_f_w3,
           hy_skip, hy_w_out, router_w, exp_w_gate, exp_w_up, exp_w_down, final_norm_g):
    depth = mod_w.shape[0]
    B, L, D = x.shape
    Lc = ctx.shape[1]
    last_ctx_read = depth - 1 if (depth - 1) % 2 == 0 else depth - 2
    TM = 512

    cs = ctx
    cond = jnp.concatenate([jax.nn.silu(c), jax.nn.silu(c_ctx)[None], jnp.zeros((8 - (B + 1) % 8, D), F32)], 0)

    for layer in range(depth):
        j = layer // 2
        m_all = small_matmul(cond, mod_w[layer], mod_b[layer])
        sh1, sc1, g1, sh2, sc2, g2 = jnp.split(m_all[:B], 6, axis=-1)
        m_ctx = jnp.broadcast_to(m_all[B:B + 1], (B, 6 * D))
        csh1, csc1, cg1, csh2, csc2, cg2 = jnp.split(m_ctx, 6, axis=-1)
        upd = layer < last_ctx_read
        wg = exp_w_gate[layer].astype(BF16)
        wu = exp_w_up[layer].astype(BF16)
        wd = exp_w_down[layer].astype(BF16)

        if layer % 2 == 0:
            w_in = mix_w_in[j].astype(BF16)
            w_out = mix_w_out[j].astype(BF16)
            if upd:
                pc = norm_proj(cs, norm_mix_g[layer], csh1, csc1, w_in, BF16, TM)
                kc_off = HEAD_PAIRS
                cmix_in = jnp.concatenate([ctx_attention(pc), fourier_mix(pc[..., 3 * NA_WIDTH:]).astype(BF16)], -1)
            else:
                pc = norm_proj(cs, norm_mix_g[layer], csh1, csc1, w_in[:, NA_WIDTH:3 * NA_WIDTH], BF16, TM)
                kc_off = 0
            p = norm_proj(x, norm_mix_g[layer], sh1, sc1, w_in, BF16, TM)
            att = na_attention(p, pc, kc_off, na_rpb[j])
            mix_in = jnp.concatenate([att, fourier_mix(p[..., 3 * NA_WIDTH:]).astype(BF16)], -1)
            x = proj_residual(mix_in, w_out, x, g1, TM)
            if upd:
                cs = proj_residual(cmix_in, w_out, cs, cg1, TM)
        else:
            w_in = hy_w_in[j].astype(BF16)
            w_out = hy_w_out[j].astype(BF16)
            fargs = (hy_f_w1[j], hy_f_b1[j], hy_f_freq[j], hy_f_w2[j], hy_f_b2[j], hy_f_w3[j])
            u = norm_proj(x, norm_mix_g[layer], sh1, sc1, w_in, F32, TM)
            y = hyena_core(u, hy_conv_w[j], hy_conv_b[j], hyena_filters(L, *fargs), hy_skip[j])
            x = proj_residual(y, w_out, x, g1, TM)
            if upd:
                uc = norm_proj(cs, norm_mix_g[layer], csh1, csc1, w_in, F32, TM)
                yc = hyena_core(uc, hy_conv_w[j], hy_conv_b[j], hyena_filters(Lc, *fargs), hy_skip[j])
                cs = proj_residual(yc, w_out, cs, cg1, TM)

        x = expert_choice_ffn(x, norm_ffn_g[layer], sh2, sc2, g2, router_w[layer], wg, wu, wd, TM)
        if upd:
            cs = expert_choice_ffn(cs, norm_ffn_g[layer], csh2, csc2, cg2, router_w[layer], wg, wu, wd, TM)

    return rms_norm_final(x, final_norm_g)
```

```python
import functools
import math

import numpy as np
import jax
import jax.numpy as jnp
from jax import lax
from jax.experimental import pallas as pl
from jax.experimental.pallas import tpu as pltpu

D_MODEL = 1024
GRID_W = 64
HEAD_DIM = 64
NA_HEADS = 12
NA_WIDTH = NA_HEADS * HEAD_DIM
NA_WIN_ROWS = 8
NA_WIN_COLS = 16
FN_GROUPS = 4
FN_GROUP_DIM = 64
FN_WIDTH = FN_GROUPS * FN_GROUP_DIM
HY_SHORT = 3
HY_EMB = 33
HY_BANDS = (HY_EMB - 1) // 2
HY_MIN_DECAY = math.log(1e-2) / 1.5
HY_MAX_DECAY = math.log(1e-2) / 0.3
N_EXPERTS = 16
EC_CAPACITY_FACTOR = 2
RMS_EPS = 1e-6

LANES = 128
HEAD_PAIRS = NA_WIDTH // LANES
NA_Q_ROWS = 2
NEG_INF = -1e30
LOG2E = math.log2(math.e)
QK_SCALE = HEAD_DIM ** -0.5 * LOG2E
VMEM_LIMIT = 56 * 1024 * 1024

BF16 = jnp.bfloat16
F32 = jnp.float32


def _cparams(*sem):
    return pltpu.CompilerParams(dimension_semantics=sem, vmem_limit_bytes=VMEM_LIMIT)


def _small_matmul_kernel(a_ref, w_ref, b_ref, o_ref):
    o_ref[...] = jnp.dot(a_ref[...], w_ref[...], preferred_element_type=F32,
                         precision=lax.Precision.HIGHEST) + b_ref[...]


def small_matmul(a, w, b, tn=1536):
    M, K = a.shape
    N = w.shape[1]
    return pl.pallas_call(
        _small_matmul_kernel,
        grid=(N // tn,),
        in_specs=[pl.BlockSpec((M, K), lambda j: (0, 0)),
                  pl.BlockSpec((K, tn), lambda j: (0, j)),
                  pl.BlockSpec((1, tn), lambda j: (0, j))],
        out_specs=pl.BlockSpec((M, tn), lambda j: (0, j)),
        out_shape=jax.ShapeDtypeStruct((M, N), F32),
        compiler_params=_cparams("arbitrary"),
        name="small_matmul",
    )(a, w, b.reshape(1, N))


def _norm_mod(x, g, shift, scale):
    ms = jnp.mean(x * x, axis=-1, keepdims=True)
    y = x * lax.rsqrt(ms + RMS_EPS) * g
    return y * (1.0 + scale) + shift


def _norm_proj_kernel(x_ref, g_ref, sh_ref, sc_ref, w_ref, cs_ref, o_ref):
    h = _norm_mod(x_ref[0], g_ref[...], sh_ref[0], sc_ref[0])
    y = jnp.dot(h.astype(BF16), w_ref[...], preferred_element_type=F32)
    o_ref[0] = (y * cs_ref[...]).astype(o_ref.dtype)


def norm_proj(x, g, shift, scale, w, out_dtype, tm, col_scale=None):
    B, L, D = x.shape
    N = w.shape[1]
    tm = min(tm, L)
    if col_scale is None:
        col_scale = jnp.ones((N,), F32)
    return pl.pallas_call(
        _norm_proj_kernel,
        grid=(B, L // tm),
        in_specs=[pl.BlockSpec((1, tm, D), lambda b, i: (b, i, 0)),
                  pl.BlockSpec((1, D), lambda b, i: (0, 0)),
                  pl.BlockSpec((1, 1, D), lambda b, i: (b, 0, 0)),
                  pl.BlockSpec((1, 1, D), lambda b, i: (b, 0, 0)),
                  pl.BlockSpec((D, N), lambda b, i: (0, 0)),
                  pl.BlockSpec((1, N), lambda b, i: (0, 0))],
        out_specs=pl.BlockSpec((1, tm, N), lambda b, i: (b, i, 0)),
        out_shape=jax.ShapeDtypeStruct((B, L, N), out_dtype),
        compiler_params=_cparams("arbitrary", "arbitrary"),
        name="norm_proj",
    )(x, g.reshape(1, D), shift.reshape(B, 1, D), scale.reshape(B, 1, D), w, col_scale.reshape(1, N))


def _norm_router_kernel(x_ref, g_ref, sh_ref, sc_ref, rw_ref, h_ref, aff_ref):
    h = _norm_mod(x_ref[0], g_ref[...], sh_ref[0], sc_ref[0])
    h_ref[0] = h.astype(h_ref.dtype)
    logits = jnp.dot(h, rw_ref[...], preferred_element_type=F32, precision=lax.Precision.HIGHEST)
    m = jnp.max(logits, axis=-1, keepdims=True)
    e = jnp.exp(logits - m)
    aff_ref[0] = e / jnp.sum(e, axis=-1, keepdims=True)


def norm_router(x, g, shift, scale, router_w, tm, h_dtype):
    B, L, D = x.shape
    E = router_w.shape[1]
    tm = min(tm, L)
    return pl.pallas_call(
        _norm_router_kernel,
        grid=(B, L // tm),
        in_specs=[pl.BlockSpec((1, tm, D), lambda b, i: (b, i, 0)),
                  pl.BlockSpec((1, D), lambda b, i: (0, 0)),
                  pl.BlockSpec((1, 1, D), lambda b, i: (b, 0, 0)),
                  pl.BlockSpec((1, 1, D), lambda b, i: (b, 0, 0)),
                  pl.BlockSpec((D, E), lambda b, i: (0, 0))],
        out_specs=[pl.BlockSpec((1, tm, D), lambda b, i: (b, i, 0)),
                   pl.BlockSpec((1, tm, E), lambda b, i: (b, i, 0))],
        out_shape=[jax.ShapeDtypeStruct((B, L, D), h_dtype),
                   jax.ShapeDtypeStruct((B, L, E), F32)],
        compiler_params=_cparams("arbitrary", "arbitrary"),
        name="norm_router",
    )(x, g.reshape(1, D), shift.reshape(B, 1, D), scale.reshape(B, 1, D), router_w)


def _proj_residual_kernel(y_ref, w_ref, x_ref, gate_ref, o_ref):
    mix = jnp.dot(y_ref[0].astype(BF16), w_ref[...], preferred_element_type=F32)
    o_ref[0] = x_ref[0] + gate_ref[0] * mix


def proj_residual(y, w, x, gate, tm):
    B, L, K = y.shape
    D = w.shape[1]
    tm = min(tm, L)
    return pl.pallas_call(
        _proj_residual_kernel,
        grid=(B, L // tm),
        in_specs=[pl.BlockSpec((1, tm, K), lambda b, i: (b, i, 0)),
                  pl.BlockSpec((K, D), lambda b, i: (0, 0)),
                  pl.BlockSpec((1, tm, D), lambda b, i: (b, i, 0)),
                  pl.BlockSpec((1, 1, D), lambda b, i: (b, 0, 0))],
        out_specs=pl.BlockSpec((1, tm, D), lambda b, i: (b, i, 0)),
        out_shape=jax.ShapeDtypeStruct((B, L, D), F32),
        compiler_params=_cparams("arbitrary", "arbitrary"),
        name="proj_residual",
    )(y, w, x, gate.reshape(B, 1, D))


def _rms_kernel(x_ref, g_ref, o_ref):
    x = x_ref[0]
    ms = jnp.mean(x * x, axis=-1, keepdims=True)
    o_ref[0] = x * lax.rsqrt(ms + RMS_EPS) * g_ref[...]


def rms_norm_final(x, g, tm=1024):
    B, L, D = x.shape
    return pl.pallas_call(
        _rms_kernel,
        grid=(B, L // tm),
        in_specs=[pl.BlockSpec((1, tm, D), lambda b, i: (b, i, 0)),
                  pl.BlockSpec((1, D), lambda b, i: (0, 0))],
        out_specs=pl.BlockSpec((1, tm, D), lambda b, i: (b, i, 0)),
        out_shape=jax.ShapeDtypeStruct((B, L, D), F32),
        compiler_params=_cparams("arbitrary", "arbitrary"),
        name="rms_final",
    )(x, g.reshape(1, D))


def _two_head_rows(q):
    lane = lax.broadcasted_iota(jnp.int32, q.shape, 1)
    zero = jnp.zeros_like(q)
    return jnp.concatenate([jnp.where(lane < HEAD_DIM, q, zero),
                            jnp.where(lane >= HEAD_DIM, q, zero)], axis=0)


def _merge_two_heads(o):
    n = o.shape[0] // 2
    lane = lax.broadcasted_iota(jnp.int32, (n, LANES), 1)
    return jnp.where(lane < HEAD_DIM, o[:n], o[n:])


_CONTRACT_LAST = (((1,), (1,)), ((), ()))


def _na_key_rows(rq):
    n = rq + NA_WIN_ROWS - 1
    return n + n % 2


def _na_kernel(q_ref, k_ref, v_ref, kc_ref, vc_ref, bias_ref, o_ref, *, rows, rq):
    nq = rq * GRID_W
    nkr = _na_key_rows(rq)
    kwin = nkr * GRID_W
    kc = kc_ref[0]
    vc = vc_ref[0]

    def step(i, carry):
        r = i * rq
        start = jnp.clip(r - NA_WIN_ROWS // 2, 0, rows - nkr)
        variant = r - start
        q0 = pl.multiple_of(r * GRID_W, nq)
        k0 = pl.multiple_of(start * GRID_W, GRID_W)
        q2 = _two_head_rows(q_ref[0, pl.ds(q0, nq), :])
        kw = k_ref[0, pl.ds(k0, kwin), :]
        vw = v_ref[0, pl.ds(k0, kwin), :]
        s_loc = lax.dot_general(q2, kw, _CONTRACT_LAST, preferred_element_type=F32) + bias_ref[0, variant]
        s_ctx = lax.dot_general(q2, kc, _CONTRACT_LAST, preferred_element_type=F32)
        m = jnp.maximum(jnp.max(s_loc, axis=-1, keepdims=True), jnp.max(s_ctx, axis=-1, keepdims=True))
        p_loc = jnp.exp2(s_loc - m)
        p_ctx = jnp.exp2(s_ctx - m)
        denom = jnp.sum(p_loc, axis=-1, keepdims=True) + jnp.sum(p_ctx, axis=-1, keepdims=True)
        o = (jnp.dot(p_loc.astype(BF16), vw, preferred_element_type=F32)
             + jnp.dot(p_ctx.astype(BF16), vc, preferred_element_type=F32))
        o = o / denom
        o_ref[0, pl.ds(q0, nq), :] = _merge_two_heads(o).astype(o_ref.dtype)
        return carry

    lax.fori_loop(0, rows // rq, step, 0, unroll=2)


def _na_bias_tables(rpb, rows, rq):
    W = GRID_W
    nkr = _na_key_rows(rq)
    n_var = nkr - rq + 1
    n_ro, n_co = 2 * NA_WIN_ROWS - 1, 2 * NA_WIN_COLS - 1
    cols = np.arange(W)
    col_start = np.clip(cols - NA_WIN_COLS // 2, 0, W - NA_WIN_COLS)
    row_sel = np.zeros((n_var, rq, nkr, n_ro), np.float32)
    for var in range(n_var):
        r = var if var <= NA_WIN_ROWS // 2 else rows - nkr + var
        start = r - var
        for dr in range(rq):
            q_row = r + dr
            if q_row >= rows:
                continue
            rs = int(np.clip(q_row - NA_WIN_ROWS // 2, 0, rows - NA_WIN_ROWS))
            for kr in range(nkr):
                key_row = start + kr
                if rs <= key_row < rs + NA_WIN_ROWS:
                    row_sel[var, dr, kr, key_row - q_row + NA_WIN_ROWS - 1] = 1.0
    col_sel = np.zeros((n_co, W, W), np.float32)
    for q in range(W):
        for kc in range(col_start[q], col_start[q] + NA_WIN_COLS):
            col_sel[kc - q + NA_WIN_COLS - 1, q, kc] = 1.0
    inside = (row_sel.sum(-1)[:, :, :, None, None] * col_sel.sum(0)[None, None, None]) > 0
    hp = lax.Precision.HIGHEST
    by_row = jnp.einsum('vdko,hoc->hvdkc', row_sel, rpb.astype(F32), precision=hp)
    tab = jnp.einsum('hvdkc,cqn->hvdkqn', by_row, col_sel, precision=hp)
    tab = jnp.where(inside[None], tab * LOG2E, NEG_INF)
    tab = tab.reshape(HEAD_PAIRS, 2, n_var, rq, nkr, W, W).transpose(0, 2, 1, 3, 5, 4, 6)
    return tab.reshape(HEAD_PAIRS, n_var, 2 * rq * W, nkr * W)


def na_attention(p, pc, kc_off, rpb):
    B, L, _ = p.shape
    Lc = pc.shape[1]
    rows = L // GRID_W
    rq = NA_Q_ROWS
    bias = _na_bias_tables(rpb, rows, rq)
    nkr = _na_key_rows(rq)
    kern = functools.partial(_na_kernel, rows=rows, rq=rq)
    return pl.pallas_call(
        kern,
        grid=(B, HEAD_PAIRS),
        in_specs=[pl.BlockSpec((1, L, LANES), lambda b, j: (b, 0, j)),
                  pl.BlockSpec((1, L, LANES), lambda b, j: (b, 0, HEAD_PAIRS + j)),
                  pl.BlockSpec((1, L, LANES), lambda b, j: (b, 0, 2 * HEAD_PAIRS + j)),
                  pl.BlockSpec((1, Lc, LANES), lambda b, j: (b, 0, kc_off + j)),
                  pl.BlockSpec((1, Lc, LANES), lambda b, j: (b, 0, kc_off + HEAD_PAIRS + j)),
                  pl.BlockSpec((1, nkr - rq + 1, 2 * rq * GRID_W, nkr * GRID_W), lambda b, j: (j, 0, 0, 0))],
        out_specs=pl.BlockSpec((1, L, LANES), lambda b, j: (b, 0, j)),
        out_shape=jax.ShapeDtypeStruct((B, L, NA_WIDTH), BF16),
        compiler_params=_cparams("arbitrary", "arbitrary"),
        name="na_attention",
    )(p, p, p, pc, pc, bias)


def _ctx_attn_kernel(q_ref, k_ref, v_ref, o_ref):
    q2 = _two_head_rows(q_ref[0])
    s = lax.dot_general(q2, k_ref[0], _CONTRACT_LAST, preferred_element_type=F32)
    m = jnp.max(s, axis=-1, keepdims=True)
    e = jnp.exp2(s - m)
    o = jnp.dot(e.astype(BF16), v_ref[0], preferred_element_type=F32) / jnp.sum(e, axis=-1, keepdims=True)
    o_ref[0] = _merge_two_heads(o).astype(o_ref.dtype)


def ctx_attention(pc):
    B, Lc, _ = pc.shape
    return pl.pallas_call(
        _ctx_attn_kernel,
        grid=(B, HEAD_PAIRS),
        in_specs=[pl.BlockSpec((1, Lc, LANES), lambda b, j: (b, 0, j)),
                  pl.BlockSpec((1, Lc, LANES), lambda b, j: (b, 0, HEAD_PAIRS + j)),
                  pl.BlockSpec((1, Lc, LANES), lambda b, j: (b, 0, 2 * HEAD_PAIRS + j))],
        out_specs=pl.BlockSpec((1, Lc, LANES), lambda b, j: (b, 0, j)),
        out_shape=jax.ShapeDtypeStruct((B, Lc, NA_WIDTH), BF16),
        compiler_params=_cparams("arbitrary", "arbitrary"),
        name="ctx_attention",
    )(pc, pc, pc)


def _expert_ffn_kernel(x_ref, gate_ref, wg_ref, wu_ref, wd_ref, o_ref, *, fchunk):
    o_ref[0, 0] = _swiglu(x_ref[0, 0], wg_ref, wu_ref, wd_ref, fchunk) * gate_ref[0, 0]


def expert_ffn(xe, gate, wg, wu, wd, fchunk=512):
    E, B, cap, D = xe.shape
    F = wg.shape[2]
    kern = functools.partial(_expert_ffn_kernel, fchunk=fchunk)
    return pl.pallas_call(
        kern,
        grid=(E, B),
        in_specs=[pl.BlockSpec((1, 1, cap, D), lambda e, b: (e, b, 0, 0)),
                  pl.BlockSpec((1, 1, cap, 1), lambda e, b: (e, b, 0, 0)),
                  pl.BlockSpec((1, D, F), lambda e, b: (e, 0, 0)),
                  pl.BlockSpec((1, D, F), lambda e, b: (e, 0, 0)),
                  pl.BlockSpec((1, F, D), lambda e, b: (e, 0, 0))],
        out_specs=pl.BlockSpec((1, 1, cap, D), lambda e, b: (e, b, 0, 0)),
        out_shape=jax.ShapeDtypeStruct((E, B, cap, D), F32),
        compiler_params=_cparams("arbitrary", "arbitrary"),
        name="expert_ffn",
    )(xe, gate, wg, wu, wd)


def _swiglu(x, wg_ref, wu_ref, wd_ref, fchunk):
    F = wg_ref.shape[2]
    acc = jnp.zeros((x.shape[0], wd_ref.shape[2]), F32)
    for f0 in range(0, F, fchunk):
        g = jnp.dot(x, wg_ref[0, :, f0:f0 + fchunk], preferred_element_type=F32)
        u = jnp.dot(x, wu_ref[0, :, f0:f0 + fchunk], preferred_element_type=F32)
        hid = (g * jax.nn.sigmoid(g) * u).astype(BF16)
        acc = acc + jnp.dot(hid, wd_ref[0, f0:f0 + fchunk, :], preferred_element_type=F32)
    return acc


def _gather_rows_copy(h_hbm, xbuf, sem, slot, row, j):
    return pltpu.make_async_copy(h_hbm.at[pl.ds(row, 1)], xbuf.at[slot, pl.ds(j, 1)], sem.at[slot])


def _expert_ffn_gather_kernel(rows_ref, rows_next_ref, gate_ref, h_hbm, wg_ref, wu_ref, wd_ref, o_ref,
                              xbuf, sem, *, fchunk):
    cap = xbuf.shape[1]
    step = pl.program_id(0) * pl.num_programs(1) + pl.program_id(1)
    nsteps = pl.num_programs(0) * pl.num_programs(1)
    slot = step % 2

    def drain(s):
        pltpu.make_async_copy(h_hbm.at[pl.ds(0, cap)], xbuf.at[s], sem.at[s]).wait()

    @pl.when(step == 0)
    def _():
        def body(j, carry):
            _gather_rows_copy(h_hbm, xbuf, sem, 0, rows_ref[0, 0, j], j).start()
            return carry
        lax.fori_loop(0, cap, body, 0)

    drain(slot)
    x = xbuf[slot].astype(BF16)
    for j in range(cap):
        _gather_rows_copy(h_hbm, xbuf, sem, 1 - slot, rows_next_ref[0, 0, j], j).start()
    y = _swiglu(x, wg_ref, wu_ref, wd_ref, fchunk)
    o_ref[0, 0] = (y * gate_ref[0, 0]).astype(o_ref.dtype)

    @pl.when(step == nsteps - 1)
    def _():
        drain(1 - slot)


def expert_ffn_gather(h_rows, rows, gate, wg, wu, wd, fchunk=512):
    E, B, cap = rows.shape
    D = h_rows.shape[1]
    F = wg.shape[2]

    rows = rows.reshape(E * B, 1, cap)

    def next_group(e, b):
        return (jnp.minimum(e * B + b + 1, E * B - 1), 0, 0)

    kern = functools.partial(_expert_ffn_gather_kernel, fchunk=fchunk)
    return pl.pallas_call(
        kern,
        grid=(E, B),
        in_specs=[pl.BlockSpec((1, 1, cap), lambda e, b: (e * B + b, 0, 0), memory_space=pltpu.SMEM),
                  pl.BlockSpec((1, 1, cap), next_group, memory_space=pltpu.SMEM),
                  pl.BlockSpec((1, 1, cap, 1), lambda e, b: (e, b, 0, 0)),
                  pl.BlockSpec(memory_space=pl.ANY),
                  pl.BlockSpec((1, D, F), lambda e, b: (e, 0, 0)),
                  pl.BlockSpec((1, D, F), lambda e, b: (e, 0, 0)),
                  pl.BlockSpec((1, F, D), lambda e, b: (e, 0, 0))],
        out_specs=pl.BlockSpec((1, 1, cap, D), lambda e, b: (e, b, 0, 0)),
        out_shape=jax.ShapeDtypeStruct((E, B, cap, D), BF16),
        scratch_shapes=[pltpu.VMEM((2, cap, D), F32), pltpu.SemaphoreType.DMA((2,))],
        compiler_params=_cparams("arbitrary", "arbitrary"),
        name="expert_ffn_gather",
    )(rows, rows, gate, h_rows, wg, wu, wd)


COMBINE_TOKENS = 512
COMBINE_WINDOW = 256
COMBINE_CHUNK = LANES


def _window_copy(ye_hbm, win, sem, e, b, w0, slot):
    return pltpu.make_async_copy(ye_hbm.at[e, b, pl.ds(w0, COMBINE_WINDOW)], win.at[slot], sem.at[slot])


def _moe_combine_kernel(starts_ref, tok_ref, x_ref, gate_ref, ye_hbm, o_ref, acc_ref, win, sem):
    b = pl.program_id(0)
    i = pl.program_id(1)
    n_exp = tok_ref.shape[0]
    cap = ye_hbm.shape[2]
    T = x_ref.shape[1]
    tile0 = i * T
    tok_iota = lax.broadcasted_iota(jnp.int32, (T, COMBINE_CHUNK), 0) + tile0

    nt = pl.num_programs(1)
    step = b * nt + i
    parity = step % 2

    def first_window(e, bb=b, ii=i):
        s = starts_ref[bb, e, ii]
        return jnp.minimum((s // COMBINE_CHUNK) * COMBINE_CHUNK, cap - COMBINE_WINDOW)

    def fetch_first_windows(bb, ii, par):
        for e in range(n_exp):
            _window_copy(ye_hbm, win, sem, e, bb, first_window(e, bb, ii), par * n_exp + e).start()

    def one_hot(e, w0, first_row=None):
        k = w0 // COMBINE_CHUNK
        hot = []
        for c in range(COMBINE_WINDOW // COMBINE_CHUNK):
            match = tok_iota == tok_ref[e, 0, pl.ds(k + c, 1), :]
            if first_row is not None:
                row = lax.broadcasted_iota(jnp.int32, match.shape, 1) + (w0 + c * COMBINE_CHUNK)
                match = match & (row >= first_row)
            hot.append(match.astype(BF16))
        return hot

    @pl.when(step == 0)
    def _():
        fetch_first_windows(b, i, 0)

    @pl.when(step + 1 < pl.num_programs(0) * nt)
    def _():
        fetch_first_windows((step + 1) // nt, (step + 1) % nt, 1 - parity)

    firsts = [first_window(e) for e in range(n_exp)]
    hot = []
    for e in range(n_exp):
        hot += one_hot(e, firsts[e])
    for e in range(n_exp):
        _window_copy(ye_hbm, win, sem, e, b, firsts[e], parity * n_exp + e).wait()
    rows_all = win[pl.ds(parity * n_exp, n_exp)].reshape(n_exp * COMBINE_WINDOW, win.shape[2])
    acc_ref[...] = jnp.dot(jnp.concatenate(hot, axis=1), rows_all, preferred_element_type=F32)

    def per_expert(e, carry):
        w0 = first_window(e)
        end = starts_ref[b, e, i + 1]
        n_more = jnp.maximum(end - (w0 + COMBINE_WINDOW) + COMBINE_WINDOW - 1, 0) // COMBINE_WINDOW

        def more(m, c):
            first_row = w0 + (m + 1) * COMBINE_WINDOW
            w = jnp.minimum(first_row, cap - COMBINE_WINDOW)
            cp = _window_copy(ye_hbm, win, sem, e, b, w, 2 * n_exp)
            cp.start()
            cp.wait()
            sel = jnp.concatenate(one_hot(e, w, first_row), axis=1)
            acc_ref[...] += jnp.dot(sel, win[2 * n_exp], preferred_element_type=F32)
            return c
        lax.fori_loop(0, n_more, more, 0)
        return carry

    lax.fori_loop(0, n_exp, per_expert, 0)
    o_ref[0] = x_ref[0] + gate_ref[0] * acc_ref[...]


def moe_combine(x, gate_res, ye, tok):
    B, n, D = x.shape
    E, _, cap, _ = ye.shape
    T = COMBINE_TOKENS
    nt = n // T
    bounds = jnp.arange(nt + 1, dtype=jnp.int32) * T
    starts = jnp.sum(tok[:, :, None, :] < bounds[None, None, :, None], axis=-1, dtype=jnp.int32)
    starts = jnp.transpose(starts, (1, 0, 2))
    tok4 = tok.reshape(E, B, cap // COMBINE_CHUNK, COMBINE_CHUNK)
    grid_spec = pltpu.PrefetchScalarGridSpec(
        num_scalar_prefetch=1,
        grid=(B, nt),
        in_specs=[pl.BlockSpec((E, 1, cap // COMBINE_CHUNK, COMBINE_CHUNK), lambda b, i, s: (0, b, 0, 0)),
                  pl.BlockSpec((1, T, D), lambda b, i, s: (b, i, 0)),
                  pl.BlockSpec((1, 1, D), lambda b, i, s: (b, 0, 0)),
                  pl.BlockSpec(memory_space=pl.ANY)],
        out_specs=pl.BlockSpec((1, T, D), lambda b, i, s: (b, i, 0)),
        scratch_shapes=[pltpu.VMEM((T, D), F32), pltpu.VMEM((2 * E + 1, COMBINE_WINDOW, D), BF16),
                        pltpu.SemaphoreType.DMA((2 * E + 1,))])
    return pl.pallas_call(
        _moe_combine_kernel,
        grid_spec=grid_spec,
        out_shape=jax.ShapeDtypeStruct((B, n, D), F32),
        compiler_params=_cparams("arbitrary", "arbitrary"),
        name="moe_combine",
    )(starts, tok4, x, gate_res.reshape(B, 1, D), ye)


def expert_choice_ffn(x, g, shift, scale, gate_res, router_w, wg, wu, wd, tm):
    B, n, D = x.shape
    E = N_EXPERTS
    cap = EC_CAPACITY_FACTOR * n // E
    fused = cap % COMBINE_WINDOW == 0 and cap > COMBINE_WINDOW and n % COMBINE_TOKENS == 0
    h, aff = norm_router(x, g, shift, scale, router_w, tm, F32 if fused else BF16)
    gate, idx = lax.top_k(jnp.transpose(aff, (2, 0, 1)), cap)
    bidx = jnp.arange(B, dtype=idx.dtype)[None, :, None]
    if fused:
        idx, gate = lax.sort((idx, gate), dimension=2, num_keys=1)
        ye = expert_ffn_gather(h.reshape(B * n, D), idx + bidx * n, gate[..., None], wg, wu, wd)
        return moe_combine(x, gate_res, ye, idx)
    xe = h[bidx, idx].reshape(E, 1, B * cap, D)
    ye = expert_ffn(xe, gate.reshape(E, 1, B * cap, 1), wg, wu, wd)
    flat = (idx + bidx * n).reshape(-1)
    moe = jnp.zeros((B * n, D), F32).at[flat].add(ye.reshape(-1, D)).reshape(B, n, D)
    return x + gate_res[:, None, :] * moe


DFT_MINOR = 128
DFT_PITCH_PAD = 8
DFT_QB = 16
DFT_UNROLL = 8


def _dft_mats(n_major, a_used):
    N = n_major * DFT_MINOR
    q = np.arange(n_major)
    a = np.arange(a_used)
    b = np.arange(DFT_MINOR)
    ang1 = -2 * np.pi * np.outer(q, a) / n_major
    f1 = np.concatenate([np.cos(ang1), np.sin(ang1)], 0)
    ang2 = -2 * np.pi * np.outer(b, b) / DFT_MINOR
    c2, s2 = np.cos(ang2), np.sin(ang2)
    f2 = np.block([[c2, -s2], [s2, c2]])
    f2c = np.block([[c2, s2], [-s2, c2]])
    lane1 = np.ones((1, 1, LANES))

    def tw(hi, lo):
        ang = -2 * np.pi * np.outer(hi, lo) / N
        return (np.cos(ang)[:, :, None] * lane1, np.sin(ang)[:, :, None] * lane1)

    t1r, t1i = tw(8 * np.arange(DFT_MINOR // 8), q)
    t0r, t0i = tw(np.arange(8), q)
    u1r, u1i = tw(8 * np.arange(n_major // 8), b)
    u0r, u0i = tw(np.arange(8), b)
    f32 = lambda *xs: [np.asarray(x, np.float32) for x in xs]
    return dict(f1=f1, f2=f2, f2c=f2c, tw_fwd=f32(t1r, t1i, t0r, t0i), tw_inv=f32(u1r, u1i, u0r, u0i))


def _cmul(ar, ai, br, bi):
    return ar * br - ai * bi, ar * bi + ai * br


def _dft_fwd_kernel(*refs, n_major, a_used, mode, precision):
    x_ref, f1_ref, f2_ref, t1r, t1i, t0r, t0i = refs[:7]
    rest = refs[7:]
    if mode == "mul":
        hr_ref, hi_ref, or_ref, oi_ref, sr, si = rest
    elif mode == "chanmix":
        mc_ref, ms_ref, or_ref, sr, si = rest
    else:
        or_ref, oi_ref, sr, si = rest
    pitch = n_major + DFT_PITCH_PAD
    qb = pl.program_id(2)
    mm_dtype = f1_ref.dtype

    @pl.when(qb == 0)
    def _stage1():
        def slab(b, carry):
            y = jnp.dot(f1_ref[...], x_ref[0, b].astype(mm_dtype), preferred_element_type=F32,
                        precision=precision)
            twr, twi = _cmul(t1r[b // 8], t1i[b // 8], t0r[b % 8], t0i[b % 8])
            yr, yi = _cmul(y[:n_major], y[n_major:], twr, twi)
            row = pl.multiple_of(b * pitch, 8)
            sr[pl.ds(row, n_major), :] = yr
            si[pl.ds(row, n_major), :] = yi
            return carry
        lax.fori_loop(0, DFT_MINOR, slab, 0, unroll=DFT_UNROLL)

    def freq(j, carry):
        q = qb * DFT_QB + j
        g = jnp.concatenate([sr[pl.ds(q, DFT_MINOR, stride=pitch), :],
                             si[pl.ds(q, DFT_MINOR, stride=pitch), :]], axis=0).astype(mm_dtype)
        xf = jnp.dot(f2_ref[...], g, preferred_element_type=F32, precision=precision)
        xr, xi = xf[:DFT_MINOR], xf[DFT_MINOR:]
        if mode == "mul":
            xr, xi = _cmul(xr, xi, hr_ref[0, j], hi_ref[0, j])
            or_ref[0, j] = xr.astype(or_ref.dtype)
            oi_ref[0, j] = xi.astype(oi_ref.dtype)
        elif mode == "chanmix":
            or_ref[0, j] = (jnp.dot(xr.astype(mm_dtype), mc_ref[...], preferred_element_type=F32)
                            + jnp.dot(xi.astype(mm_dtype), ms_ref[...], preferred_element_type=F32))
        else:
            or_ref[0, j] = xr
            oi_ref[0, j] = xi
        return carry
    lax.fori_loop(0, DFT_QB, freq, 0, unroll=DFT_UNROLL)


def dft_forward(xp, n_major, mode="spectrum", h=None, chan=None, precise=False):
    Bb, _, a_used, C = xp.shape
    mats = _dft_mats(n_major, a_used)
    mm_dtype = F32 if precise else BF16
    precision = lax.Precision.HIGHEST if precise else None
    pitch = n_major + DFT_PITCH_PAD
    const = lambda shape: pl.BlockSpec(shape, lambda bb, cb, qb: (0,) * len(shape))
    spec_blk = pl.BlockSpec((1, DFT_QB, DFT_MINOR, LANES), lambda bb, cb, qb: (bb, qb, 0, cb))
    args = [xp, jnp.asarray(mats["f1"], mm_dtype), jnp.asarray(mats["f2"], mm_dtype)] + mats["tw_fwd"]
    in_specs = [pl.BlockSpec((1, DFT_MINOR, a_used, LANES), lambda bb, cb, qb: (bb, 0, 0, cb)),
                const((2 * n_major, a_used)), const((2 * DFT_MINOR, 2 * DFT_MINOR)),
                const((DFT_MINOR // 8, n_major, LANES)), const((DFT_MINOR // 8, n_major, LANES)),
                const((8, n_major, LANES)), const((8, n_major, LANES))]
    spec_shape = (Bb, n_major, DFT_MINOR, C)
    if mode == "mul":
        args += [h[0], h[1]]
        hblk = pl.BlockSpec((1, DFT_QB, DFT_MINOR, LANES), lambda bb, cb, qb: (0, qb, 0, cb))
        in_specs += [hblk, hblk]
        out_specs = [spec_blk, spec_blk]
        out_shape = [jax.ShapeDtypeStruct(spec_shape, BF16)] * 2
    elif mode == "chanmix":
        args += [chan[0].astype(mm_dtype), chan[1].astype(mm_dtype)]
        in_specs += [const((LANES, LANES)), const((LANES, LANES))]
        out_specs = spec_blk
        out_shape = jax.ShapeDtypeStruct(spec_shape, F32)
    else:
        out_specs = [spec_blk, spec_blk]
        out_shape = [jax.ShapeDtypeStruct(spec_shape, F32)] * 2
    kern = functools.partial(_dft_fwd_kernel, n_major=n_major, a_used=a_used, mode=mode, precision=precision)
    return pl.pallas_call(
        kern,
        grid=(Bb, C // LANES, n_major // DFT_QB),
        in_specs=in_specs,
        out_specs=out_specs,
        out_shape=out_shape,
        scratch_shapes=[pltpu.VMEM((DFT_MINOR * pitch, LANES), F32)] * 2,
        compiler_params=_cparams("arbitrary", "arbitrary", "arbitrary"),
        name="dft_forward_" + mode,
    )(*args)


def _dft_inv_kernel(gr_ref, gi_ref, f2c_ref, f1c_ref, u1r, u1i, u0r, u0i, o_ref, sr, si, *, n_major, a_out):
    pitch = DFT_MINOR + DFT_PITCH_PAD
    qb = pl.program_id(2)

    def freq(j, carry):
        q = qb * DFT_QB + j
        g = jnp.concatenate([gr_ref[0, j], gi_ref[0, j]], axis=0)
        t = jnp.dot(f2c_ref[...], g, preferred_element_type=F32)
        twr, twi = _cmul(u1r[q // 8], u1i[q // 8], u0r[q % 8], u0i[q % 8])
        tr, ti = _cmul(t[:DFT_MINOR], t[DFT_MINOR:], twr, -twi)
        row = pl.multiple_of(q * pitch, 8)
        sr[pl.ds(row, DFT_MINOR), :] = tr
        si[pl.ds(row, DFT_MINOR), :] = ti
        return carry
    lax.fori_loop(0, DFT_QB, freq, 0, unroll=DFT_UNROLL)

    @pl.when(qb == pl.num_programs(2) - 1)
    def _stage2():
        def slab(b, carry):
            g = jnp.concatenate([sr[pl.ds(b, n_major, stride=pitch), :],
                                 si[pl.ds(b, n_major, stride=pitch), :]], axis=0).astype(BF16)
            o_ref[0, b] = jnp.dot(f1c_ref[...], g, preferred_element_type=F32)
            return carry
        lax.fori_loop(0, DFT_MINOR, slab, 0, unroll=DFT_UNROLL)


def dft_inverse_real(gr, gi, a_out):
    Bb, n_major, _, C = gr.shape
    N = n_major * DFT_MINOR
    mats = _dft_mats(n_major, a_out)
    ang = 2 * np.pi * np.outer(np.arange(a_out), np.arange(n_major)) / n_major
    f1c = np.concatenate([np.cos(ang), -np.sin(ang)], 1) / N
    pitch = DFT_MINOR + DFT_PITCH_PAD
    const = lambda shape: pl.BlockSpec(shape, lambda bb, cb, qb: (0,) * len(shape))
    gblk = pl.BlockSpec((1, DFT_QB, DFT_MINOR, LANES), lambda bb, cb, qb: (bb, qb, 0, cb))
    kern = functools.partial(_dft_inv_kernel, n_major=n_major, a_out=a_out)
    return pl.pallas_call(
        kern,
        grid=(Bb, C // LANES, n_major // DFT_QB),
        in_specs=[gblk, gblk, const((2 * DFT_MINOR, 2 * DFT_MINOR)), const((a_out, 2 * n_major)),
                  const((n_major // 8, DFT_MINOR, LANES)), const((n_major // 8, DFT_MINOR, LANES)),
                  const((8, DFT_MINOR, LANES)), const((8, DFT_MINOR, LANES))],
        out_specs=pl.BlockSpec((1, DFT_MINOR, a_out, LANES), lambda bb, cb, qb: (bb, 0, 0, cb)),
        out_shape=jax.ShapeDtypeStruct((Bb, DFT_MINOR, a_out, C), F32),
        scratch_shapes=[pltpu.VMEM((n_major * pitch, LANES), F32)] * 2,
        compiler_params=_cparams("arbitrary", "arbitrary", "arbitrary"),
        name="dft_inverse",
    )(gr, gi, jnp.asarray(mats["f2c"], BF16), jnp.asarray(f1c, BF16), *mats["tw_inv"])


def _to_slabs(x, a_used):
    Bb, _, C = x.shape
    return x.reshape(Bb, a_used, DFT_MINOR, C).transpose(0, 2, 1, 3)


def _from_slabs(xs):
    Bb, m, a, C = xs.shape
    return xs.transpose(0, 2, 1, 3).reshape(Bb, a * m, C)


def _channel_dft_mats(L):
    c = np.arange(FN_GROUP_DIM)
    ang = -2 * np.pi * np.outer(c, c) / FN_GROUP_DIM
    eye = np.eye(LANES // FN_GROUP_DIM)
    norm = 1.0 / math.sqrt(L * FN_GROUP_DIM)
    return (jnp.asarray(np.kron(eye, np.cos(ang)) * norm, F32),
            jnp.asarray(np.kron(eye, -np.sin(ang)) * norm, F32))


def fourier_mix_pallas(u):
    B, L, C = u.shape
    n_major = L // DFT_MINOR
    y = dft_forward(_to_slabs(u, n_major), n_major, mode="chanmix", chan=_channel_dft_mats(L))
    return y.transpose(0, 2, 1, 3).reshape(B, L, C)


def long_conv_pallas(vx, h_fwd, h_bwd):
    B, L, D = vx.shape
    n_major = 2 * L // DFT_MINOR
    taps = _to_slabs(jnp.stack([h_fwd, h_bwd]), n_major // 2)
    tr, ti = dft_forward(taps, n_major, mode="spectrum", precise=True)
    h = (tr[0:1] + tr[1:2], ti[0:1] - ti[1:2])
    gr, gi = dft_forward(_to_slabs(vx, n_major // 2), n_major, mode="mul", h=h)
    return _from_slabs(dft_inverse_real(gr, gi, n_major // 2))


def fourier_mix(u):
    B, L, _ = u.shape
    if L % (DFT_MINOR * DFT_QB) == 0:
        return fourier_mix_pallas(u.astype(F32))
    ug = u.astype(F32).reshape(B, L, FN_GROUPS, FN_GROUP_DIM)
    y = jnp.fft.fft2(ug, axes=(1, 3), norm='ortho').real
    return y.reshape(B, L, FN_WIDTH)


def hyena_filters(L, w1, b1, freq, w2, b2, w3):
    hp = lax.Precision.HIGHEST
    t = jnp.linspace(0.0, 1.0, L, dtype=F32)[:, None]
    w = 2 * math.pi * jnp.arange(L, dtype=F32)[:, None] / L
    bands = jnp.linspace(1e-4, HY_BANDS - 1, HY_BANDS, dtype=F32)[None, :]
    z = jnp.concatenate([t, jnp.cos(bands * w), -jnp.sin(bands * w)], axis=-1)
    hid = jnp.sin(freq * (jnp.dot(z, w1, precision=hp) + b1))
    hid = jnp.sin(freq * (jnp.dot(hid, w2, precision=hp) + b2))
    h = jnp.dot(hid, w3, precision=hp).reshape(L, 2, D_MODEL)
    deltas = jnp.abs(jnp.linspace(HY_MIN_DECAY, HY_MAX_DECAY, D_MODEL, dtype=F32))
    window = jnp.exp(-t * deltas[None, :])
    h = h * window[:, None, :]
    h_fwd = h[:, 0]
    h_bwd = h[:, 1] * (jnp.arange(L) > 0)[:, None]
    norm = jnp.sum(jnp.abs(h_fwd), axis=0, keepdims=True) + jnp.sum(jnp.abs(h_bwd), axis=0, keepdims=True)
    return h_fwd / norm, h_bwd / norm


def hyena_core(u, conv_w, conv_b, taps, skip):
    B, L, _ = u.shape
    up = jnp.pad(u, ((0, 0), (1, 1), (0, 0)))
    u = up[:, :-2] * conv_w[0] + up[:, 1:-1] * conv_w[1] + up[:, 2:] * conv_w[2] + conv_b
    x0, x1, v = jnp.split(u, 3, axis=-1)
    vx = v * x1
    h_fwd, h_bwd = taps
    if (2 * L) % (DFT_MINOR * DFT_QB) == 0:
        y = long_conv_pallas(vx, h_fwd, h_bwd)
    else:
        filt = jnp.concatenate([h_fwd, jnp.zeros((1, h_fwd.shape[1]), F32), h_bwd[1:][::-1]], axis=0)
        y = jnp.fft.irfft(jnp.fft.rfft(vx, n=2 * L, axis=1) * jnp.fft.rfft(filt, n=2 * L, axis=0)[None],
                          n=2 * L, axis=1)[:, :L]
    return x0 * (y + vx * skip)


def kernel(x, c, ctx, c_ctx, mod_w, mod_b, norm_mix_g, norm_ffn_g, mix_w_in, na_rpb, mix_w_out,
           hy_w_in, hy_conv_w, hy_conv_b, hy_f_w1, hy_f_b1, hy_f_freq, hy_f_w2, hy_f_b2, hy_f_w3,
           hy_skip, hy_w_out, router_w, exp_w_gate, exp_w_up, exp_w_down, final_norm_g):
    depth = mod_w.shape[0]
    B, L, D = x.shape
    Lc = ctx.shape[1]
    last_ctx_read = depth - 1 if (depth - 1) % 2 == 0 else depth - 2
    TM = 512

    cs = ctx
    cond = jnp.concatenate([jax.nn.silu(c), jax.nn.silu(c_ctx)[None], jnp.zeros((8 - (B + 1) % 8, D), F32)], 0)

    for layer in range(depth):
        j = layer // 2
        m_all = small_matmul(cond, mod_w[layer], mod_b[layer])
        sh1, sc1, g1, sh2, sc2, g2 = jnp.split(m_all[:B], 6, axis=-1)
        m_ctx = jnp.broadcast_to(m_all[B:B + 1], (B, 6 * D))
        csh1, csc1, cg1, csh2, csc2, cg2 = jnp.split(m_ctx, 6, axis=-1)
        upd = layer < last_ctx_read
        wg = exp_w_gate[layer].astype(BF16)
        wu = exp_w_up[layer].astype(BF16)
        wd = exp_w_down[layer].astype(BF16)

        if layer % 2 == 0:
            w_in = mix_w_in[j].astype(BF16)
            w_out = mix_w_out[j].astype(BF16)
            q_scale = jnp.concatenate([jnp.full((NA_WIDTH,), QK_SCALE, F32),
                                       jnp.ones((w_in.shape[1] - NA_WIDTH,), F32)])
            if upd:
                pc = norm_proj(cs, norm_mix_g[layer], csh1, csc1, w_in, BF16, TM, q_scale)
                kc_off = HEAD_PAIRS
                cmix_in = jnp.concatenate([ctx_attention(pc), fourier_mix(pc[..., 3 * NA_WIDTH:]).astype(BF16)], -1)
            else:
                pc = norm_proj(cs, norm_mix_g[layer], csh1, csc1, w_in[:, NA_WIDTH:3 * NA_WIDTH], BF16, TM)
                kc_off = 0
            p = norm_proj(x, norm_mix_g[layer], sh1, sc1, w_in, BF16, TM, q_scale)
            att = na_attention(p, pc, kc_off, na_rpb[j])
            mix_in = jnp.concatenate([att, fourier_mix(p[..., 3 * NA_WIDTH:]).astype(BF16)], -1)
            x = proj_residual(mix_in, w_out, x, g1, TM)
            if upd:
                cs = proj_residual(cmix_in, w_out, cs, cg1, TM)
        else:
            w_in = hy_w_in[j].astype(BF16)
            w_out = hy_w_out[j].astype(BF16)
            fargs = (hy_f_w1[j], hy_f_b1[j], hy_f_freq[j], hy_f_w2[j], hy_f_b2[j], hy_f_w3[j])
            u = norm_proj(x, norm_mix_g[layer], sh1, sc1, w_in, F32, TM)
            y = hyena_core(u, hy_conv_w[j], hy_conv_b[j], hyena_filters(L, *fargs), hy_skip[j])
            x = proj_residual(y, w_out, x, g1, TM)
            if upd:
                uc = norm_proj(cs, norm_mix_g[layer], csh1, csc1, w_in, F32, TM)
                yc = hyena_core(uc, hy_conv_w[j], hy_conv_b[j], hyena_filters(Lc, *fargs), hy_skip[j])
                cs = proj_residual(yc, w_out, cs, cg1, TM)

        x = expert_choice_ffn(x, norm_ffn_g[layer], sh2, sc2, g2, router_w[layer], wg, wu, wd, TM)
        if upd:
            cs = expert_choice_ffn(cs, norm_ffn_g[layer], csh2, csc2, cg2, router_w[layer], wg, wu, wd, TM)

    return rms_norm_final(x, final_norm_g)
```

```python
import functools
import math

import numpy as np
import jax
import jax.numpy as jnp
from jax import lax
from jax.experimental import pallas as pl
from jax.experimental.pallas import tpu as pltpu

D_MODEL = 1024
GRID_W = 64
HEAD_DIM = 64
NA_HEADS = 12
NA_WIDTH = NA_HEADS * HEAD_DIM
NA_WIN_ROWS = 8
NA_WIN_COLS = 16
FN_GROUPS = 4
FN_GROUP_DIM = 64
FN_WIDTH = FN_GROUPS * FN_GROUP_DIM
HY_SHORT = 3
HY_EMB = 33
HY_BANDS = (HY_EMB - 1) // 2
HY_MIN_DECAY = math.log(1e-2) / 1.5
HY_MAX_DECAY = math.log(1e-2) / 0.3
N_EXPERTS = 16
EC_CAPACITY_FACTOR = 2
RMS_EPS = 1e-6

LANES = 128
HEAD_PAIRS = NA_WIDTH // LANES
NA_Q_ROWS = 2
NEG_INF = -1e30
LOG2E = math.log2(math.e)
QK_SCALE = HEAD_DIM ** -0.5 * LOG2E
VMEM_LIMIT = 56 * 1024 * 1024

BF16 = jnp.bfloat16
F32 = jnp.float32


def _cparams(*sem):
    return pltpu.CompilerParams(dimension_semantics=sem, vmem_limit_bytes=VMEM_LIMIT)


def _small_matmul_kernel(a_ref, w_ref, b_ref, o_ref):
    o_ref[...] = jnp.dot(a_ref[...], w_ref[...], preferred_element_type=F32,
                         precision=lax.Precision.HIGHEST) + b_ref[...]


def small_matmul(a, w, b, tn=1536):
    M, K = a.shape
    N = w.shape[1]
    return pl.pallas_call(
        _small_matmul_kernel,
        grid=(N // tn,),
        in_specs=[pl.BlockSpec((M, K), lambda j: (0, 0)),
                  pl.BlockSpec((K, tn), lambda j: (0, j)),
                  pl.BlockSpec((1, tn), lambda j: (0, j))],
        out_specs=pl.BlockSpec((M, tn), lambda j: (0, j)),
        out_shape=jax.ShapeDtypeStruct((M, N), F32),
        compiler_params=_cparams("arbitrary"),
        name="small_matmul",
    )(a, w, b.reshape(1, N))


def _norm_mod(x, g, shift, scale):
    ms = jnp.mean(x * x, axis=-1, keepdims=True)
    y = x * lax.rsqrt(ms + RMS_EPS) * g
    return y * (1.0 + scale) + shift


def _norm_proj_kernel(x_ref, g_ref, sh_ref, sc_ref, w_ref, cs_ref, o_ref):
    h = _norm_mod(x_ref[0], g_ref[...], sh_ref[0], sc_ref[0])
    y = jnp.dot(h.astype(BF16), w_ref[...], preferred_element_type=F32)
    o_ref[0] = (y * cs_ref[...]).astype(o_ref.dtype)


def norm_proj(x, g, shift, scale, w, out_dtype, tm, col_scale=None):
    B, L, D = x.shape
    N = w.shape[1]
    tm = min(tm, L)
    if col_scale is None:
        col_scale = jnp.ones((N,), F32)
    return pl.pallas_call(
        _norm_proj_kernel,
        grid=(B, L // tm),
        in_specs=[pl.BlockSpec((1, tm, D), lambda b, i: (b, i, 0)),
                  pl.BlockSpec((1, D), lambda b, i: (0, 0)),
                  pl.BlockSpec((1, 1, D), lambda b, i: (b, 0, 0)),
                  pl.BlockSpec((1, 1, D), lambda b, i: (b, 0, 0)),
                  pl.BlockSpec((D, N), lambda b, i: (0, 0)),
                  pl.BlockSpec((1, N), lambda b, i: (0, 0))],
        out_specs=pl.BlockSpec((1, tm, N), lambda b, i: (b, i, 0)),
        out_shape=jax.ShapeDtypeStruct((B, L, N), out_dtype),
        compiler_params=_cparams("arbitrary", "arbitrary"),
        name="norm_proj",
    )(x, g.reshape(1, D), shift.reshape(B, 1, D), scale.reshape(B, 1, D), w, col_scale.reshape(1, N))


def _norm_router_kernel(x_ref, g_ref, sh_ref, sc_ref, rw_ref, h_ref, aff_ref):
    h = _norm_mod(x_ref[0], g_ref[...], sh_ref[0], sc_ref[0])
    h_ref[0] = h.astype(h_ref.dtype)
    logits = jnp.dot(h, rw_ref[...], preferred_element_type=F32, precision=lax.Precision.HIGHEST)
    m = jnp.max(logits, axis=-1, keepdims=True)
    e = jnp.exp(logits - m)
    aff_ref[0] = e / jnp.sum(e, axis=-1, keepdims=True)


def norm_router(x, g, shift, scale, router_w, tm, h_dtype):
    B, L, D = x.shape
    E = router_w.shape[1]
    tm = min(tm, L)
    return pl.pallas_call(
        _norm_router_kernel,
        grid=(B, L // tm),
        in_specs=[pl.BlockSpec((1, tm, D), lambda b, i: (b, i, 0)),
                  pl.BlockSpec((1, D), lambda b, i: (0, 0)),
                  pl.BlockSpec((1, 1, D), lambda b, i: (b, 0, 0)),
                  pl.BlockSpec((1, 1, D), lambda b, i: (b, 0, 0)),
                  pl.BlockSpec((D, E), lambda b, i: (0, 0))],
        out_specs=[pl.BlockSpec((1, tm, D), lambda b, i: (b, i, 0)),
                   pl.BlockSpec((1, tm, E), lambda b, i: (b, i, 0))],
        out_shape=[jax.ShapeDtypeStruct((B, L, D), h_dtype),
                   jax.ShapeDtypeStruct((B, L, E), F32)],
        compiler_params=_cparams("arbitrary", "arbitrary"),
        name="norm_router",
    )(x, g.reshape(1, D), shift.reshape(B, 1, D), scale.reshape(B, 1, D), router_w)


def _proj_residual_kernel(y_ref, w_ref, x_ref, gate_ref, o_ref):
    mix = jnp.dot(y_ref[0].astype(BF16), w_ref[...], preferred_element_type=F32)
    o_ref[0] = x_ref[0] + gate_ref[0] * mix


def proj_residual(y, w, x, gate, tm):
    B, L, K = y.shape
    D = w.shape[1]
    tm = min(tm, L)
    return pl.pallas_call(
        _proj_residual_kernel,
        grid=(B, L // tm),
        in_specs=[pl.BlockSpec((1, tm, K), lambda b, i: (b, i, 0)),
                  pl.BlockSpec((K, D), lambda b, i: (0, 0)),
                  pl.BlockSpec((1, tm, D), lambda b, i: (b, i, 0)),
                  pl.BlockSpec((1, 1, D), lambda b, i: (b, 0, 0))],
        out_specs=pl.BlockSpec((1, tm, D), lambda b, i: (b, i, 0)),
        out_shape=jax.ShapeDtypeStruct((B, L, D), F32),
        compiler_params=_cparams("arbitrary", "arbitrary"),
        name="proj_residual",
    )(y, w, x, gate.reshape(B, 1, D))


def _rms_kernel(x_ref, g_ref, o_ref):
    x = x_ref[0]
    ms = jnp.mean(x * x, axis=-1, keepdims=True)
    o_ref[0] = x * lax.rsqrt(ms + RMS_EPS) * g_ref[...]


def rms_norm_final(x, g, tm=1024):
    B, L, D = x.shape
    return pl.pallas_call(
        _rms_kernel,
        grid=(B, L // tm),
        in_specs=[pl.BlockSpec((1, tm, D), lambda b, i: (b, i, 0)),
                  pl.BlockSpec((1, D), lambda b, i: (0, 0))],
        out_specs=pl.BlockSpec((1, tm, D), lambda b, i: (b, i, 0)),
        out_shape=jax.ShapeDtypeStruct((B, L, D), F32),
        compiler_params=_cparams("arbitrary", "arbitrary"),
        name="rms_final",
    )(x, g.reshape(1, D))


def _two_head_rows(q):
    lane = lax.broadcasted_iota(jnp.int32, q.shape, 1)
    zero = jnp.zeros_like(q)
    return jnp.concatenate([jnp.where(lane < HEAD_DIM, q, zero),
                            jnp.where(lane >= HEAD_DIM, q, zero)], axis=0)


def _merge_two_heads(o):
    n = o.shape[0] // 2
    lane = lax.broadcasted_iota(jnp.int32, (n, LANES), 1)
    return jnp.where(lane < HEAD_DIM, o[:n], o[n:])


_CONTRACT_LAST = (((1,), (1,)), ((), ()))


def _na_key_rows(rq):
    n = rq + NA_WIN_ROWS - 1
    return n + n % 2


def _na_kernel(q_ref, k_ref, v_ref, kc_ref, vc_ref, bias_ref, o_ref, *, rows, rq):
    nq = rq * GRID_W
    nkr = _na_key_rows(rq)
    kwin = nkr * GRID_W
    kc = kc_ref[0]
    vc = vc_ref[0]

    def step(i, carry):
        r = i * rq
        start = jnp.clip(r - NA_WIN_ROWS // 2, 0, rows - nkr)
        variant = r - start
        q0 = pl.multiple_of(r * GRID_W, nq)
        k0 = pl.multiple_of(start * GRID_W, GRID_W)
        q2 = _two_head_rows(q_ref[0, pl.ds(q0, nq), :])
        kw = k_ref[0, pl.ds(k0, kwin), :]
        vw = v_ref[0, pl.ds(k0, kwin), :]
        s_loc = lax.dot_general(q2, kw, _CONTRACT_LAST, preferred_element_type=F32) + bias_ref[0, variant]
        s_ctx = lax.dot_general(q2, kc, _CONTRACT_LAST, preferred_element_type=F32)
        m = jnp.maximum(jnp.max(s_loc, axis=-1, keepdims=True), jnp.max(s_ctx, axis=-1, keepdims=True))
        p_loc = jnp.exp2(s_loc - m)
        p_ctx = jnp.exp2(s_ctx - m)
        denom = jnp.sum(p_loc, axis=-1, keepdims=True) + jnp.sum(p_ctx, axis=-1, keepdims=True)
        o = (jnp.dot(p_loc.astype(BF16), vw, preferred_element_type=F32)
             + jnp.dot(p_ctx.astype(BF16), vc, preferred_element_type=F32))
        o = o / denom
        o_ref[0, pl.ds(q0, nq), :] = _merge_two_heads(o).astype(o_ref.dtype)
        return carry

    lax.fori_loop(0, rows // rq, step, 0, unroll=2)


def _na_bias_tables(rpb, rows, rq):
    W = GRID_W
    nkr = _na_key_rows(rq)
    n_var = nkr - rq + 1
    n_ro, n_co = 2 * NA_WIN_ROWS - 1, 2 * NA_WIN_COLS - 1
    cols = np.arange(W)
    col_start = np.clip(cols - NA_WIN_COLS // 2, 0, W - NA_WIN_COLS)
    row_sel = np.zeros((n_var, rq, nkr, n_ro), np.float32)
    for var in range(n_var):
        r = var if var <= NA_WIN_ROWS // 2 else rows - nkr + var
        start = r - var
        for dr in range(rq):
            q_row = r + dr
            if q_row >= rows:
                continue
            rs = int(np.clip(q_row - NA_WIN_ROWS // 2, 0, rows - NA_WIN_ROWS))
            for kr in range(nkr):
                key_row = start + kr
                if rs <= key_row < rs + NA_WIN_ROWS:
                    row_sel[var, dr, kr, key_row - q_row + NA_WIN_ROWS - 1] = 1.0
    col_sel = np.zeros((n_co, W, W), np.float32)
    for q in range(W):
        for kc in range(col_start[q], col_start[q] + NA_WIN_COLS):
            col_sel[kc - q + NA_WIN_COLS - 1, q, kc] = 1.0
    inside = (row_sel.sum(-1)[:, :, :, None, None] * col_sel.sum(0)[None, None, None]) > 0
    hp = lax.Precision.HIGHEST
    by_row = jnp.einsum('vdko,hoc->hvdkc', row_sel, rpb.astype(F32), precision=hp)
    tab = jnp.einsum('hvdkc,cqn->hvdkqn', by_row, col_sel, precision=hp)
    tab = jnp.where(inside[None], tab * LOG2E, NEG_INF)
    tab = tab.reshape(HEAD_PAIRS, 2, n_var, rq, nkr, W, W).transpose(0, 2, 1, 3, 5, 4, 6)
    return tab.reshape(HEAD_PAIRS, n_var, 2 * rq * W, nkr * W)


def na_attention(p, pc, kc_off, rpb):
    B, L, _ = p.shape
    Lc = pc.shape[1]
    rows = L // GRID_W
    rq = NA_Q_ROWS
    bias = _na_bias_tables(rpb, rows, rq)
    nkr = _na_key_rows(rq)
    kern = functools.partial(_na_kernel, rows=rows, rq=rq)
    return pl.pallas_call(
        kern,
        grid=(B, HEAD_PAIRS),
        in_specs=[pl.BlockSpec((1, L, LANES), lambda b, j: (b, 0, j)),
                  pl.BlockSpec((1, L, LANES), lambda b, j: (b, 0, HEAD_PAIRS + j)),
                  pl.BlockSpec((1, L, LANES), lambda b, j: (b, 0, 2 * HEAD_PAIRS + j)),
                  pl.BlockSpec((1, Lc, LANES), lambda b, j: (b, 0, kc_off + j)),
                  pl.BlockSpec((1, Lc, LANES), lambda b, j: (b, 0, kc_off + HEAD_PAIRS + j)),
                  pl.BlockSpec((1, nkr - rq + 1, 2 * rq * GRID_W, nkr * GRID_W), lambda b, j: (j, 0, 0, 0))],
        out_specs=pl.BlockSpec((1, L, LANES), lambda b, j: (b, 0, j)),
        out_shape=jax.ShapeDtypeStruct((B, L, NA_WIDTH), BF16),
        compiler_params=_cparams("arbitrary", "arbitrary"),
        name="na_attention",
    )(p, p, p, pc, pc, bias)


def _ctx_attn_kernel(q_ref, k_ref, v_ref, o_ref):
    q2 = _two_head_rows(q_ref[0])
    s = lax.dot_general(q2, k_ref[0], _CONTRACT_LAST, preferred_element_type=F32)
    m = jnp.max(s, axis=-1, keepdims=True)
    e = jnp.exp2(s - m)
    o = jnp.dot(e.astype(BF16), v_ref[0], preferred_element_type=F32) / jnp.sum(e, axis=-1, keepdims=True)
    o_ref[0] = _merge_two_heads(o).astype(o_ref.dtype)


def ctx_attention(pc):
    B, Lc, _ = pc.shape
    return pl.pallas_call(
        _ctx_attn_kernel,
        grid=(B, HEAD_PAIRS),
        in_specs=[pl.BlockSpec((1, Lc, LANES), lambda b, j: (b, 0, j)),
                  pl.BlockSpec((1, Lc, LANES), lambda b, j: (b, 0, HEAD_PAIRS + j)),
                  pl.BlockSpec((1, Lc, LANES), lambda b, j: (b, 0, 2 * HEAD_PAIRS + j))],
        out_specs=pl.BlockSpec((1, Lc, LANES), lambda b, j: (b, 0, j)),
        out_shape=jax.ShapeDtypeStruct((B, Lc, NA_WIDTH), BF16),
        compiler_params=_cparams("arbitrary", "arbitrary"),
        name="ctx_attention",
    )(pc, pc, pc)


def _expert_ffn_kernel(x_ref, gate_ref, wg_ref, wu_ref, wd_ref, o_ref, *, fchunk):
    o_ref[0, 0] = _swiglu(x_ref[0, 0], wg_ref, wu_ref, wd_ref, fchunk) * gate_ref[0, 0]


def expert_ffn(xe, gate, wg, wu, wd, fchunk=512):
    E, B, cap, D = xe.shape
    F = wg.shape[2]
    kern = functools.partial(_expert_ffn_kernel, fchunk=fchunk)
    return pl.pallas_call(
        kern,
        grid=(E, B),
        in_specs=[pl.BlockSpec((1, 1, cap, D), lambda e, b: (e, b, 0, 0)),
                  pl.BlockSpec((1, 1, cap, 1), lambda e, b: (e, b, 0, 0)),
                  pl.BlockSpec((1, D, F), lambda e, b: (e, 0, 0)),
                  pl.BlockSpec((1, D, F), lambda e, b: (e, 0, 0)),
                  pl.BlockSpec((1, F, D), lambda e, b: (e, 0, 0))],
        out_specs=pl.BlockSpec((1, 1, cap, D), lambda e, b: (e, b, 0, 0)),
        out_shape=jax.ShapeDtypeStruct((E, B, cap, D), F32),
        compiler_params=_cparams("arbitrary", "arbitrary"),
        name="expert_ffn",
    )(xe, gate, wg, wu, wd)


def _swiglu(x, wg_ref, wu_ref, wd_ref, fchunk):
    F = wg_ref.shape[2]
    acc = jnp.zeros((x.shape[0], wd_ref.shape[2]), F32)
    for f0 in range(0, F, fchunk):
        g = jnp.dot(x, wg_ref[0, :, f0:f0 + fchunk], preferred_element_type=F32)
        u = jnp.dot(x, wu_ref[0, :, f0:f0 + fchunk], preferred_element_type=F32)
        hid = (g * jax.nn.sigmoid(g) * u).astype(BF16)
        acc = acc + jnp.dot(hid, wd_ref[0, f0:f0 + fchunk, :], preferred_element_type=F32)
    return acc


def _gather_rows_copy(h_hbm, xbuf, sem, slot, row, j):
    return pltpu.make_async_copy(h_hbm.at[pl.ds(row, 1)], xbuf.at[slot, pl.ds(j, 1)], sem.at[slot])


def _expert_ffn_gather_kernel(rows_ref, rows_next_ref, gate_ref, h_hbm, wg_ref, wu_ref, wd_ref, o_ref,
                              xbuf, xb_ref, acc_ref, sem, *, fchunk):
    cap = xbuf.shape[1]
    step = pl.program_id(0) * pl.num_programs(1) + pl.program_id(1)
    nsteps = pl.num_programs(0) * pl.num_programs(1)
    slot = step % 2

    def drain(s):
        pltpu.make_async_copy(h_hbm.at[pl.ds(0, cap)], xbuf.at[s], sem.at[s]).wait()

    @pl.when(step == 0)
    def _():
        def body(j, carry):
            _gather_rows_copy(h_hbm, xbuf, sem, 0, rows_ref[0, 0, j], j).start()
            return carry
        lax.fori_loop(0, cap, body, 0)

    drain(slot)
    xb_ref[...] = xbuf[slot].astype(BF16)
    acc_ref[...] = jnp.zeros_like(acc_ref)
    n_chunks = wg_ref.shape[2] // fchunk
    rows_per_chunk = cap // n_chunks

    def chunk(k, carry):
        for jj in range(rows_per_chunk):
            j = k * rows_per_chunk + jj
            _gather_rows_copy(h_hbm, xbuf, sem, 1 - slot, rows_next_ref[0, 0, j], j).start()
        f0 = pl.multiple_of(k * fchunk, fchunk)
        x = xb_ref[...]
        g = jnp.dot(x, wg_ref[0, :, pl.ds(f0, fchunk)], preferred_element_type=F32)
        u = jnp.dot(x, wu_ref[0, :, pl.ds(f0, fchunk)], preferred_element_type=F32)
        hid = (g * jax.nn.sigmoid(g) * u).astype(BF16)
        acc_ref[...] += jnp.dot(hid, wd_ref[0, pl.ds(f0, fchunk), :], preferred_element_type=F32)
        return carry
    lax.fori_loop(0, n_chunks, chunk, 0)
    o_ref[0, 0] = (acc_ref[...] * gate_ref[0, 0]).astype(o_ref.dtype)

    @pl.when(step == nsteps - 1)
    def _():
        drain(1 - slot)


def expert_ffn_gather(h_rows, rows, gate, wg, wu, wd, fchunk=512):
    E, B, cap = rows.shape
    D = h_rows.shape[1]
    F = wg.shape[2]

    rows = rows.reshape(E * B, 1, cap)
    fchunk = min(fchunk, F)

    def next_group(e, b):
        return (jnp.minimum(e * B + b + 1, E * B - 1), 0, 0)

    kern = functools.partial(_expert_ffn_gather_kernel, fchunk=fchunk)
    return pl.pallas_call(
        kern,
        grid=(E, B),
        in_specs=[pl.BlockSpec((1, 1, cap), lambda e, b: (e * B + b, 0, 0), memory_space=pltpu.SMEM),
                  pl.BlockSpec((1, 1, cap), next_group, memory_space=pltpu.SMEM),
                  pl.BlockSpec((1, 1, cap, 1), lambda e, b: (e, b, 0, 0)),
                  pl.BlockSpec(memory_space=pl.ANY),
                  pl.BlockSpec((1, D, F), lambda e, b: (e, 0, 0)),
                  pl.BlockSpec((1, D, F), lambda e, b: (e, 0, 0)),
                  pl.BlockSpec((1, F, D), lambda e, b: (e, 0, 0))],
        out_specs=pl.BlockSpec((1, 1, cap, D), lambda e, b: (e, b, 0, 0)),
        out_shape=jax.ShapeDtypeStruct((E, B, cap, D), BF16),
        scratch_shapes=[pltpu.VMEM((2, cap, D), F32), pltpu.VMEM((cap, D), BF16), pltpu.VMEM((cap, D), F32),
                        pltpu.SemaphoreType.DMA((2,))],
        compiler_params=_cparams("arbitrary", "arbitrary"),
        name="expert_ffn_gather",
    )(rows, rows, gate, h_rows, wg, wu, wd)


COMBINE_TOKENS = 512
COMBINE_WINDOW = 256
COMBINE_CHUNK = LANES


def _window_copy(ye_hbm, win, sem, e, b, w0, slot):
    return pltpu.make_async_copy(ye_hbm.at[e, b, pl.ds(w0, COMBINE_WINDOW)], win.at[slot], sem.at[slot])


def _moe_combine_kernel(starts_ref, tok_ref, x_ref, gate_ref, ye_hbm, o_ref, acc_ref, win, sem):
    b = pl.program_id(0)
    i = pl.program_id(1)
    n_exp = tok_ref.shape[0]
    cap = ye_hbm.shape[2]
    T = x_ref.shape[1]
    tile0 = i * T
    tok_iota = lax.broadcasted_iota(jnp.int32, (T, COMBINE_CHUNK), 0) + tile0

    nt = pl.num_programs(1)
    step = b * nt + i
    parity = step % 2

    def first_window(e, bb=b, ii=i):
        s = starts_ref[bb, e, ii]
        return jnp.minimum((s // COMBINE_CHUNK) * COMBINE_CHUNK, cap - COMBINE_WINDOW)

    def fetch_first_windows(bb, ii, par):
        for e in range(n_exp):
            _window_copy(ye_hbm, win, sem, e, bb, first_window(e, bb, ii), par * n_exp + e).start()

    def one_hot(e, w0, first_row=None):
        k = w0 // COMBINE_CHUNK
        hot = []
        for c in range(COMBINE_WINDOW // COMBINE_CHUNK):
            match = tok_iota == tok_ref[e, 0, pl.ds(k + c, 1), :]
            if first_row is not None:
                row = lax.broadcasted_iota(jnp.int32, match.shape, 1) + (w0 + c * COMBINE_CHUNK)
                match = match & (row >= first_row)
            hot.append(match.astype(BF16))
        return hot

    @pl.when(step == 0)
    def _():
        fetch_first_windows(b, i, 0)

    @pl.when(step + 1 < pl.num_programs(0) * nt)
    def _():
        fetch_first_windows((step + 1) // nt, (step + 1) % nt, 1 - parity)

    firsts = [first_window(e) for e in range(n_exp)]
    hot = []
    for e in range(n_exp):
        hot += one_hot(e, firsts[e])
    for e in range(n_exp):
        _window_copy(ye_hbm, win, sem, e, b, firsts[e], parity * n_exp + e).wait()
    rows_all = win[pl.ds(parity * n_exp, n_exp)].reshape(n_exp * COMBINE_WINDOW, win.shape[2])
    acc_ref[...] = jnp.dot(jnp.concatenate(hot, axis=1), rows_all, preferred_element_type=F32)

    def per_expert(e, carry):
        w0 = first_window(e)
        end = starts_ref[b, e, i + 1]
        n_more = jnp.maximum(end - (w0 + COMBINE_WINDOW) + COMBINE_WINDOW - 1, 0) // COMBINE_WINDOW

        def more(m, c):
            first_row = w0 + (m + 1) * COMBINE_WINDOW
            w = jnp.minimum(first_row, cap - COMBINE_WINDOW)
            cp = _window_copy(ye_hbm, win, sem, e, b, w, 2 * n_exp)
            cp.start()
            cp.wait()
            sel = jnp.concatenate(one_hot(e, w, first_row), axis=1)
            acc_ref[...] += jnp.dot(sel, win[2 * n_exp], preferred_element_type=F32)
            return c
        lax.fori_loop(0, n_more, more, 0)
        return carry

    lax.fori_loop(0, n_exp, per_expert, 0)
    o_ref[0] = x_ref[0] + gate_ref[0] * acc_ref[...]


def moe_combine(x, gate_res, ye, tok):
    B, n, D = x.shape
    E, _, cap, _ = ye.shape
    T = COMBINE_TOKENS
    nt = n // T
    bounds = jnp.arange(nt + 1, dtype=jnp.int32) * T
    starts = jnp.sum(tok[:, :, None, :] < bounds[None, None, :, None], axis=-1, dtype=jnp.int32)
    starts = jnp.transpose(starts, (1, 0, 2))
    tok4 = tok.reshape(E, B, cap // COMBINE_CHUNK, COMBINE_CHUNK)
    grid_spec = pltpu.PrefetchScalarGridSpec(
        num_scalar_prefetch=1,
        grid=(B, nt),
        in_specs=[pl.BlockSpec((E, 1, cap // COMBINE_CHUNK, COMBINE_CHUNK), lambda b, i, s: (0, b, 0, 0)),
                  pl.BlockSpec((1, T, D), lambda b, i, s: (b, i, 0)),
                  pl.BlockSpec((1, 1, D), lambda b, i, s: (b, 0, 0)),
                  pl.BlockSpec(memory_space=pl.ANY)],
        out_specs=pl.BlockSpec((1, T, D), lambda b, i, s: (b, i, 0)),
        scratch_shapes=[pltpu.VMEM((T, D), F32), pltpu.VMEM((2 * E + 1, COMBINE_WINDOW, D), BF16),
                        pltpu.SemaphoreType.DMA((2 * E + 1,))])
    return pl.pallas_call(
        _moe_combine_kernel,
        grid_spec=grid_spec,
        out_shape=jax.ShapeDtypeStruct((B, n, D), F32),
        compiler_params=_cparams("arbitrary", "arbitrary"),
        name="moe_combine",
    )(starts, tok4, x, gate_res.reshape(B, 1, D), ye)


def expert_choice_ffn(x, g, shift, scale, gate_res, router_w, wg, wu, wd, tm):
    B, n, D = x.shape
    E = N_EXPERTS
    cap = EC_CAPACITY_FACTOR * n // E
    fused = cap % COMBINE_WINDOW == 0 and cap > COMBINE_WINDOW and n % COMBINE_TOKENS == 0
    h, aff = norm_router(x, g, shift, scale, router_w, tm, F32 if fused else BF16)
    gate, idx = lax.top_k(jnp.transpose(aff, (2, 0, 1)), cap)
    bidx = jnp.arange(B, dtype=idx.dtype)[None, :, None]
    if fused:
        idx, gate = lax.sort((idx, gate), dimension=2, num_keys=1)
        ye = expert_ffn_gather(h.reshape(B * n, D), idx + bidx * n, gate[..., None], wg, wu, wd)
        return moe_combine(x, gate_res, ye, idx)
    xe = h[bidx, idx].reshape(E, 1, B * cap, D)
    ye = expert_ffn(xe, gate.reshape(E, 1, B * cap, 1), wg, wu, wd)
    flat = (idx + bidx * n).reshape(-1)
    moe = jnp.zeros((B * n, D), F32).at[flat].add(ye.reshape(-1, D)).reshape(B, n, D)
    return x + gate_res[:, None, :] * moe


DFT_MINOR = 128
DFT_PITCH_PAD = 8
DFT_QB = 32
DFT_UNROLL = 8


def _dft_mats(n_major, a_used):
    N = n_major * DFT_MINOR
    q = np.arange(n_major)
    a = np.arange(a_used)
    b = np.arange(DFT_MINOR)
    ang1 = -2 * np.pi * np.outer(q, a) / n_major
    f1 = np.concatenate([np.cos(ang1), np.sin(ang1)], 0)
    ang2 = -2 * np.pi * np.outer(b, b) / DFT_MINOR
    c2, s2 = np.cos(ang2), np.sin(ang2)
    f2 = np.block([[c2, -s2], [s2, c2]])
    f2c = np.block([[c2, s2], [-s2, c2]])
    lane1 = np.ones((1, 1, LANES))

    def tw(hi, lo):
        ang = -2 * np.pi * np.outer(hi, lo) / N
        return (np.cos(ang)[:, :, None] * lane1, np.sin(ang)[:, :, None] * lane1)

    t1r, t1i = tw(8 * np.arange(DFT_MINOR // 8), q)
    t0r, t0i = tw(np.arange(8), q)
    u1r, u1i = tw(8 * np.arange(n_major // 8), b)
    u0r, u0i = tw(np.arange(8), b)
    f32 = lambda *xs: [np.asarray(x, np.float32) for x in xs]
    return dict(f1=f1, f2=f2, f2c=f2c, tw_fwd=f32(t1r, t1i, t0r, t0i), tw_inv=f32(u1r, u1i, u0r, u0i))


def _cmul(ar, ai, br, bi):
    return ar * br - ai * bi, ar * bi + ai * br


def _dft_fwd_kernel(*refs, n_major, a_used, mode, precision):
    x_ref, f1_ref, f2_ref, t1r, t1i, t0r, t0i = refs[:7]
    rest = refs[7:]
    if mode == "mul":
        hr_ref, hi_ref, or_ref, oi_ref, sr, si = rest
    elif mode == "chanmix":
        mc_ref, ms_ref, or_ref, sr, si = rest
    else:
        or_ref, oi_ref, sr, si = rest
    pitch = n_major + DFT_PITCH_PAD
    qb = pl.program_id(2)
    mm_dtype = f1_ref.dtype

    @pl.when(qb == 0)
    def _stage1():
        def slab(b, carry):
            y = jnp.dot(f1_ref[...], x_ref[0, b].astype(mm_dtype), preferred_element_type=F32,
                        precision=precision)
            twr, twi = _cmul(t1r[b // 8], t1i[b // 8], t0r[b % 8], t0i[b % 8])
            yr, yi = _cmul(y[:n_major], y[n_major:], twr, twi)
            row = pl.multiple_of(b * pitch, 8)
            sr[pl.ds(row, n_major), :] = yr
            si[pl.ds(row, n_major), :] = yi
            return carry
        lax.fori_loop(0, DFT_MINOR, slab, 0, unroll=DFT_UNROLL)

    def freq(j, carry):
        q = qb * DFT_QB + j
        g = jnp.concatenate([sr[pl.ds(q, DFT_MINOR, stride=pitch), :],
                             si[pl.ds(q, DFT_MINOR, stride=pitch), :]], axis=0).astype(mm_dtype)
        xf = jnp.dot(f2_ref[...], g, preferred_element_type=F32, precision=precision)
        xr, xi = xf[:DFT_MINOR], xf[DFT_MINOR:]
        if mode == "mul":
            xr, xi = _cmul(xr, xi, hr_ref[0, j].astype(F32), hi_ref[0, j].astype(F32))
            or_ref[0, j] = xr.astype(or_ref.dtype)
            oi_ref[0, j] = xi.astype(oi_ref.dtype)
        elif mode == "chanmix":
            or_ref[0, j] = (jnp.dot(xr.astype(mm_dtype), mc_ref[...], preferred_element_type=F32)
                            + jnp.dot(xi.astype(mm_dtype), ms_ref[...], preferred_element_type=F32))
        else:
            or_ref[0, j] = xr
            oi_ref[0, j] = xi
        return carry
    lax.fori_loop(0, DFT_QB, freq, 0, unroll=DFT_UNROLL)


def dft_forward(xp, n_major, mode="spectrum", h=None, chan=None, precise=False):
    Bb, _, a_used, C = xp.shape
    mats = _dft_mats(n_major, a_used)
    mm_dtype = F32 if precise else BF16
    precision = lax.Precision.HIGHEST if precise else None
    pitch = n_major + DFT_PITCH_PAD
    const = lambda shape: pl.BlockSpec(shape, lambda bb, cb, qb: (0,) * len(shape))
    spec_blk = pl.BlockSpec((1, DFT_QB, DFT_MINOR, LANES), lambda bb, cb, qb: (bb, qb, 0, cb))
    args = [xp, jnp.asarray(mats["f1"], mm_dtype), jnp.asarray(mats["f2"], mm_dtype)] + mats["tw_fwd"]
    in_specs = [pl.BlockSpec((1, DFT_MINOR, a_used, LANES), lambda bb, cb, qb: (bb, 0, 0, cb)),
                const((2 * n_major, a_used)), const((2 * DFT_MINOR, 2 * DFT_MINOR)),
                const((DFT_MINOR // 8, n_major, LANES)), const((DFT_MINOR // 8, n_major, LANES)),
                const((8, n_major, LANES)), const((8, n_major, LANES))]
    spec_shape = (Bb, n_major, DFT_MINOR, C)
    if mode == "mul":
        args += [h[0], h[1]]
        hblk = pl.BlockSpec((1, DFT_QB, DFT_MINOR, LANES), lambda bb, cb, qb: (0, qb, 0, cb))
        in_specs += [hblk, hblk]
        out_specs = [spec_blk, spec_blk]
        out_shape = [jax.ShapeDtypeStruct(spec_shape, BF16)] * 2
    elif mode == "chanmix":
        args += [chan[0].astype(mm_dtype), chan[1].astype(mm_dtype)]
        in_specs += [const((LANES, LANES)), const((LANES, LANES))]
        out_specs = spec_blk
        out_shape = jax.ShapeDtypeStruct(spec_shape, F32)
    else:
        out_specs = [spec_blk, spec_blk]
        out_shape = [jax.ShapeDtypeStruct(spec_shape, F32)] * 2
    kern = functools.partial(_dft_fwd_kernel, n_major=n_major, a_used=a_used, mode=mode, precision=precision)
    return pl.pallas_call(
        kern,
        grid=(Bb, C // LANES, n_major // DFT_QB),
        in_specs=in_specs,
        out_specs=out_specs,
        out_shape=out_shape,
        scratch_shapes=[pltpu.VMEM((DFT_MINOR * pitch, LANES), F32)] * 2,
        compiler_params=_cparams("arbitrary", "arbitrary", "arbitrary"),
        name="dft_forward_" + mode,
    )(*args)


def _dft_inv_kernel(gr_ref, gi_ref, f2c_ref, f1c_ref, u1r, u1i, u0r, u0i, o_ref, sr, si, *, n_major, a_out):
    pitch = DFT_MINOR + DFT_PITCH_PAD
    qb = pl.program_id(2)

    def freq(j, carry):
        q = qb * DFT_QB + j
        g = jnp.concatenate([gr_ref[0, j], gi_ref[0, j]], axis=0)
        t = jnp.dot(f2c_ref[...], g, preferred_element_type=F32)
        twr, twi = _cmul(u1r[q // 8], u1i[q // 8], u0r[q % 8], u0i[q % 8])
        tr, ti = _cmul(t[:DFT_MINOR], t[DFT_MINOR:], twr, -twi)
        row = pl.multiple_of(q * pitch, 8)
        sr[pl.ds(row, DFT_MINOR), :] = tr
        si[pl.ds(row, DFT_MINOR), :] = ti
        return carry
    lax.fori_loop(0, DFT_QB, freq, 0, unroll=DFT_UNROLL)

    @pl.when(qb == pl.num_programs(2) - 1)
    def _stage2():
        def slab(b, carry):
            g = jnp.concatenate([sr[pl.ds(b, n_major, stride=pitch), :],
                                 si[pl.ds(b, n_major, stride=pitch), :]], axis=0).astype(BF16)
            o_ref[0, b] = jnp.dot(f1c_ref[...], g, preferred_element_type=F32)
            return carry
        lax.fori_loop(0, DFT_MINOR, slab, 0, unroll=DFT_UNROLL)


def dft_inverse_real(gr, gi, a_out):
    Bb, n_major, _, C = gr.shape
    N = n_major * DFT_MINOR
    mats = _dft_mats(n_major, a_out)
    ang = 2 * np.pi * np.outer(np.arange(a_out), np.arange(n_major)) / n_major
    f1c = np.concatenate([np.cos(ang), -np.sin(ang)], 1) / N
    pitch = DFT_MINOR + DFT_PITCH_PAD
    const = lambda shape: pl.BlockSpec(shape, lambda bb, cb, qb: (0,) * len(shape))
    gblk = pl.BlockSpec((1, DFT_QB, DFT_MINOR, LANES), lambda bb, cb, qb: (bb, qb, 0, cb))
    kern = functools.partial(_dft_inv_kernel, n_major=n_major, a_out=a_out)
    return pl.pallas_call(
        kern,
        grid=(Bb, C // LANES, n_major // DFT_QB),
        in_specs=[gblk, gblk, const((2 * DFT_MINOR, 2 * DFT_MINOR)), const((a_out, 2 * n_major)),
                  const((n_major // 8, DFT_MINOR, LANES)), const((n_major // 8, DFT_MINOR, LANES)),
                  const((8, DFT_MINOR, LANES)), const((8, DFT_MINOR, LANES))],
        out_specs=pl.BlockSpec((1, DFT_MINOR, a_out, LANES), lambda bb, cb, qb: (bb, 0, 0, cb)),
        out_shape=jax.ShapeDtypeStruct((Bb, DFT_MINOR, a_out, C), F32),
        scratch_shapes=[pltpu.VMEM((n_major * pitch, LANES), F32)] * 2,
        compiler_params=_cparams("arbitrary", "arbitrary", "arbitrary"),
        name="dft_inverse",
    )(gr, gi, jnp.asarray(mats["f2c"], BF16), jnp.asarray(f1c, BF16), *mats["tw_inv"])


def _to_slabs(x, a_used):
    Bb, _, C = x.shape
    return x.reshape(Bb, a_used, DFT_MINOR, C).transpose(0, 2, 1, 3)


def _from_slabs(xs):
    Bb, m, a, C = xs.shape
    return xs.transpose(0, 2, 1, 3).reshape(Bb, a * m, C)


def _channel_dft_mats(L):
    c = np.arange(FN_GROUP_DIM)
    ang = -2 * np.pi * np.outer(c, c) / FN_GROUP_DIM
    eye = np.eye(LANES // FN_GROUP_DIM)
    norm = 1.0 / math.sqrt(L * FN_GROUP_DIM)
    return (jnp.asarray(np.kron(eye, np.cos(ang)) * norm, F32),
            jnp.asarray(np.kron(eye, -np.sin(ang)) * norm, F32))


def fourier_mix_pallas(u):
    B, L, C = u.shape
    n_major = L // DFT_MINOR
    y = dft_forward(_to_slabs(u, n_major), n_major, mode="chanmix", chan=_channel_dft_mats(L))
    return y.transpose(0, 2, 1, 3).reshape(B, L, C)


def long_conv_pallas(vx, h_fwd, h_bwd):
    B, L, D = vx.shape
    n_major = 2 * L // DFT_MINOR
    taps = _to_slabs(jnp.stack([h_fwd, h_bwd]), n_major // 2)
    tr, ti = dft_forward(taps, n_major, mode="spectrum")
    h = ((tr[0:1] + tr[1:2]).astype(BF16), (ti[0:1] - ti[1:2]).astype(BF16))
    gr, gi = dft_forward(_to_slabs(vx, n_major // 2), n_major, mode="mul", h=h)
    return _from_slabs(dft_inverse_real(gr, gi, n_major // 2))


def fourier_mix(u):
    B, L, _ = u.shape
    if L % (DFT_MINOR * DFT_QB) == 0:
        return fourier_mix_pallas(u.astype(F32))
    ug = u.astype(F32).reshape(B, L, FN_GROUPS, FN_GROUP_DIM)
    y = jnp.fft.fft2(ug, axes=(1, 3), norm='ortho').real
    return y.reshape(B, L, FN_WIDTH)


def hyena_filters(L, w1, b1, freq, w2, b2, w3):
    hp = lax.Precision.HIGHEST
    t = jnp.linspace(0.0, 1.0, L, dtype=F32)[:, None]
    w = 2 * math.pi * jnp.arange(L, dtype=F32)[:, None] / L
    bands = jnp.linspace(1e-4, HY_BANDS - 1, HY_BANDS, dtype=F32)[None, :]
    z = jnp.concatenate([t, jnp.cos(bands * w), -jnp.sin(bands * w)], axis=-1)
    hid = jnp.sin(freq * (jnp.dot(z, w1, precision=hp) + b1))
    hid = jnp.sin(freq * (jnp.dot(hid, w2, precision=hp) + b2))
    h = jnp.dot(hid, w3, precision=hp).reshape(L, 2, D_MODEL)
    deltas = jnp.abs(jnp.linspace(HY_MIN_DECAY, HY_MAX_DECAY, D_MODEL, dtype=F32))
    window = jnp.exp(-t * deltas[None, :])
    h = h * window[:, None, :]
    h_fwd = h[:, 0]
    h_bwd = h[:, 1] * (jnp.arange(L) > 0)[:, None]
    norm = jnp.sum(jnp.abs(h_fwd), axis=0, keepdims=True) + jnp.sum(jnp.abs(h_bwd), axis=0, keepdims=True)
    return h_fwd / norm, h_bwd / norm


def hyena_core(u, conv_w, conv_b, taps, skip):
    B, L, _ = u.shape
    up = jnp.pad(u, ((0, 0), (1, 1), (0, 0)))
    u = up[:, :-2] * conv_w[0] + up[:, 1:-1] * conv_w[1] + up[:, 2:] * conv_w[2] + conv_b
    x0, x1, v = jnp.split(u, 3, axis=-1)
    vx = v * x1
    h_fwd, h_bwd = taps
    if (2 * L) % (DFT_MINOR * DFT_QB) == 0:
        y = long_conv_pallas(vx, h_fwd, h_bwd)
    else:
        filt = jnp.concatenate([h_fwd, jnp.zeros((1, h_fwd.shape[1]), F32), h_bwd[1:][::-1]], axis=0)
        y = jnp.fft.irfft(jnp.fft.rfft(vx, n=2 * L, axis=1) * jnp.fft.rfft(filt, n=2 * L, axis=0)[None],
                          n=2 * L, axis=1)[:, :L]
    return x0 * (y + vx * skip)


def kernel(x, c, ctx, c_ctx, mod_w, mod_b, norm_mix_g, norm_ffn_g, mix_w_in, na_rpb, mix_w_out,
           hy_w_in, hy_conv_w, hy_conv_b, hy_f_w1, hy_f_b1, hy_f_freq, hy_f_w2, hy_f_b2, hy_f_w3,
           hy_skip, hy_w_out, router_w, exp_w_gate, exp_w_up, exp_w_down, final_norm_g):
    depth = mod_w.shape[0]
    B, L, D = x.shape
    Lc = ctx.shape[1]
    last_ctx_read = depth - 1 if (depth - 1) % 2 == 0 else depth - 2
    TM = 512

    cs = ctx
    cond = jnp.concatenate([jax.nn.silu(c), jax.nn.silu(c_ctx)[None], jnp.zeros((8 - (B + 1) % 8, D), F32)], 0)

    for layer in range(depth):
        j = layer // 2
        m_all = small_matmul(cond, mod_w[layer], mod_b[layer])
        sh1, sc1, g1, sh2, sc2, g2 = jnp.split(m_all[:B], 6, axis=-1)
        m_ctx = jnp.broadcast_to(m_all[B:B + 1], (B, 6 * D))
        csh1, csc1, cg1, csh2, csc2, cg2 = jnp.split(m_ctx, 6, axis=-1)
        upd = layer < last_ctx_read
        wg = exp_w_gate[layer].astype(BF16)
        wu = exp_w_up[layer].astype(BF16)
        wd = exp_w_down[layer].astype(BF16)

        if layer % 2 == 0:
            w_in = mix_w_in[j].astype(BF16)
            w_out = mix_w_out[j].astype(BF16)
            q_scale = jnp.concatenate([jnp.full((NA_WIDTH,), QK_SCALE, F32),
                                       jnp.ones((w_in.shape[1] - NA_WIDTH,), F32)])
            if upd:
                pc = norm_proj(cs, norm_mix_g[layer], csh1, csc1, w_in, BF16, TM, q_scale)
                kc_off = HEAD_PAIRS
                cmix_in = jnp.concatenate([ctx_attention(pc), fourier_mix(pc[..., 3 * NA_WIDTH:]).astype(BF16)], -1)
            else:
                pc = norm_proj(cs, norm_mix_g[layer], csh1, csc1, w_in[:, NA_WIDTH:3 * NA_WIDTH], BF16, TM)
                kc_off = 0
            p = norm_proj(x, norm_mix_g[layer], sh1, sc1, w_in, BF16, TM, q_scale)
            att = na_attention(p, pc, kc_off, na_rpb[j])
            mix_in = jnp.concatenate([att, fourier_mix(p[..., 3 * NA_WIDTH:]).astype(BF16)], -1)
            x = proj_residual(mix_in, w_out, x, g1, TM)
            if upd:
                cs = proj_residual(cmix_in, w_out, cs, cg1, TM)
        else:
            w_in = hy_w_in[j].astype(BF16)
            w_out = hy_w_out[j].astype(BF16)
            fargs = (hy_f_w1[j], hy_f_b1[j], hy_f_freq[j], hy_f_w2[j], hy_f_b2[j], hy_f_w3[j])
            u = norm_proj(x, norm_mix_g[layer], sh1, sc1, w_in, F32, TM)
            y = hyena_core(u, hy_conv_w[j], hy_conv_b[j], hyena_filters(L, *fargs), hy_skip[j])
            x = proj_residual(y, w_out, x, g1, TM)
            if upd:
                uc = norm_proj(cs, norm_mix_g[layer], csh1, csc1, w_in, F32, TM)
                yc = hyena_core(uc, hy_conv_w[j], hy_conv_b[j], hyena_filters(Lc, *fargs), hy_skip[j])
                cs = proj_residual(yc, w_out, cs, cg1, TM)

        x = expert_choice_ffn(x, norm_ffn_g[layer], sh2, sc2, g2, router_w[layer], wg, wu, wd, TM)
        if upd:
            cs = expert_choice_ffn(cs, norm_ffn_g[layer], csh2, csc2, cg2, router_w[layer], wg, wu, wd, TM)

    return rms_norm_final(x, final_norm_g)
```

```python
import functools
import math

import numpy as np
import jax
import jax.numpy as jnp
from jax import lax
from jax.experimental import pallas as pl
from jax.experimental.pallas import tpu as pltpu

D_MODEL = 1024
GRID_W = 64
HEAD_DIM = 64
NA_HEADS = 12
NA_WIDTH = NA_HEADS * HEAD_DIM
NA_WIN_ROWS = 8
NA_WIN_COLS = 16
FN_GROUPS = 4
FN_GROUP_DIM = 64
FN_WIDTH = FN_GROUPS * FN_GROUP_DIM
HY_SHORT = 3
HY_EMB = 33
HY_BANDS = (HY_EMB - 1) // 2
HY_MIN_DECAY = math.log(1e-2) / 1.5
HY_MAX_DECAY = math.log(1e-2) / 0.3
N_EXPERTS = 16
EC_CAPACITY_FACTOR = 2
RMS_EPS = 1e-6

LANES = 128
HEAD_PAIRS = NA_WIDTH // LANES
NA_Q_ROWS = 2
NEG_INF = -1e30
LOG2E = math.log2(math.e)
QK_SCALE = HEAD_DIM ** -0.5 * LOG2E
VMEM_LIMIT = 56 * 1024 * 1024

BF16 = jnp.bfloat16
F32 = jnp.float32


def _cparams(*sem):
    return pltpu.CompilerParams(dimension_semantics=sem, vmem_limit_bytes=VMEM_LIMIT)


def _small_matmul_kernel(a_ref, w_ref, b_ref, o_ref):
    o_ref[...] = jnp.dot(a_ref[...], w_ref[...], preferred_element_type=F32,
                         precision=lax.Precision.HIGHEST) + b_ref[...]


def small_matmul(a, w, b, tn=1536):
    M, K = a.shape
    N = w.shape[1]
    return pl.pallas_call(
        _small_matmul_kernel,
        grid=(N // tn,),
        in_specs=[pl.BlockSpec((M, K), lambda j: (0, 0)),
                  pl.BlockSpec((K, tn), lambda j: (0, j)),
                  pl.BlockSpec((1, tn), lambda j: (0, j))],
        out_specs=pl.BlockSpec((M, tn), lambda j: (0, j)),
        out_shape=jax.ShapeDtypeStruct((M, N), F32),
        compiler_params=_cparams("arbitrary"),
        name="small_matmul",
    )(a, w, b.reshape(1, N))


def _norm_mod(x, g, shift, scale):
    ms = jnp.mean(x * x, axis=-1, keepdims=True)
    y = x * lax.rsqrt(ms + RMS_EPS) * g
    return y * (1.0 + scale) + shift


def _norm_proj_kernel(x_ref, g_ref, sh_ref, sc_ref, w_ref, cs_ref, o_ref):
    h = _norm_mod(x_ref[0], g_ref[...], sh_ref[0], sc_ref[0])
    y = jnp.dot(h.astype(BF16), w_ref[...], preferred_element_type=F32)
    o_ref[0] = (y * cs_ref[...]).astype(o_ref.dtype)


def norm_proj(x, g, shift, scale, w, out_dtype, tm, col_scale=None):
    B, L, D = x.shape
    N = w.shape[1]
    tm = min(tm, L)
    if col_scale is None:
        col_scale = jnp.ones((N,), F32)
    return pl.pallas_call(
        _norm_proj_kernel,
        grid=(B, L // tm),
        in_specs=[pl.BlockSpec((1, tm, D), lambda b, i: (b, i, 0)),
                  pl.BlockSpec((1, D), lambda b, i: (0, 0)),
                  pl.BlockSpec((1, 1, D), lambda b, i: (b, 0, 0)),
                  pl.BlockSpec((1, 1, D), lambda b, i: (b, 0, 0)),
                  pl.BlockSpec((D, N), lambda b, i: (0, 0)),
                  pl.BlockSpec((1, N), lambda b, i: (0, 0))],
        out_specs=pl.BlockSpec((1, tm, N), lambda b, i: (b, i, 0)),
        out_shape=jax.ShapeDtypeStruct((B, L, N), out_dtype),
        compiler_params=_cparams("arbitrary", "arbitrary"),
        name="norm_proj",
    )(x, g.reshape(1, D), shift.reshape(B, 1, D), scale.reshape(B, 1, D), w, col_scale.reshape(1, N))


def _norm_router_kernel(x_ref, g_ref, sh_ref, sc_ref, rw_ref, h_ref, aff_ref, *, token_tiles):
    h = _norm_mod(x_ref[0], g_ref[...], sh_ref[0], sc_ref[0])
    if token_tiles:
        n_tok, D = h.shape
        tiles = h_ref.at[0]
        for c in range(D // LANES):
            tiles[pl.ds(c, n_tok, stride=D // LANES), :] = h[:, c * LANES:(c + 1) * LANES]
    else:
        h_ref[0] = h.astype(h_ref.dtype)
    logits = jnp.dot(h, rw_ref[...], preferred_element_type=F32, precision=lax.Precision.HIGHEST)
    m = jnp.max(logits, axis=-1, keepdims=True)
    e = jnp.exp(logits - m)
    aff_ref[0] = e / jnp.sum(e, axis=-1, keepdims=True)


def norm_router(x, g, shift, scale, router_w, tm, token_tiles):
    B, L, D = x.shape
    E = router_w.shape[1]
    tm = min(tm, L)
    if token_tiles:
        sub = D // LANES
        h_spec = pl.BlockSpec((1, tm * sub, LANES), lambda b, i: (b, i, 0))
        h_shape = jax.ShapeDtypeStruct((B, L * sub, LANES), F32)
    else:
        h_spec = pl.BlockSpec((1, tm, D), lambda b, i: (b, i, 0))
        h_shape = jax.ShapeDtypeStruct((B, L, D), BF16)
    return pl.pallas_call(
        functools.partial(_norm_router_kernel, token_tiles=token_tiles),
        grid=(B, L // tm),
        in_specs=[pl.BlockSpec((1, tm, D), lambda b, i: (b, i, 0)),
                  pl.BlockSpec((1, D), lambda b, i: (0, 0)),
                  pl.BlockSpec((1, 1, D), lambda b, i: (b, 0, 0)),
                  pl.BlockSpec((1, 1, D), lambda b, i: (b, 0, 0)),
                  pl.BlockSpec((D, E), lambda b, i: (0, 0))],
        out_specs=[h_spec, pl.BlockSpec((1, tm, E), lambda b, i: (b, i, 0))],
        out_shape=[h_shape, jax.ShapeDtypeStruct((B, L, E), F32)],
        compiler_params=_cparams("arbitrary", "arbitrary"),
        name="norm_router",
    )(x, g.reshape(1, D), shift.reshape(B, 1, D), scale.reshape(B, 1, D), router_w)


def _proj_residual_kernel(y_ref, w_ref, x_ref, gate_ref, o_ref):
    mix = jnp.dot(y_ref[0].astype(BF16), w_ref[...], preferred_element_type=F32)
    o_ref[0] = x_ref[0] + gate_ref[0] * mix


def proj_residual(y, w, x, gate, tm):
    B, L, K = y.shape
    D = w.shape[1]
    tm = min(tm, L)
    return pl.pallas_call(
        _proj_residual_kernel,
        grid=(B, L // tm),
        in_specs=[pl.BlockSpec((1, tm, K), lambda b, i: (b, i, 0)),
                  pl.BlockSpec((K, D), lambda b, i: (0, 0)),
                  pl.BlockSpec((1, tm, D), lambda b, i: (b, i, 0)),
                  pl.BlockSpec((1, 1, D), lambda b, i: (b, 0, 0))],
        out_specs=pl.BlockSpec((1, tm, D), lambda b, i: (b, i, 0)),
        out_shape=jax.ShapeDtypeStruct((B, L, D), F32),
        compiler_params=_cparams("arbitrary", "arbitrary"),
        name="proj_residual",
    )(y, w, x, gate.reshape(B, 1, D))


def _rms_kernel(x_ref, g_ref, o_ref):
    x = x_ref[0]
    ms = jnp.mean(x * x, axis=-1, keepdims=True)
    o_ref[0] = x * lax.rsqrt(ms + RMS_EPS) * g_ref[...]


def rms_norm_final(x, g, tm=1024):
    B, L, D = x.shape
    return pl.pallas_call(
        _rms_kernel,
        grid=(B, L // tm),
        in_specs=[pl.BlockSpec((1, tm, D), lambda b, i: (b, i, 0)),
                  pl.BlockSpec((1, D), lambda b, i: (0, 0))],
        out_specs=pl.BlockSpec((1, tm, D), lambda b, i: (b, i, 0)),
        out_shape=jax.ShapeDtypeStruct((B, L, D), F32),
        compiler_params=_cparams("arbitrary", "arbitrary"),
        name="rms_final",
    )(x, g.reshape(1, D))


def _two_head_rows(q):
    lane = lax.broadcasted_iota(jnp.int32, q.shape, 1)
    zero = jnp.zeros_like(q)
    return jnp.concatenate([jnp.where(lane < HEAD_DIM, q, zero),
                            jnp.where(lane >= HEAD_DIM, q, zero)], axis=0)


def _merge_two_heads(o):
    n = o.shape[0] // 2
    lane = lax.broadcasted_iota(jnp.int32, (n, LANES), 1)
    return jnp.where(lane < HEAD_DIM, o[:n], o[n:])


_CONTRACT_LAST = (((1,), (1,)), ((), ()))


def _na_key_rows(rq):
    n = rq + NA_WIN_ROWS - 1
    return n + n % 2


def _na_kernel(q_ref, k_ref, v_ref, kc_ref, vc_ref, bias_ref, o_ref, *, rows, rq):
    nq = rq * GRID_W
    nkr = _na_key_rows(rq)
    kwin = nkr * GRID_W
    kc = kc_ref[0]
    vc = vc_ref[0]

    def step(i, carry):
        r = i * rq
        start = jnp.clip(r - NA_WIN_ROWS // 2, 0, rows - nkr)
        variant = r - start
        q0 = pl.multiple_of(r * GRID_W, nq)
        k0 = pl.multiple_of(start * GRID_W, GRID_W)
        q2 = _two_head_rows(q_ref[0, pl.ds(q0, nq), :])
        kw = k_ref[0, pl.ds(k0, kwin), :]
        vw = v_ref[0, pl.ds(k0, kwin), :]
        s_loc = lax.dot_general(q2, kw, _CONTRACT_LAST, preferred_element_type=F32) + bias_ref[0, variant]
        s_ctx = lax.dot_general(q2, kc, _CONTRACT_LAST, preferred_element_type=F32)
        m = jnp.maximum(jnp.max(s_loc, axis=-1, keepdims=True), jnp.max(s_ctx, axis=-1, keepdims=True))
        p_loc = jnp.exp2(s_loc - m)
        p_ctx = jnp.exp2(s_ctx - m)
        denom = jnp.sum(p_loc, axis=-1, keepdims=True) + jnp.sum(p_ctx, axis=-1, keepdims=True)
        o = (jnp.dot(p_loc.astype(BF16), vw, preferred_element_type=F32)
             + jnp.dot(p_ctx.astype(BF16), vc, preferred_element_type=F32))
        o = o / denom
        o_ref[0, pl.ds(q0, nq), :] = _merge_two_heads(o).astype(o_ref.dtype)
        return carry

    lax.fori_loop(0, rows // rq, step, 0, unroll=2)


def _na_bias_tables(rpb, rows, rq):
    W = GRID_W
    nkr = _na_key_rows(rq)
    n_var = nkr - rq + 1
    n_ro, n_co = 2 * NA_WIN_ROWS - 1, 2 * NA_WIN_COLS - 1
    cols = np.arange(W)
    col_start = np.clip(cols - NA_WIN_COLS // 2, 0, W - NA_WIN_COLS)
    row_sel = np.zeros((n_var, rq, nkr, n_ro), np.float32)
    for var in range(n_var):
        r = var if var <= NA_WIN_ROWS // 2 else rows - nkr + var
        start = r - var
        for dr in range(rq):
            q_row = r + dr
            if q_row >= rows:
                continue
            rs = int(np.clip(q_row - NA_WIN_ROWS // 2, 0, rows - NA_WIN_ROWS))
            for kr in range(nkr):
                key_row = start + kr
                if rs <= key_row < rs + NA_WIN_ROWS:
                    row_sel[var, dr, kr, key_row - q_row + NA_WIN_ROWS - 1] = 1.0
    col_sel = np.zeros((n_co, W, W), np.float32)
    for q in range(W):
        for kc in range(col_start[q], col_start[q] + NA_WIN_COLS):
            col_sel[kc - q + NA_WIN_COLS - 1, q, kc] = 1.0
    inside = (row_sel.sum(-1)[:, :, :, None, None] * col_sel.sum(0)[None, None, None]) > 0
    hp = lax.Precision.HIGHEST
    by_row = jnp.einsum('vdko,hoc->hvdkc', row_sel, rpb.astype(F32), precision=hp)
    tab = jnp.einsum('hvdkc,cqn->hvdkqn', by_row, col_sel, precision=hp)
    tab = jnp.where(inside[None], tab * LOG2E, NEG_INF)
    tab = tab.reshape(HEAD_PAIRS, 2, n_var, rq, nkr, W, W).transpose(0, 2, 1, 3, 5, 4, 6)
    return tab.reshape(HEAD_PAIRS, n_var, 2 * rq * W, nkr * W)


def na_attention(p, pc, kc_off, rpb):
    B, L, _ = p.shape
    Lc = pc.shape[1]
    rows = L // GRID_W
    rq = NA_Q_ROWS
    bias = _na_bias_tables(rpb, rows, rq)
    nkr = _na_key_rows(rq)
    kern = functools.partial(_na_kernel, rows=rows, rq=rq)
    return pl.pallas_call(
        kern,
        grid=(B, HEAD_PAIRS),
        in_specs=[pl.BlockSpec((1, L, LANES), lambda b, j: (b, 0, j)),
                  pl.BlockSpec((1, L, LANES), lambda b, j: (b, 0, HEAD_PAIRS + j)),
                  pl.BlockSpec((1, L, LANES), lambda b, j: (b, 0, 2 * HEAD_PAIRS + j)),
                  pl.BlockSpec((1, Lc, LANES), lambda b, j: (b, 0, kc_off + j)),
                  pl.BlockSpec((1, Lc, LANES), lambda b, j: (b, 0, kc_off + HEAD_PAIRS + j)),
                  pl.BlockSpec((1, nkr - rq + 1, 2 * rq * GRID_W, nkr * GRID_W), lambda b, j: (j, 0, 0, 0))],
        out_specs=pl.BlockSpec((1, L, LANES), lambda b, j: (b, 0, j)),
        out_shape=jax.ShapeDtypeStruct((B, L, NA_WIDTH), BF16),
        compiler_params=_cparams("arbitrary", "arbitrary"),
        name="na_attention",
    )(p, p, p, pc, pc, bias)


def _ctx_attn_kernel(q_ref, k_ref, v_ref, o_ref):
    q2 = _two_head_rows(q_ref[0])
    s = lax.dot_general(q2, k_ref[0], _CONTRACT_LAST, preferred_element_type=F32)
    m = jnp.max(s, axis=-1, keepdims=True)
    e = jnp.exp2(s - m)
    o = jnp.dot(e.astype(BF16), v_ref[0], preferred_element_type=F32) / jnp.sum(e, axis=-1, keepdims=True)
    o_ref[0] = _merge_two_heads(o).astype(o_ref.dtype)


def ctx_attention(pc):
    B, Lc, _ = pc.shape
    return pl.pallas_call(
        _ctx_attn_kernel,
        grid=(B, HEAD_PAIRS),
        in_specs=[pl.BlockSpec((1, Lc, LANES), lambda b, j: (b, 0, j)),
                  pl.BlockSpec((1, Lc, LANES), lambda b, j: (b, 0, HEAD_PAIRS + j)),
                  pl.BlockSpec((1, Lc, LANES), lambda b, j: (b, 0, 2 * HEAD_PAIRS + j))],
        out_specs=pl.BlockSpec((1, Lc, LANES), lambda b, j: (b, 0, j)),
        out_shape=jax.ShapeDtypeStruct((B, Lc, NA_WIDTH), BF16),
        compiler_params=_cparams("arbitrary", "arbitrary"),
        name="ctx_attention",
    )(pc, pc, pc)


def _expert_ffn_kernel(x_ref, gate_ref, wg_ref, wu_ref, wd_ref, o_ref, *, fchunk):
    o_ref[0, 0] = _swiglu(x_ref[0, 0], wg_ref, wu_ref, wd_ref, fchunk) * gate_ref[0, 0]


def expert_ffn(xe, gate, wg, wu, wd, fchunk=512):
    E, B, cap, D = xe.shape
    F = wg.shape[2]
    kern = functools.partial(_expert_ffn_kernel, fchunk=fchunk)
    return pl.pallas_call(
        kern,
        grid=(E, B),
        in_specs=[pl.BlockSpec((1, 1, cap, D), lambda e, b: (e, b, 0, 0)),
                  pl.BlockSpec((1, 1, cap, 1), lambda e, b: (e, b, 0, 0)),
                  pl.BlockSpec((1, D, F), lambda e, b: (e, 0, 0)),
                  pl.BlockSpec((1, D, F), lambda e, b: (e, 0, 0)),
                  pl.BlockSpec((1, F, D), lambda e, b: (e, 0, 0))],
        out_specs=pl.BlockSpec((1, 1, cap, D), lambda e, b: (e, b, 0, 0)),
        out_shape=jax.ShapeDtypeStruct((E, B, cap, D), F32),
        compiler_params=_cparams("arbitrary", "arbitrary"),
        name="expert_ffn",
    )(xe, gate, wg, wu, wd)


def _swiglu(x, wg_ref, wu_ref, wd_ref, fchunk):
    F = wg_ref.shape[2]
    acc = jnp.zeros((x.shape[0], wd_ref.shape[2]), F32)
    for f0 in range(0, F, fchunk):
        g = jnp.dot(x, wg_ref[0, :, f0:f0 + fchunk], preferred_element_type=F32)
        u = jnp.dot(x, wu_ref[0, :, f0:f0 + fchunk], preferred_element_type=F32)
        hid = (g * jax.nn.sigmoid(g) * u).astype(BF16)
        acc = acc + jnp.dot(hid, wd_ref[0, f0:f0 + fchunk, :], preferred_element_type=F32)
    return acc


def _gather_rows_copy(h_hbm, xbuf, sem, slot, row, j, sub):
    src = h_hbm.at[pl.ds(pl.multiple_of(row, sub), sub)]
    dst = xbuf.at[slot, pl.ds(pl.multiple_of(j * sub, sub), sub)]
    return pltpu.make_async_copy(src, dst, sem.at[slot])


def _expert_ffn_gather_kernel(rows_ref, rows_next_ref, gate_ref, h_hbm, wg_ref, wu_ref, wd_ref, o_ref,
                              xbuf, xb_ref, acc_ref, sem, *, fchunk):
    sub = xb_ref.shape[1] // LANES
    cap = xbuf.shape[1] // sub
    step = pl.program_id(0) * pl.num_programs(1) + pl.program_id(1)
    nsteps = pl.num_programs(0) * pl.num_programs(1)
    slot = step % 2

    def drain(s):
        pltpu.make_async_copy(h_hbm.at[pl.ds(0, cap * sub)], xbuf.at[s], sem.at[s]).wait()

    @pl.when(step == 0)
    def _():
        def body(j, carry):
            _gather_rows_copy(h_hbm, xbuf, sem, 0, rows_ref[0, 0, j], j, sub).start()
            return carry
        lax.fori_loop(0, cap, body, 0)

    drain(slot)
    for c in range(sub):
        xb_ref[:, c * LANES:(c + 1) * LANES] = xbuf[slot, pl.ds(c, cap, stride=sub), :].astype(BF16)
    acc_ref[...] = jnp.zeros_like(acc_ref)
    n_chunks = wg_ref.shape[2] // fchunk
    rows_per_chunk = cap // n_chunks

    def chunk(k, carry):
        for jj in range(rows_per_chunk):
            j = k * rows_per_chunk + jj
            _gather_rows_copy(h_hbm, xbuf, sem, 1 - slot, rows_next_ref[0, 0, j], j, sub).start()
        f0 = pl.multiple_of(k * fchunk, fchunk)
        x = xb_ref[...]
        g = jnp.dot(x, wg_ref[0, :, pl.ds(f0, fchunk)], preferred_element_type=F32)
        u = jnp.dot(x, wu_ref[0, :, pl.ds(f0, fchunk)], preferred_element_type=F32)
        hid = (g * jax.nn.sigmoid(g) * u).astype(BF16)
        acc_ref[...] += jnp.dot(hid, wd_ref[0, pl.ds(f0, fchunk), :], preferred_element_type=F32)
        return carry
    lax.fori_loop(0, n_chunks, chunk, 0)
    o_ref[0, 0] = (acc_ref[...] * gate_ref[0, 0]).astype(o_ref.dtype)

    @pl.when(step == nsteps - 1)
    def _():
        drain(1 - slot)


def expert_ffn_gather(h_rows, rows, gate, wg, wu, wd, fchunk=512):
    E, B, cap = rows.shape
    D = wg.shape[1]
    sub = D // LANES
    rows = rows * sub
    F = wg.shape[2]

    rows = rows.reshape(E * B, 1, cap)
    fchunk = min(fchunk, F)

    def next_group(e, b):
        return (jnp.minimum(e * B + b + 1, E * B - 1), 0, 0)

    kern = functools.partial(_expert_ffn_gather_kernel, fchunk=fchunk)
    return pl.pallas_call(
        kern,
        grid=(E, B),
        in_specs=[pl.BlockSpec((1, 1, cap), lambda e, b: (e * B + b, 0, 0), memory_space=pltpu.SMEM),
                  pl.BlockSpec((1, 1, cap), next_group, memory_space=pltpu.SMEM),
                  pl.BlockSpec((1, 1, cap, 1), lambda e, b: (e, b, 0, 0)),
                  pl.BlockSpec(memory_space=pl.ANY),
                  pl.BlockSpec((1, D, F), lambda e, b: (e, 0, 0)),
                  pl.BlockSpec((1, D, F), lambda e, b: (e, 0, 0)),
                  pl.BlockSpec((1, F, D), lambda e, b: (e, 0, 0))],
        out_specs=pl.BlockSpec((1, 1, cap, D), lambda e, b: (e, b, 0, 0)),
        out_shape=jax.ShapeDtypeStruct((E, B, cap, D), BF16),
        scratch_shapes=[pltpu.VMEM((2, cap * sub, LANES), F32), pltpu.VMEM((cap, D), BF16), pltpu.VMEM((cap, D), F32),
                        pltpu.SemaphoreType.DMA((2,))],
        compiler_params=_cparams("arbitrary", "arbitrary"),
        name="expert_ffn_gather",
    )(rows, rows, gate, h_rows, wg, wu, wd)


COMBINE_TOKENS = 512
COMBINE_WINDOW = 256
COMBINE_CHUNK = LANES


def _window_copy(ye_hbm, win, sem, e, b, w0, slot):
    return pltpu.make_async_copy(ye_hbm.at[e, b, pl.ds(w0, COMBINE_WINDOW)], win.at[slot], sem.at[slot])


def _moe_combine_kernel(starts_ref, tok_ref, x_ref, gate_ref, ye_hbm, o_ref, acc_ref, win, sem):
    b = pl.program_id(0)
    i = pl.program_id(1)
    n_exp = tok_ref.shape[0]
    cap = ye_hbm.shape[2]
    T = x_ref.shape[1]
    tile0 = i * T
    tok_iota = lax.broadcasted_iota(jnp.int32, (T, COMBINE_CHUNK), 0) + tile0

    nt = pl.num_programs(1)
    step = b * nt + i
    parity = step % 2

    def first_window(e, bb=b, ii=i):
        s = starts_ref[bb, e, ii]
        return jnp.minimum((s // COMBINE_CHUNK) * COMBINE_CHUNK, cap - COMBINE_WINDOW)

    def fetch_first_windows(bb, ii, par):
        for e in range(n_exp):
            _window_copy(ye_hbm, win, sem, e, bb, first_window(e, bb, ii), par * n_exp + e).start()

    def one_hot(e, w0, first_row=None):
        k = w0 // COMBINE_CHUNK
        hot = []
        for c in range(COMBINE_WINDOW // COMBINE_CHUNK):
            match = tok_iota == tok_ref[e, 0, pl.ds(k + c, 1), :]
            if first_row is not None:
                row = lax.broadcasted_iota(jnp.int32, match.shape, 1) + (w0 + c * COMBINE_CHUNK)
                match = match & (row >= first_row)
            hot.append(match.astype(BF16))
        return hot

    @pl.when(step == 0)
    def _():
        fetch_first_windows(b, i, 0)

    @pl.when(step + 1 < pl.num_programs(0) * nt)
    def _():
        fetch_first_windows((step + 1) // nt, (step + 1) % nt, 1 - parity)

    firsts = [first_window(e) for e in range(n_exp)]
    hot = []
    for e in range(n_exp):
        hot += one_hot(e, firsts[e])
    for e in range(n_exp):
        _window_copy(ye_hbm, win, sem, e, b, firsts[e], parity * n_exp + e).wait()
    rows_all = win[pl.ds(parity * n_exp, n_exp)].reshape(n_exp * COMBINE_WINDOW, win.shape[2])
    acc_ref[...] = jnp.dot(jnp.concatenate(hot, axis=1), rows_all, preferred_element_type=F32)

    def per_expert(e, carry):
        w0 = first_window(e)
        end = starts_ref[b, e, i + 1]
        n_more = jnp.maximum(end - (w0 + COMBINE_WINDOW) + COMBINE_WINDOW - 1, 0) // COMBINE_WINDOW

        def more(m, c):
            first_row = w0 + (m + 1) * COMBINE_WINDOW
            w = jnp.minimum(first_row, cap - COMBINE_WINDOW)
            cp = _window_copy(ye_hbm, win, sem, e, b, w, 2 * n_exp)
            cp.start()
            cp.wait()
            sel = jnp.concatenate(one_hot(e, w, first_row), axis=1)
            acc_ref[...] += jnp.dot(sel, win[2 * n_exp], preferred_element_type=F32)
            return c
        lax.fori_loop(0, n_more, more, 0)
        return carry

    lax.fori_loop(0, n_exp, per_expert, 0)
    o_ref[0] = x_ref[0] + gate_ref[0] * acc_ref[...]


def moe_combine(x, gate_res, ye, tok):
    B, n, D = x.shape
    E, _, cap, _ = ye.shape
    T = COMBINE_TOKENS
    nt = n // T
    bounds = jnp.arange(nt + 1, dtype=jnp.int32) * T
    starts = jnp.sum(tok[:, :, None, :] < bounds[None, None, :, None], axis=-1, dtype=jnp.int32)
    starts = jnp.transpose(starts, (1, 0, 2))
    tok4 = tok.reshape(E, B, cap // COMBINE_CHUNK, COMBINE_CHUNK)
    grid_spec = pltpu.PrefetchScalarGridSpec(
        num_scalar_prefetch=1,
        grid=(B, nt),
        in_specs=[pl.BlockSpec((E, 1, cap // COMBINE_CHUNK, COMBINE_CHUNK), lambda b, i, s: (0, b, 0, 0)),
                  pl.BlockSpec((1, T, D), lambda b, i, s: (b, i, 0)),
                  pl.BlockSpec((1, 1, D), lambda b, i, s: (b, 0, 0)),
                  pl.BlockSpec(memory_space=pl.ANY)],
        out_specs=pl.BlockSpec((1, T, D), lambda b, i, s: (b, i, 0)),
        scratch_shapes=[pltpu.VMEM((T, D), F32), pltpu.VMEM((2 * E + 1, COMBINE_WINDOW, D), BF16),
                        pltpu.SemaphoreType.DMA((2 * E + 1,))])
    return pl.pallas_call(
        _moe_combine_kernel,
        grid_spec=grid_spec,
        out_shape=jax.ShapeDtypeStruct((B, n, D), F32),
        compiler_params=_cparams("arbitrary", "arbitrary"),
        name="moe_combine",
    )(starts, tok4, x, gate_res.reshape(B, 1, D), ye)


def expert_choice_ffn(x, g, shift, scale, gate_res, router_w, wg, wu, wd, tm):
    B, n, D = x.shape
    E = N_EXPERTS
    cap = EC_CAPACITY_FACTOR * n // E
    fused = cap % COMBINE_WINDOW == 0 and cap > COMBINE_WINDOW and n % COMBINE_TOKENS == 0
    h, aff = norm_router(x, g, shift, scale, router_w, tm, fused)
    gate, idx = lax.top_k(jnp.transpose(aff, (2, 0, 1)), cap)
    bidx = jnp.arange(B, dtype=idx.dtype)[None, :, None]
    if fused:
        idx, gate = lax.sort((idx, gate), dimension=2, num_keys=1)
        ye = expert_ffn_gather(h.reshape(-1, LANES), idx + bidx * n, gate[..., None], wg, wu, wd)
        return moe_combine(x, gate_res, ye, idx)
    xe = h[bidx, idx].reshape(E, 1, B * cap, D)
    ye = expert_ffn(xe, gate.reshape(E, 1, B * cap, 1), wg, wu, wd)
    flat = (idx + bidx * n).reshape(-1)
    moe = jnp.zeros((B * n, D), F32).at[flat].add(ye.reshape(-1, D)).reshape(B, n, D)
    return x + gate_res[:, None, :] * moe


DFT_MINOR = 128
DFT_PITCH_PAD = 8
DFT_QB = 32
DFT_UNROLL = 8


def _dft_mats(n_major, a_used):
    N = n_major * DFT_MINOR
    q = np.arange(n_major)
    a = np.arange(a_used)
    b = np.arange(DFT_MINOR)
    ang1 = -2 * np.pi * np.outer(q, a) / n_major
    f1 = np.concatenate([np.cos(ang1), np.sin(ang1)], 0)
    ang2 = -2 * np.pi * np.outer(b, b) / DFT_MINOR
    c2, s2 = np.cos(ang2), np.sin(ang2)
    f2 = np.block([[c2, -s2], [s2, c2]])
    f2c = np.block([[c2, s2], [-s2, c2]])
    lane1 = np.ones((1, 1, LANES))

    def tw(hi, lo):
        ang = -2 * np.pi * np.outer(hi, lo) / N
        return (np.cos(ang)[:, :, None] * lane1, np.sin(ang)[:, :, None] * lane1)

    t1r, t1i = tw(8 * np.arange(DFT_MINOR // 8), q)
    t0r, t0i = tw(np.arange(8), q)
    u1r, u1i = tw(8 * np.arange(n_major // 8), b)
    u0r, u0i = tw(np.arange(8), b)
    f32 = lambda *xs: [np.asarray(x, np.float32) for x in xs]
    return dict(f1=f1, f2=f2, f2c=f2c, tw_fwd=f32(t1r, t1i, t0r, t0i), tw_inv=f32(u1r, u1i, u0r, u0i))


def _cmul(ar, ai, br, bi):
    return ar * br - ai * bi, ar * bi + ai * br


def _dft_fwd_kernel(*refs, n_major, a_used, mode, precision):
    x_ref, f1_ref, f2_ref, t1r, t1i, t0r, t0i = refs[:7]
    rest = refs[7:]
    if mode == "mul":
        hr_ref, hi_ref, or_ref, oi_ref, sr, si = rest
    elif mode == "chanmix":
        mc_ref, ms_ref, or_ref, sr, si = rest
    else:
        or_ref, oi_ref, sr, si = rest
    pitch = n_major + DFT_PITCH_PAD
    qb = pl.program_id(2)
    mm_dtype = f1_ref.dtype

    @pl.when(qb == 0)
    def _stage1():
        def slab(b, carry):
            y = jnp.dot(f1_ref[...], x_ref[0, b].astype(mm_dtype), preferred_element_type=F32,
                        precision=precision)
            twr, twi = _cmul(t1r[b // 8], t1i[b // 8], t0r[b % 8], t0i[b % 8])
            yr, yi = _cmul(y[:n_major], y[n_major:], twr, twi)
            row = pl.multiple_of(b * pitch, 8)
            sr[pl.ds(row, n_major), :] = yr
            si[pl.ds(row, n_major), :] = yi
            return carry
        lax.fori_loop(0, DFT_MINOR, slab, 0, unroll=DFT_UNROLL)

    def freq(j, carry):
        q = qb * DFT_QB + j
        g = jnp.concatenate([sr[pl.ds(q, DFT_MINOR, stride=pitch), :],
                             si[pl.ds(q, DFT_MINOR, stride=pitch), :]], axis=0).astype(mm_dtype)
        xf = jnp.dot(f2_ref[...], g, preferred_element_type=F32, precision=precision)
        xr, xi = xf[:DFT_MINOR], xf[DFT_MINOR:]
        if mode == "mul":
            xr, xi = _cmul(xr, xi, hr_ref[0, j].astype(F32), hi_ref[0, j].astype(F32))
            or_ref[0, j] = xr.astype(or_ref.dtype)
            oi_ref[0, j] = xi.astype(oi_ref.dtype)
        elif mode == "chanmix":
            or_ref[0, j] = (jnp.dot(xr.astype(mm_dtype), mc_ref[...], preferred_element_type=F32)
                            + jnp.dot(xi.astype(mm_dtype), ms_ref[...], preferred_element_type=F32))
        else:
            or_ref[0, j] = xr
            oi_ref[0, j] = xi
        return carry
    lax.fori_loop(0, DFT_QB, freq, 0, unroll=DFT_UNROLL)


def dft_forward(xp, n_major, mode="spectrum", h=None, chan=None, precise=False):
    Bb, _, a_used, C = xp.shape
    mats = _dft_mats(n_major, a_used)
    mm_dtype = F32 if precise else BF16
    precision = lax.Precision.HIGHEST if precise else None
    pitch = n_major + DFT_PITCH_PAD
    const = lambda shape: pl.BlockSpec(shape, lambda bb, cb, qb: (0,) * len(shape))
    spec_blk = pl.BlockSpec((1, DFT_QB, DFT_MINOR, LANES), lambda bb, cb, qb: (bb, qb, 0, cb))
    args = [xp, jnp.asarray(mats["f1"], mm_dtype), jnp.asarray(mats["f2"], mm_dtype)] + mats["tw_fwd"]
    in_specs = [pl.BlockSpec((1, DFT_MINOR, a_used, LANES), lambda bb, cb, qb: (bb, 0, 0, cb)),
                const((2 * n_major, a_used)), const((2 * DFT_MINOR, 2 * DFT_MINOR)),
                const((DFT_MINOR // 8, n_major, LANES)), const((DFT_MINOR // 8, n_major, LANES)),
                const((8, n_major, LANES)), const((8, n_major, LANES))]
    spec_shape = (Bb, n_major, DFT_MINOR, C)
    if mode == "mul":
        args += [h[0], h[1]]
        hblk = pl.BlockSpec((1, DFT_QB, DFT_MINOR, LANES), lambda bb, cb, qb: (0, qb, 0, cb))
        in_specs += [hblk, hblk]
        out_specs = [spec_blk, spec_blk]
        out_shape = [jax.ShapeDtypeStruct(spec_shape, BF16)] * 2
    elif mode == "chanmix":
        args += [chan[0].astype(mm_dtype), chan[1].astype(mm_dtype)]
        in_specs += [const((LANES, LANES)), const((LANES, LANES))]
        out_specs = spec_blk
        out_shape = jax.ShapeDtypeStruct(spec_shape, F32)
    else:
        out_specs = [spec_blk, spec_blk]
        out_shape = [jax.ShapeDtypeStruct(spec_shape, F32)] * 2
    kern = functools.partial(_dft_fwd_kernel, n_major=n_major, a_used=a_used, mode=mode, precision=precision)
    return pl.pallas_call(
        kern,
        grid=(Bb, C // LANES, n_major // DFT_QB),
        in_specs=in_specs,
        out_specs=out_specs,
        out_shape=out_shape,
        scratch_shapes=[pltpu.VMEM((DFT_MINOR * pitch, LANES), F32)] * 2,
        compiler_params=_cparams("arbitrary", "arbitrary", "arbitrary"),
        name="dft_forward_" + mode,
    )(*args)


def _dft_inv_kernel(gr_ref, gi_ref, f2c_ref, f1c_ref, u1r, u1i, u0r, u0i, o_ref, sr, si, *, n_major, a_out):
    pitch = DFT_MINOR + DFT_PITCH_PAD
    qb = pl.program_id(2)

    def freq(j, carry):
        q = qb * DFT_QB + j
        g = jnp.concatenate([gr_ref[0, j], gi_ref[0, j]], axis=0)
        t = jnp.dot(f2c_ref[...], g, preferred_element_type=F32)
        twr, twi = _cmul(u1r[q // 8], u1i[q // 8], u0r[q % 8], u0i[q % 8])
        tr, ti = _cmul(t[:DFT_MINOR], t[DFT_MINOR:], twr, -twi)
        row = pl.multiple_of(q * pitch, 8)
        sr[pl.ds(row, DFT_MINOR), :] = tr
        si[pl.ds(row, DFT_MINOR), :] = ti
        return carry
    lax.fori_loop(0, DFT_QB, freq, 0, unroll=DFT_UNROLL)

    @pl.when(qb == pl.num_programs(2) - 1)
    def _stage2():
        def slab(b, carry):
            g = jnp.concatenate([sr[pl.ds(b, n_major, stride=pitch), :],
                                 si[pl.ds(b, n_major, stride=pitch), :]], axis=0).astype(BF16)
            o_ref[0, b] = jnp.dot(f1c_ref[...], g, preferred_element_type=F32)
            return carry
        lax.fori_loop(0, DFT_MINOR, slab, 0, unroll=DFT_UNROLL)


def dft_inverse_real(gr, gi, a_out):
    Bb, n_major, _, C = gr.shape
    N = n_major * DFT_MINOR
    mats = _dft_mats(n_major, a_out)
    ang = 2 * np.pi * np.outer(np.arange(a_out), np.arange(n_major)) / n_major
    f1c = np.concatenate([np.cos(ang), -np.sin(ang)], 1) / N
    pitch = DFT_MINOR + DFT_PITCH_PAD
    const = lambda shape: pl.BlockSpec(shape, lambda bb, cb, qb: (0,) * len(shape))
    gblk = pl.BlockSpec((1, DFT_QB, DFT_MINOR, LANES), lambda bb, cb, qb: (bb, qb, 0, cb))
    kern = functools.partial(_dft_inv_kernel, n_major=n_major, a_out=a_out)
    return pl.pallas_call(
        kern,
        grid=(Bb, C // LANES, n_major // DFT_QB),
        in_specs=[gblk, gblk, const((2 * DFT_MINOR, 2 * DFT_MINOR)), const((a_out, 2 * n_major)),
                  const((n_major // 8, DFT_MINOR, LANES)), const((n_major // 8, DFT_MINOR, LANES)),
                  const((8, DFT_MINOR, LANES)), const((8, DFT_MINOR, LANES))],
        out_specs=pl.BlockSpec((1, DFT_MINOR, a_out, LANES), lambda bb, cb, qb: (bb, 0, 0, cb)),
        out_shape=jax.ShapeDtypeStruct((Bb, DFT_MINOR, a_out, C), F32),
        scratch_shapes=[pltpu.VMEM((n_major * pitch, LANES), F32)] * 2,
        compiler_params=_cparams("arbitrary", "arbitrary", "arbitrary"),
        name="dft_inverse",
    )(gr, gi, jnp.asarray(mats["f2c"], BF16), jnp.asarray(f1c, BF16), *mats["tw_inv"])


def _to_slabs(x, a_used):
    Bb, _, C = x.shape
    return x.reshape(Bb, a_used, DFT_MINOR, C).transpose(0, 2, 1, 3)


def _from_slabs(xs):
    Bb, m, a, C = xs.shape
    return xs.transpose(0, 2, 1, 3).reshape(Bb, a * m, C)


def _channel_dft_mats(L):
    c = np.arange(FN_GROUP_DIM)
    ang = -2 * np.pi * np.outer(c, c) / FN_GROUP_DIM
    eye = np.eye(LANES // FN_GROUP_DIM)
    norm = 1.0 / math.sqrt(L * FN_GROUP_DIM)
    return (jnp.asarray(np.kron(eye, np.cos(ang)) * norm, F32),
            jnp.asarray(np.kron(eye, -np.sin(ang)) * norm, F32))


def fourier_mix_pallas(u):
    B, L, C = u.shape
    n_major = L // DFT_MINOR
    y = dft_forward(_to_slabs(u, n_major), n_major, mode="chanmix", chan=_channel_dft_mats(L))
    return y.transpose(0, 2, 1, 3).reshape(B, L, C)


def long_conv_pallas(vx, h_fwd, h_bwd):
    B, L, D = vx.shape
    n_major = 2 * L // DFT_MINOR
    taps = _to_slabs(jnp.stack([h_fwd, h_bwd]), n_major // 2)
    tr, ti = dft_forward(taps, n_major, mode="spectrum")
    h = ((tr[0:1] + tr[1:2]).astype(BF16), (ti[0:1] - ti[1:2]).astype(BF16))
    gr, gi = dft_forward(_to_slabs(vx, n_major // 2), n_major, mode="mul", h=h)
    return _from_slabs(dft_inverse_real(gr, gi, n_major // 2))


def fourier_mix(u):
    B, L, _ = u.shape
    if L % (DFT_MINOR * DFT_QB) == 0:
        return fourier_mix_pallas(u.astype(F32))
    ug = u.astype(F32).reshape(B, L, FN_GROUPS, FN_GROUP_DIM)
    y = jnp.fft.fft2(ug, axes=(1, 3), norm='ortho').real
    return y.reshape(B, L, FN_WIDTH)


def hyena_filters(L, w1, b1, freq, w2, b2, w3):
    hp = lax.Precision.HIGHEST
    t = jnp.linspace(0.0, 1.0, L, dtype=F32)[:, None]
    w = 2 * math.pi * jnp.arange(L, dtype=F32)[:, None] / L
    bands = jnp.linspace(1e-4, HY_BANDS - 1, HY_BANDS, dtype=F32)[None, :]
    z = jnp.concatenate([t, jnp.cos(bands * w), -jnp.sin(bands * w)], axis=-1)
    hid = jnp.sin(freq * (jnp.dot(z, w1, precision=hp) + b1))
    hid = jnp.sin(freq * (jnp.dot(hid, w2, precision=hp) + b2))
    h = jnp.dot(hid, w3, precision=hp).reshape(L, 2, D_MODEL)
    deltas = jnp.abs(jnp.linspace(HY_MIN_DECAY, HY_MAX_DECAY, D_MODEL, dtype=F32))
    window = jnp.exp(-t * deltas[None, :])
    h = h * window[:, None, :]
    h_fwd = h[:, 0]
    h_bwd = h[:, 1] * (jnp.arange(L) > 0)[:, None]
    norm = jnp.sum(jnp.abs(h_fwd), axis=0, keepdims=True) + jnp.sum(jnp.abs(h_bwd), axis=0, keepdims=True)
    return h_fwd / norm, h_bwd / norm


HALO_ROWS = 8


def _hyena_pre_kernel(u_ref, prev_ref, next_ref, w_ref, b_ref, x0_ref, vx_ref):
    i = pl.program_id(1)
    u = u_ref[0]
    T = u.shape[0]
    D = x0_ref.shape[2]
    row = lax.broadcasted_iota(jnp.int32, (T, 1), 0)
    before = jnp.where(i > 0, prev_ref[0, HALO_ROWS - 1:HALO_ROWS, :], 0.0)
    after = jnp.where(i < pl.num_programs(1) - 1, next_ref[0, 0:1, :], 0.0)
    u_m1 = jnp.where(row == 0, before, pltpu.roll(u, 1, axis=0))
    u_p1 = jnp.where(row == T - 1, after, pltpu.roll(u, T - 1, axis=0))
    y = u_m1 * w_ref[0:1, :] + u * w_ref[1:2, :] + u_p1 * w_ref[2:3, :] + b_ref[...]
    x0_ref[0] = y[:, :D]
    vx_ref[0] = y[:, 2 * D:] * y[:, D:2 * D]


def hyena_pre(u, conv_w, conv_b, tm=256):
    B, L, D3 = u.shape
    D = D3 // 3
    tm = min(tm, L)
    hb = tm // HALO_ROWS
    last = L // HALO_ROWS - 1
    return pl.pallas_call(
        _hyena_pre_kernel,
        grid=(B, L // tm),
        in_specs=[pl.BlockSpec((1, tm, D3), lambda b, i: (b, i, 0)),
                  pl.BlockSpec((1, HALO_ROWS, D3), lambda b, i: (b, jnp.maximum(i * hb - 1, 0), 0)),
                  pl.BlockSpec((1, HALO_ROWS, D3), lambda b, i: (b, jnp.minimum((i + 1) * hb, last), 0)),
                  pl.BlockSpec((HY_SHORT, D3), lambda b, i: (0, 0)),
                  pl.BlockSpec((1, D3), lambda b, i: (0, 0))],
        out_specs=[pl.BlockSpec((1, tm, D), lambda b, i: (b, i, 0)),
                   pl.BlockSpec((1, tm, D), lambda b, i: (b, i, 0))],
        out_shape=[jax.ShapeDtypeStruct((B, L, D), F32)] * 2,
        compiler_params=_cparams("arbitrary", "arbitrary"),
        name="hyena_pre",
    )(u, u, u, conv_w, conv_b.reshape(1, D3))


def _hyena_out_kernel(y_ref, vx_ref, x0_ref, skip_ref, w_ref, x_ref, gate_ref, o_ref):
    z = x0_ref[0] * (y_ref[0] + vx_ref[0] * skip_ref[...])
    mix = jnp.dot(z.astype(BF16), w_ref[...], preferred_element_type=F32)
    o_ref[0] = x_ref[0] + gate_ref[0] * mix


def hyena_out_residual(y, vx, x0, skip, w, x, gate, tm):
    B, L, D = x.shape
    tm = min(tm, L)
    tile = pl.BlockSpec((1, tm, D), lambda b, i: (b, i, 0))
    return pl.pallas_call(
        _hyena_out_kernel,
        grid=(B, L // tm),
        in_specs=[tile, tile, tile,
                  pl.BlockSpec((1, D), lambda b, i: (0, 0)),
                  pl.BlockSpec((D, D), lambda b, i: (0, 0)),
                  tile,
                  pl.BlockSpec((1, 1, D), lambda b, i: (b, 0, 0))],
        out_specs=tile,
        out_shape=jax.ShapeDtypeStruct((B, L, D), F32),
        compiler_params=_cparams("arbitrary", "arbitrary"),
        name="hyena_out_residual",
    )(y, vx, x0, skip.reshape(1, D), w, x, gate.reshape(B, 1, D))


def hyena_mixer_residual(x, u, conv_w, conv_b, taps, skip, w_out, gate, tm):
    L = u.shape[1]
    x0, vx = hyena_pre(u, conv_w, conv_b)
    h_fwd, h_bwd = taps
    if (2 * L) % (DFT_MINOR * DFT_QB) == 0:
        y = long_conv_pallas(vx, h_fwd, h_bwd)
    else:
        filt = jnp.concatenate([h_fwd, jnp.zeros((1, h_fwd.shape[1]), F32), h_bwd[1:][::-1]], axis=0)
        y = jnp.fft.irfft(jnp.fft.rfft(vx, n=2 * L, axis=1) * jnp.fft.rfft(filt, n=2 * L, axis=0)[None],
                          n=2 * L, axis=1)[:, :L]
    return hyena_out_residual(y, vx, x0, skip, w_out, x, gate, tm)


def kernel(x, c, ctx, c_ctx, mod_w, mod_b, norm_mix_g, norm_ffn_g, mix_w_in, na_rpb, mix_w_out,
           hy_w_in, hy_conv_w, hy_conv_b, hy_f_w1, hy_f_b1, hy_f_freq, hy_f_w2, hy_f_b2, hy_f_w3,
           hy_skip, hy_w_out, router_w, exp_w_gate, exp_w_up, exp_w_down, final_norm_g):
    depth = mod_w.shape[0]
    B, L, D = x.shape
    Lc = ctx.shape[1]
    last_ctx_read = depth - 1 if (depth - 1) % 2 == 0 else depth - 2
    TM = 512

    cs = ctx
    cond = jnp.concatenate([jax.nn.silu(c), jax.nn.silu(c_ctx)[None], jnp.zeros((8 - (B + 1) % 8, D), F32)], 0)

    for layer in range(depth):
        j = layer // 2
        m_all = small_matmul(cond, mod_w[layer], mod_b[layer])
        sh1, sc1, g1, sh2, sc2, g2 = jnp.split(m_all[:B], 6, axis=-1)
        m_ctx = jnp.broadcast_to(m_all[B:B + 1], (B, 6 * D))
        csh1, csc1, cg1, csh2, csc2, cg2 = jnp.split(m_ctx, 6, axis=-1)
        upd = layer < last_ctx_read
        wg = exp_w_gate[layer].astype(BF16)
        wu = exp_w_up[layer].astype(BF16)
        wd = exp_w_down[layer].astype(BF16)

        if layer % 2 == 0:
            w_in = mix_w_in[j].astype(BF16)
            w_out = mix_w_out[j].astype(BF16)
            q_scale = jnp.concatenate([jnp.full((NA_WIDTH,), QK_SCALE, F32),
                                       jnp.ones((w_in.shape[1] - NA_WIDTH,), F32)])
            if upd:
                pc = norm_proj(cs, norm_mix_g[layer], csh1, csc1, w_in, BF16, TM, q_scale)
                kc_off = HEAD_PAIRS
                cmix_in = jnp.concatenate([ctx_attention(pc), fourier_mix(pc[..., 3 * NA_WIDTH:]).astype(BF16)], -1)
            else:
                pc = norm_proj(cs, norm_mix_g[layer], csh1, csc1, w_in[:, NA_WIDTH:3 * NA_WIDTH], BF16, TM)
                kc_off = 0
            p = norm_proj(x, norm_mix_g[layer], sh1, sc1, w_in, BF16, TM, q_scale)
            att = na_attention(p, pc, kc_off, na_rpb[j])
            mix_in = jnp.concatenate([att, fourier_mix(p[..., 3 * NA_WIDTH:]).astype(BF16)], -1)
            x = proj_residual(mix_in, w_out, x, g1, TM)
            if upd:
                cs = proj_residual(cmix_in, w_out, cs, cg1, TM)
        else:
            w_in = hy_w_in[j].astype(BF16)
            w_out = hy_w_out[j].astype(BF16)
            fargs = (hy_f_w1[j], hy_f_b1[j], hy_f_freq[j], hy_f_w2[j], hy_f_b2[j], hy_f_w3[j])
            u = norm_proj(x, norm_mix_g[layer], sh1, sc1, w_in, F32, TM)
            x = hyena_mixer_residual(x, u, hy_conv_w[j], hy_conv_b[j], hyena_filters(L, *fargs), hy_skip[j],
                                     w_out, g1, TM)
            if upd:
                uc = norm_proj(cs, norm_mix_g[layer], csh1, csc1, w_in, F32, TM)
                cs = hyena_mixer_residual(cs, uc, hy_conv_w[j], hy_conv_b[j], hyena_filters(Lc, *fargs),
                                          hy_skip[j], w_out, cg1, TM)

        x = expert_choice_ffn(x, norm_ffn_g[layer], sh2, sc2, g2, router_w[layer], wg, wu, wd, TM)
        if upd:
            cs = expert_choice_ffn(cs, norm_ffn_g[layer], csh2, csc2, cg2, router_w[layer], wg, wu, wd, TM)

    return rms_norm_final(x, final_norm_g)
```

```python
import functools
import math

import numpy as np
import jax
import jax.numpy as jnp
from jax import lax
from jax.experimental import pallas as pl
from jax.experimental.pallas import tpu as pltpu

D_MODEL = 1024
GRID_W = 64
HEAD_DIM = 64
NA_HEADS = 12
NA_WIDTH = NA_HEADS * HEAD_DIM
NA_WIN_ROWS = 8
NA_WIN_COLS = 16
FN_GROUPS = 4
FN_GROUP_DIM = 64
FN_WIDTH = FN_GROUPS * FN_GROUP_DIM
HY_SHORT = 3
HY_EMB = 33
HY_BANDS = (HY_EMB - 1) // 2
HY_MIN_DECAY = math.log(1e-2) / 1.5
HY_MAX_DECAY = math.log(1e-2) / 0.3
N_EXPERTS = 16
EC_CAPACITY_FACTOR = 2
RMS_EPS = 1e-6

LANES = 128
HEAD_PAIRS = NA_WIDTH // LANES
NA_Q_ROWS = 2
NEG_INF = -1e30
LOG2E = math.log2(math.e)
QK_SCALE = HEAD_DIM ** -0.5 * LOG2E
VMEM_LIMIT = 56 * 1024 * 1024

BF16 = jnp.bfloat16
F32 = jnp.float32


def _cparams(*sem):
    return pltpu.CompilerParams(dimension_semantics=sem, vmem_limit_bytes=VMEM_LIMIT)


def _small_matmul_kernel(a_ref, w_ref, b_ref, o_ref):
    o_ref[...] = jnp.dot(a_ref[...], w_ref[...], preferred_element_type=F32,
                         precision=lax.Precision.HIGHEST) + b_ref[...]


def small_matmul(a, w, b, tn=1536):
    M, K = a.shape
    N = w.shape[1]
    return pl.pallas_call(
        _small_matmul_kernel,
        grid=(N // tn,),
        in_specs=[pl.BlockSpec((M, K), lambda j: (0, 0)),
                  pl.BlockSpec((K, tn), lambda j: (0, j)),
                  pl.BlockSpec((1, tn), lambda j: (0, j))],
        out_specs=pl.BlockSpec((M, tn), lambda j: (0, j)),
        out_shape=jax.ShapeDtypeStruct((M, N), F32),
        compiler_params=_cparams("arbitrary"),
        name="small_matmul",
    )(a, w, b.reshape(1, N))


def _norm_mod(x, g, shift, scale):
    ms = jnp.mean(x * x, axis=-1, keepdims=True)
    y = x * lax.rsqrt(ms + RMS_EPS) * g
    return y * (1.0 + scale) + shift


def _norm_proj_kernel(x_ref, g_ref, sh_ref, sc_ref, w_ref, cs_ref, o_ref):
    h = _norm_mod(x_ref[0], g_ref[...], sh_ref[0], sc_ref[0])
    y = jnp.dot(h.astype(BF16), w_ref[...], preferred_element_type=F32)
    o_ref[0] = (y * cs_ref[...]).astype(o_ref.dtype)


def norm_proj(x, g, shift, scale, w, out_dtype, tm, col_scale=None):
    B, L, D = x.shape
    N = w.shape[1]
    tm = min(tm, L)
    if col_scale is None:
        col_scale = jnp.ones((N,), F32)
    return pl.pallas_call(
        _norm_proj_kernel,
        grid=(B, L // tm),
        in_specs=[pl.BlockSpec((1, tm, D), lambda b, i: (b, i, 0)),
                  pl.BlockSpec((1, D), lambda b, i: (0, 0)),
                  pl.BlockSpec((1, 1, D), lambda b, i: (b, 0, 0)),
                  pl.BlockSpec((1, 1, D), lambda b, i: (b, 0, 0)),
                  pl.BlockSpec((D, N), lambda b, i: (0, 0)),
                  pl.BlockSpec((1, N), lambda b, i: (0, 0))],
        out_specs=pl.BlockSpec((1, tm, N), lambda b, i: (b, i, 0)),
        out_shape=jax.ShapeDtypeStruct((B, L, N), out_dtype),
        compiler_params=_cparams("arbitrary", "arbitrary"),
        name="norm_proj",
    )(x, g.reshape(1, D), shift.reshape(B, 1, D), scale.reshape(B, 1, D), w, col_scale.reshape(1, N))


def _norm_router_kernel(x_ref, g_ref, sh_ref, sc_ref, rw_ref, h_ref, aff_ref, *, token_tiles):
    h = _norm_mod(x_ref[0], g_ref[...], sh_ref[0], sc_ref[0])
    if token_tiles:
        n_tok, D = h.shape
        tiles = h_ref.at[0]
        for c in range(D // LANES):
            tiles[pl.ds(c, n_tok, stride=D // LANES), :] = h[:, c * LANES:(c + 1) * LANES]
    else:
        h_ref[0] = h.astype(h_ref.dtype)
    logits = jnp.dot(h, rw_ref[...], preferred_element_type=F32, precision=lax.Precision.HIGHEST)
    m = jnp.max(logits, axis=-1, keepdims=True)
    e = jnp.exp(logits - m)
    aff_ref[0] = e / jnp.sum(e, axis=-1, keepdims=True)


def norm_router(x, g, shift, scale, router_w, tm, token_tiles):
    B, L, D = x.shape
    E = router_w.shape[1]
    tm = min(tm, L)
    if token_tiles:
        sub = D // LANES
        h_spec = pl.BlockSpec((1, tm * sub, LANES), lambda b, i: (b, i, 0))
        h_shape = jax.ShapeDtypeStruct((B, L * sub, LANES), F32)
    else:
        h_spec = pl.BlockSpec((1, tm, D), lambda b, i: (b, i, 0))
        h_shape = jax.ShapeDtypeStruct((B, L, D), BF16)
    return pl.pallas_call(
        functools.partial(_norm_router_kernel, token_tiles=token_tiles),
        grid=(B, L // tm),
        in_specs=[pl.BlockSpec((1, tm, D), lambda b, i: (b, i, 0)),
                  pl.BlockSpec((1, D), lambda b, i: (0, 0)),
                  pl.BlockSpec((1, 1, D), lambda b, i: (b, 0, 0)),
                  pl.BlockSpec((1, 1, D), lambda b, i: (b, 0, 0)),
                  pl.BlockSpec((D, E), lambda b, i: (0, 0))],
        out_specs=[h_spec, pl.BlockSpec((1, tm, E), lambda b, i: (b, i, 0))],
        out_shape=[h_shape, jax.ShapeDtypeStruct((B, L, E), F32)],
        compiler_params=_cparams("arbitrary", "arbitrary"),
        name="norm_router",
    )(x, g.reshape(1, D), shift.reshape(B, 1, D), scale.reshape(B, 1, D), router_w)


def _proj_residual_kernel(y_ref, w_ref, x_ref, gate_ref, o_ref):
    mix = jnp.dot(y_ref[0].astype(BF16), w_ref[...], preferred_element_type=F32)
    o_ref[0] = x_ref[0] + gate_ref[0] * mix


def proj_residual(y, w, x, gate, tm):
    B, L, K = y.shape
    D = w.shape[1]
    tm = min(tm, L)
    return pl.pallas_call(
        _proj_residual_kernel,
        grid=(B, L // tm),
        in_specs=[pl.BlockSpec((1, tm, K), lambda b, i: (b, i, 0)),
                  pl.BlockSpec((K, D), lambda b, i: (0, 0)),
                  pl.BlockSpec((1, tm, D), lambda b, i: (b, i, 0)),
                  pl.BlockSpec((1, 1, D), lambda b, i: (b, 0, 0))],
        out_specs=pl.BlockSpec((1, tm, D), lambda b, i: (b, i, 0)),
        out_shape=jax.ShapeDtypeStruct((B, L, D), F32),
        compiler_params=_cparams("arbitrary", "arbitrary"),
        name="proj_residual",
    )(y, w, x, gate.reshape(B, 1, D))


def _rms_kernel(x_ref, g_ref, o_ref):
    x = x_ref[0]
    ms = jnp.mean(x * x, axis=-1, keepdims=True)
    o_ref[0] = x * lax.rsqrt(ms + RMS_EPS) * g_ref[...]


def rms_norm_final(x, g, tm=1024):
    B, L, D = x.shape
    return pl.pallas_call(
        _rms_kernel,
        grid=(B, L // tm),
        in_specs=[pl.BlockSpec((1, tm, D), lambda b, i: (b, i, 0)),
                  pl.BlockSpec((1, D), lambda b, i: (0, 0))],
        out_specs=pl.BlockSpec((1, tm, D), lambda b, i: (b, i, 0)),
        out_shape=jax.ShapeDtypeStruct((B, L, D), F32),
        compiler_params=_cparams("arbitrary", "arbitrary"),
        name="rms_final",
    )(x, g.reshape(1, D))


def _two_head_rows(q):
    lane = lax.broadcasted_iota(jnp.int32, q.shape, 1)
    zero = jnp.zeros_like(q)
    return jnp.concatenate([jnp.where(lane < HEAD_DIM, q, zero),
                            jnp.where(lane >= HEAD_DIM, q, zero)], axis=0)


def _merge_two_heads(o):
    n = o.shape[0] // 2
    lane = lax.broadcasted_iota(jnp.int32, (n, LANES), 1)
    return jnp.where(lane < HEAD_DIM, o[:n], o[n:])


_CONTRACT_LAST = (((1,), (1,)), ((), ()))


def _na_key_rows(rq):
    n = rq + NA_WIN_ROWS - 1
    return n + n % 2


def _na_kernel(q_ref, k_ref, v_ref, kc_ref, vc_ref, bias_ref, o_ref, *, rows, rq):
    nq = rq * GRID_W
    nkr = _na_key_rows(rq)
    kwin = nkr * GRID_W
    kc = kc_ref[0]
    vc = vc_ref[0]

    def step(i, carry):
        r = i * rq
        start = jnp.clip(r - NA_WIN_ROWS // 2, 0, rows - nkr)
        variant = r - start
        q0 = pl.multiple_of(r * GRID_W, nq)
        k0 = pl.multiple_of(start * GRID_W, GRID_W)
        q2 = _two_head_rows(q_ref[0, pl.ds(q0, nq), :])
        kw = k_ref[0, pl.ds(k0, kwin), :]
        vw = v_ref[0, pl.ds(k0, kwin), :]
        s_loc = lax.dot_general(q2, kw, _CONTRACT_LAST, preferred_element_type=F32) + bias_ref[0, variant]
        s_ctx = lax.dot_general(q2, kc, _CONTRACT_LAST, preferred_element_type=F32)
        m = jnp.maximum(jnp.max(s_loc, axis=-1, keepdims=True), jnp.max(s_ctx, axis=-1, keepdims=True))
        p_loc = jnp.exp2(s_loc - m)
        p_ctx = jnp.exp2(s_ctx - m)
        denom = jnp.sum(p_loc, axis=-1, keepdims=True) + jnp.sum(p_ctx, axis=-1, keepdims=True)
        o = (jnp.dot(p_loc.astype(BF16), vw, preferred_element_type=F32)
             + jnp.dot(p_ctx.astype(BF16), vc, preferred_element_type=F32))
        o = o / denom
        o_ref[0, pl.ds(q0, nq), :] = _merge_two_heads(o).astype(o_ref.dtype)
        return carry

    lax.fori_loop(0, rows // rq, step, 0, unroll=2)


def _na_bias_tables(rpb, rows, rq):
    W = GRID_W
    nkr = _na_key_rows(rq)
    n_var = nkr - rq + 1
    n_ro, n_co = 2 * NA_WIN_ROWS - 1, 2 * NA_WIN_COLS - 1
    cols = np.arange(W)
    col_start = np.clip(cols - NA_WIN_COLS // 2, 0, W - NA_WIN_COLS)
    row_sel = np.zeros((n_var, rq, nkr, n_ro), np.float32)
    for var in range(n_var):
        r = var if var <= NA_WIN_ROWS // 2 else rows - nkr + var
        start = r - var
        for dr in range(rq):
            q_row = r + dr
            if q_row >= rows:
                continue
            rs = int(np.clip(q_row - NA_WIN_ROWS // 2, 0, rows - NA_WIN_ROWS))
            for kr in range(nkr):
                key_row = start + kr
                if rs <= key_row < rs + NA_WIN_ROWS:
                    row_sel[var, dr, kr, key_row - q_row + NA_WIN_ROWS - 1] = 1.0
    col_sel = np.zeros((n_co, W, W), np.float32)
    for q in range(W):
        for kc in range(col_start[q], col_start[q] + NA_WIN_COLS):
            col_sel[kc - q + NA_WIN_COLS - 1, q, kc] = 1.0
    inside = (row_sel.sum(-1)[:, :, :, None, None] * col_sel.sum(0)[None, None, None]) > 0
    hp = lax.Precision.HIGHEST
    by_row = jnp.einsum('vdko,hoc->hvdkc', row_sel, rpb.astype(F32), precision=hp)
    tab = jnp.einsum('hvdkc,cqn->hvdkqn', by_row, col_sel, precision=hp)
    tab = jnp.where(inside[None], tab * LOG2E, NEG_INF)
    tab = tab.reshape(HEAD_PAIRS, 2, n_var, rq, nkr, W, W).transpose(0, 2, 1, 3, 5, 4, 6)
    return tab.reshape(HEAD_PAIRS, n_var, 2 * rq * W, nkr * W)


def na_attention(p, pc, kc_off, rpb):
    B, L, _ = p.shape
    Lc = pc.shape[1]
    rows = L // GRID_W
    rq = NA_Q_ROWS
    bias = _na_bias_tables(rpb, rows, rq)
    nkr = _na_key_rows(rq)
    kern = functools.partial(_na_kernel, rows=rows, rq=rq)
    return pl.pallas_call(
        kern,
        grid=(B, HEAD_PAIRS),
        in_specs=[pl.BlockSpec((1, L, LANES), lambda b, j: (b, 0, j)),
                  pl.BlockSpec((1, L, LANES), lambda b, j: (b, 0, HEAD_PAIRS + j)),
                  pl.BlockSpec((1, L, LANES), lambda b, j: (b, 0, 2 * HEAD_PAIRS + j)),
                  pl.BlockSpec((1, Lc, LANES), lambda b, j: (b, 0, kc_off + j)),
                  pl.BlockSpec((1, Lc, LANES), lambda b, j: (b, 0, kc_off + HEAD_PAIRS + j)),
                  pl.BlockSpec((1, nkr - rq + 1, 2 * rq * GRID_W, nkr * GRID_W), lambda b, j: (j, 0, 0, 0))],
        out_specs=pl.BlockSpec((1, L, LANES), lambda b, j: (b, 0, j)),
        out_shape=jax.ShapeDtypeStruct((B, L, NA_WIDTH), BF16),
        compiler_params=_cparams("arbitrary", "arbitrary"),
        name="na_attention",
    )(p, p, p, pc, pc, bias)


def _ctx_attn_kernel(q_ref, k_ref, v_ref, o_ref):
    q2 = _two_head_rows(q_ref[0])
    s = lax.dot_general(q2, k_ref[0], _CONTRACT_LAST, preferred_element_type=F32)
    m = jnp.max(s, axis=-1, keepdims=True)
    e = jnp.exp2(s - m)
    o = jnp.dot(e.astype(BF16), v_ref[0], preferred_element_type=F32) / jnp.sum(e, axis=-1, keepdims=True)
    o_ref[0] = _merge_two_heads(o).astype(o_ref.dtype)


def ctx_attention(pc):
    B, Lc, _ = pc.shape
    return pl.pallas_call(
        _ctx_attn_kernel,
        grid=(B, HEAD_PAIRS),
        in_specs=[pl.BlockSpec((1, Lc, LANES), lambda b, j: (b, 0, j)),
                  pl.BlockSpec((1, Lc, LANES), lambda b, j: (b, 0, HEAD_PAIRS + j)),
                  pl.BlockSpec((1, Lc, LANES), lambda b, j: (b, 0, 2 * HEAD_PAIRS + j))],
        out_specs=pl.BlockSpec((1, Lc, LANES), lambda b, j: (b, 0, j)),
        out_shape=jax.ShapeDtypeStruct((B, Lc, NA_WIDTH), BF16),
        compiler_params=_cparams("arbitrary", "arbitrary"),
        name="ctx_attention",
    )(pc, pc, pc)


def _expert_ffn_kernel(x_ref, gate_ref, wg_ref, wu_ref, wd_ref, o_ref, *, fchunk):
    o_ref[0, 0] = _swiglu(x_ref[0, 0], wg_ref, wu_ref, wd_ref, fchunk) * gate_ref[0, 0]


def expert_ffn(xe, gate, wg, wu, wd, fchunk=512):
    E, B, cap, D = xe.shape
    F = wg.shape[2]
    kern = functools.partial(_expert_ffn_kernel, fchunk=fchunk)
    return pl.pallas_call(
        kern,
        grid=(E, B),
        in_specs=[pl.BlockSpec((1, 1, cap, D), lambda e, b: (e, b, 0, 0)),
                  pl.BlockSpec((1, 1, cap, 1), lambda e, b: (e, b, 0, 0)),
                  pl.BlockSpec((1, D, F), lambda e, b: (e, 0, 0)),
                  pl.BlockSpec((1, D, F), lambda e, b: (e, 0, 0)),
                  pl.BlockSpec((1, F, D), lambda e, b: (e, 0, 0))],
        out_specs=pl.BlockSpec((1, 1, cap, D), lambda e, b: (e, b, 0, 0)),
        out_shape=jax.ShapeDtypeStruct((E, B, cap, D), F32),
        compiler_params=_cparams("arbitrary", "arbitrary"),
        name="expert_ffn",
    )(xe, gate, wg, wu, wd)


def _swiglu(x, wg_ref, wu_ref, wd_ref, fchunk):
    F = wg_ref.shape[2]
    acc = jnp.zeros((x.shape[0], wd_ref.shape[2]), F32)
    for f0 in range(0, F, fchunk):
        g = jnp.dot(x, wg_ref[0, :, f0:f0 + fchunk], preferred_element_type=F32)
        u = jnp.dot(x, wu_ref[0, :, f0:f0 + fchunk], preferred_element_type=F32)
        hid = (g * jax.nn.sigmoid(g) * u).astype(BF16)
        acc = acc + jnp.dot(hid, wd_ref[0, f0:f0 + fchunk, :], preferred_element_type=F32)
    return acc


def _gather_rows_copy(h_hbm, xbuf, sem, slot, row, j, sub):
    src = h_hbm.at[pl.ds(pl.multiple_of(row, sub), sub)]
    dst = xbuf.at[slot, pl.ds(pl.multiple_of(j * sub, sub), sub)]
    return pltpu.make_async_copy(src, dst, sem.at[slot])


def _expert_ffn_gather_kernel(rows_ref, rows_next_ref, gate_ref, h_hbm, wg_ref, wu_ref, wd_ref, o_ref,
                              xbuf, xb_ref, acc_ref, sem, *, fchunk):
    sub = xb_ref.shape[1] // LANES
    cap = xbuf.shape[1] // sub
    step = pl.program_id(0) * pl.num_programs(1) + pl.program_id(1)
    nsteps = pl.num_programs(0) * pl.num_programs(1)
    slot = step % 2

    def drain(s):
        pltpu.make_async_copy(h_hbm.at[pl.ds(0, cap * sub)], xbuf.at[s], sem.at[s]).wait()

    @pl.when(step == 0)
    def _():
        def body(j, carry):
            _gather_rows_copy(h_hbm, xbuf, sem, 0, rows_ref[0, 0, j], j, sub).start()
            return carry
        lax.fori_loop(0, cap, body, 0)

    drain(slot)
    for c in range(sub):
        xb_ref[:, c * LANES:(c + 1) * LANES] = xbuf[slot, pl.ds(c, cap, stride=sub), :].astype(BF16)
    acc_ref[...] = jnp.zeros_like(acc_ref)
    n_chunks = wg_ref.shape[2] // fchunk
    n_issuing = max(n_chunks // 2, 1)
    rows_per_chunk = cap // n_issuing

    def chunk(k, carry, issue):
        if issue:
            for jj in range(rows_per_chunk):
                j = k * rows_per_chunk + jj
                _gather_rows_copy(h_hbm, xbuf, sem, 1 - slot, rows_next_ref[0, 0, j], j, sub).start()
        f0 = pl.multiple_of(k * fchunk, fchunk)
        x = xb_ref[...]
        g = jnp.dot(x, wg_ref[0, :, pl.ds(f0, fchunk)], preferred_element_type=F32)
        u = jnp.dot(x, wu_ref[0, :, pl.ds(f0, fchunk)], preferred_element_type=F32)
        hid = (g * jax.nn.sigmoid(g) * u).astype(BF16)
        acc_ref[...] += jnp.dot(hid, wd_ref[0, pl.ds(f0, fchunk), :], preferred_element_type=F32)
        return carry
    lax.fori_loop(0, n_issuing, functools.partial(chunk, issue=True), 0)
    lax.fori_loop(n_issuing, n_chunks, functools.partial(chunk, issue=False), 0)
    o_ref[0, 0] = (acc_ref[...] * gate_ref[0, 0]).astype(o_ref.dtype)

    @pl.when(step == nsteps - 1)
    def _():
        drain(1 - slot)


def expert_ffn_gather(h_rows, rows, gate, wg, wu, wd, fchunk=512):
    E, B, cap = rows.shape
    D = wg.shape[1]
    sub = D // LANES
    rows = rows * sub
    F = wg.shape[2]

    rows = rows.reshape(E * B, 1, cap)
    fchunk = min(fchunk, F)

    def next_group(e, b):
        return (jnp.minimum(e * B + b + 1, E * B - 1), 0, 0)

    kern = functools.partial(_expert_ffn_gather_kernel, fchunk=fchunk)
    return pl.pallas_call(
        kern,
        grid=(E, B),
        in_specs=[pl.BlockSpec((1, 1, cap), lambda e, b: (e * B + b, 0, 0), memory_space=pltpu.SMEM),
                  pl.BlockSpec((1, 1, cap), next_group, memory_space=pltpu.SMEM),
                  pl.BlockSpec((1, 1, cap, 1), lambda e, b: (e, b, 0, 0)),
                  pl.BlockSpec(memory_space=pl.ANY),
                  pl.BlockSpec((1, D, F), lambda e, b: (e, 0, 0)),
                  pl.BlockSpec((1, D, F), lambda e, b: (e, 0, 0)),
                  pl.BlockSpec((1, F, D), lambda e, b: (e, 0, 0))],
        out_specs=pl.BlockSpec((1, 1, cap, D), lambda e, b: (e, b, 0, 0)),
        out_shape=jax.ShapeDtypeStruct((E, B, cap, D), BF16),
        scratch_shapes=[pltpu.VMEM((2, cap * sub, LANES), F32), pltpu.VMEM((cap, D), BF16), pltpu.VMEM((cap, D), F32),
                        pltpu.SemaphoreType.DMA((2,))],
        compiler_params=_cparams("arbitrary", "arbitrary"),
        name="expert_ffn_gather",
    )(rows, rows, gate, h_rows, wg, wu, wd)


COMBINE_TOKENS = 512
COMBINE_WINDOW = 256
COMBINE_CHUNK = LANES


def _window_copy(ye_hbm, win, sem, e, b, w0, slot):
    return pltpu.make_async_copy(ye_hbm.at[e, b, pl.ds(w0, COMBINE_WINDOW)], win.at[slot], sem.at[slot])


def _moe_combine_kernel(starts_ref, tok_ref, x_ref, gate_ref, ye_hbm, o_ref, acc_ref, win, sem):
    b = pl.program_id(0)
    i = pl.program_id(1)
    n_exp = tok_ref.shape[0]
    cap = ye_hbm.shape[2]
    T = x_ref.shape[1]
    tile0 = i * T
    tok_iota = lax.broadcasted_iota(jnp.int32, (T, COMBINE_CHUNK), 0) + tile0

    nt = pl.num_programs(1)
    step = b * nt + i
    parity = step % 2

    def first_window(e, bb=b, ii=i):
        s = starts_ref[bb, e, ii]
        return jnp.minimum((s // COMBINE_CHUNK) * COMBINE_CHUNK, cap - COMBINE_WINDOW)

    def fetch_first_windows(bb, ii, par):
        for e in range(n_exp):
            _window_copy(ye_hbm, win, sem, e, bb, first_window(e, bb, ii), par * n_exp + e).start()

    def one_hot(e, w0, first_row=None):
        k = w0 // COMBINE_CHUNK
        hot = []
        for c in range(COMBINE_WINDOW // COMBINE_CHUNK):
            match = tok_iota == tok_ref[e, 0, pl.ds(k + c, 1), :]
            if first_row is not None:
                row = lax.broadcasted_iota(jnp.int32, match.shape, 1) + (w0 + c * COMBINE_CHUNK)
                match = match & (row >= first_row)
            hot.append(match.astype(BF16))
        return hot

    @pl.when(step == 0)
    def _():
        fetch_first_windows(b, i, 0)

    @pl.when(step + 1 < pl.num_programs(0) * nt)
    def _():
        fetch_first_windows((step + 1) // nt, (step + 1) % nt, 1 - parity)

    firsts = [first_window(e) for e in range(n_exp)]
    hot = []
    for e in range(n_exp):
        hot += one_hot(e, firsts[e])
    for e in range(n_exp):
        _window_copy(ye_hbm, win, sem, e, b, firsts[e], parity * n_exp + e).wait()
    rows_all = win[pl.ds(parity * n_exp, n_exp)].reshape(n_exp * COMBINE_WINDOW, win.shape[2])
    acc_ref[...] = jnp.dot(jnp.concatenate(hot, axis=1), rows_all, preferred_element_type=F32)

    def per_expert(e, carry):
        w0 = first_window(e)
        end = starts_ref[b, e, i + 1]
        n_more = jnp.maximum(end - (w0 + COMBINE_WINDOW) + COMBINE_WINDOW - 1, 0) // COMBINE_WINDOW

        def more(m, c):
            first_row = w0 + (m + 1) * COMBINE_WINDOW
            w = jnp.minimum(first_row, cap - COMBINE_WINDOW)
            cp = _window_copy(ye_hbm, win, sem, e, b, w, 2 * n_exp)
            cp.start()
            cp.wait()
            sel = jnp.concatenate(one_hot(e, w, first_row), axis=1)
            acc_ref[...] += jnp.dot(sel, win[2 * n_exp], preferred_element_type=F32)
            return c
        lax.fori_loop(0, n_more, more, 0)
        return carry

    lax.fori_loop(0, n_exp, per_expert, 0)
    o_ref[0] = x_ref[0] + gate_ref[0] * acc_ref[...]


def moe_combine(x, gate_res, ye, tok):
    B, n, D = x.shape
    E, _, cap, _ = ye.shape
    T = COMBINE_TOKENS
    nt = n // T
    bounds = jnp.arange(nt + 1, dtype=jnp.int32) * T
    starts = jnp.sum(tok[:, :, None, :] < bounds[None, None, :, None], axis=-1, dtype=jnp.int32)
    starts = jnp.transpose(starts, (1, 0, 2))
    tok4 = tok.reshape(E, B, cap // COMBINE_CHUNK, COMBINE_CHUNK)
    grid_spec = pltpu.PrefetchScalarGridSpec(
        num_scalar_prefetch=1,
        grid=(B, nt),
        in_specs=[pl.BlockSpec((E, 1, cap // COMBINE_CHUNK, COMBINE_CHUNK), lambda b, i, s: (0, b, 0, 0)),
                  pl.BlockSpec((1, T, D), lambda b, i, s: (b, i, 0)),
                  pl.BlockSpec((1, 1, D), lambda b, i, s: (b, 0, 0)),
                  pl.BlockSpec(memory_space=pl.ANY)],
        out_specs=pl.BlockSpec((1, T, D), lambda b, i, s: (b, i, 0)),
        scratch_shapes=[pltpu.VMEM((T, D), F32), pltpu.VMEM((2 * E + 1, COMBINE_WINDOW, D), BF16),
                        pltpu.SemaphoreType.DMA((2 * E + 1,))])
    return pl.pallas_call(
        _moe_combine_kernel,
        grid_spec=grid_spec,
        out_shape=jax.ShapeDtypeStruct((B, n, D), F32),
        compiler_params=_cparams("arbitrary", "arbitrary"),
        name="moe_combine",
    )(starts, tok4, x, gate_res.reshape(B, 1, D), ye)


def expert_choice_ffn(x, g, shift, scale, gate_res, router_w, wg, wu, wd, tm):
    B, n, D = x.shape
    E = N_EXPERTS
    cap = EC_CAPACITY_FACTOR * n // E
    fused = cap % COMBINE_WINDOW == 0 and cap > COMBINE_WINDOW and n % COMBINE_TOKENS == 0
    h, aff = norm_router(x, g, shift, scale, router_w, tm, fused)
    gate, idx = lax.top_k(jnp.transpose(aff, (2, 0, 1)), cap)
    bidx = jnp.arange(B, dtype=idx.dtype)[None, :, None]
    if fused:
        idx, gate = lax.sort((idx, gate), dimension=2, num_keys=1)
        ye = expert_ffn_gather(h.reshape(-1, LANES), idx + bidx * n, gate[..., None], wg, wu, wd)
        return moe_combine(x, gate_res, ye, idx)
    xe = h[bidx, idx].reshape(E, 1, B * cap, D)
    ye = expert_ffn(xe, gate.reshape(E, 1, B * cap, 1), wg, wu, wd)
    flat = (idx + bidx * n).reshape(-1)
    moe = jnp.zeros((B * n, D), F32).at[flat].add(ye.reshape(-1, D)).reshape(B, n, D)
    return x + gate_res[:, None, :] * moe


DFT_MINOR = 128
DFT_PITCH_PAD = 8
DFT_QB = 32
DFT_UNROLL = 8


def _dft_mats(n_major, a_used):
    N = n_major * DFT_MINOR
    q = np.arange(n_major)
    a = np.arange(a_used)
    b = np.arange(DFT_MINOR)
    ang1 = -2 * np.pi * np.outer(q, a) / n_major
    f1 = np.concatenate([np.cos(ang1), np.sin(ang1)], 0)
    ang2 = -2 * np.pi * np.outer(b, b) / DFT_MINOR
    c2, s2 = np.cos(ang2), np.sin(ang2)
    f2 = np.block([[c2, -s2], [s2, c2]])
    f2c = np.block([[c2, s2], [-s2, c2]])
    lane1 = np.ones((1, 1, LANES))

    def tw(hi, lo):
        ang = -2 * np.pi * np.outer(hi, lo) / N
        return (np.cos(ang)[:, :, None] * lane1, np.sin(ang)[:, :, None] * lane1)

    t1r, t1i = tw(8 * np.arange(DFT_MINOR // 8), q)
    t0r, t0i = tw(np.arange(8), q)
    u1r, u1i = tw(8 * np.arange(n_major // 8), b)
    u0r, u0i = tw(np.arange(8), b)
    f32 = lambda *xs: [np.asarray(x, np.float32) for x in xs]
    return dict(f1=f1, f2=f2, f2c=f2c, tw_fwd=f32(t1r, t1i, t0r, t0i), tw_inv=f32(u1r, u1i, u0r, u0i))


def _cmul(ar, ai, br, bi):
    return ar * br - ai * bi, ar * bi + ai * br


def _dft_fwd_kernel(*refs, n_major, a_used, mode, precision):
    x_ref, f1_ref, f2_ref, t1r, t1i, t0r, t0i = refs[:7]
    rest = refs[7:]
    if mode == "mul":
        hr_ref, hi_ref, or_ref, oi_ref, sr, si = rest
    elif mode == "chanmix":
        mc_ref, ms_ref, or_ref, sr, si = rest
    else:
        or_ref, oi_ref, sr, si = rest
    pitch = n_major + DFT_PITCH_PAD
    qb = pl.program_id(2)
    mm_dtype = f1_ref.dtype

    @pl.when(qb == 0)
    def _stage1():
        def slab(b, carry):
            y = jnp.dot(f1_ref[...], x_ref[0, b].astype(mm_dtype), preferred_element_type=F32,
                        precision=precision)
            twr, twi = _cmul(t1r[b // 8], t1i[b // 8], t0r[b % 8], t0i[b % 8])
            yr, yi = _cmul(y[:n_major], y[n_major:], twr, twi)
            row = pl.multiple_of(b * pitch, 8)
            sr[pl.ds(row, n_major), :] = yr
            si[pl.ds(row, n_major), :] = yi
            return carry
        lax.fori_loop(0, DFT_MINOR, slab, 0, unroll=DFT_UNROLL)

    def freq(j, carry):
        q = qb * DFT_QB + j
        g = jnp.concatenate([sr[pl.ds(q, DFT_MINOR, stride=pitch), :],
                             si[pl.ds(q, DFT_MINOR, stride=pitch), :]], axis=0).astype(mm_dtype)
        xf = jnp.dot(f2_ref[...], g, preferred_element_type=F32, precision=precision)
        xr, xi = xf[:DFT_MINOR], xf[DFT_MINOR:]
        if mode == "mul":
            xr, xi = _cmul(xr, xi, hr_ref[0, j].astype(F32), hi_ref[0, j].astype(F32))
            or_ref[0, j] = xr.astype(or_ref.dtype)
            oi_ref[0, j] = xi.astype(oi_ref.dtype)
        elif mode == "chanmix":
            or_ref[0, j] = (jnp.dot(xr.astype(mm_dtype), mc_ref[...], preferred_element_type=F32)
                            + jnp.dot(xi.astype(mm_dtype), ms_ref[...], preferred_element_type=F32))
        else:
            or_ref[0, j] = xr
            oi_ref[0, j] = xi
        return carry
    lax.fori_loop(0, DFT_QB, freq, 0, unroll=DFT_UNROLL)


def dft_forward(xp, n_major, mode="spectrum", h=None, chan=None, precise=False):
    Bb, _, a_used, C = xp.shape
    mats = _dft_mats(n_major, a_used)
    mm_dtype = F32 if precise else BF16
    precision = lax.Precision.HIGHEST if precise else None
    pitch = n_major + DFT_PITCH_PAD
    const = lambda shape: pl.BlockSpec(shape, lambda bb, cb, qb: (0,) * len(shape))
    spec_blk = pl.BlockSpec((1, DFT_QB, DFT_MINOR, LANES), lambda bb, cb, qb: (bb, qb, 0, cb))
    args = [xp, jnp.asarray(mats["f1"], mm_dtype), jnp.asarray(mats["f2"], mm_dtype)] + mats["tw_fwd"]
    in_specs = [pl.BlockSpec((1, DFT_MINOR, a_used, LANES), lambda bb, cb, qb: (bb, 0, 0, cb)),
                const((2 * n_major, a_used)), const((2 * DFT_MINOR, 2 * DFT_MINOR)),
                const((DFT_MINOR // 8, n_major, LANES)), const((DFT_MINOR // 8, n_major, LANES)),
                const((8, n_major, LANES)), const((8, n_major, LANES))]
    spec_shape = (Bb, n_major, DFT_MINOR, C)
    if mode == "mul":
        args += [h[0], h[1]]
        hblk = pl.BlockSpec((1, DFT_QB, DFT_MINOR, LANES), lambda bb, cb, qb: (0, qb, 0, cb))
        in_specs += [hblk, hblk]
        out_specs = [spec_blk, spec_blk]
        out_shape = [jax.ShapeDtypeStruct(spec_shape, BF16)] * 2
    elif mode == "chanmix":
        args += [chan[0].astype(mm_dtype), chan[1].astype(mm_dtype)]
        in_specs += [const((LANES, LANES)), const((LANES, LANES))]
        out_specs = spec_blk
        out_shape = jax.ShapeDtypeStruct(spec_shape, F32)
    else:
        out_specs = [spec_blk, spec_blk]
        out_shape = [jax.ShapeDtypeStruct(spec_shape, F32)] * 2
    kern = functools.partial(_dft_fwd_kernel, n_major=n_major, a_used=a_used, mode=mode, precision=precision)
    return pl.pallas_call(
        kern,
        grid=(Bb, C // LANES, n_major // DFT_QB),
        in_specs=in_specs,
        out_specs=out_specs,
        out_shape=out_shape,
        scratch_shapes=[pltpu.VMEM((DFT_MINOR * pitch, LANES), F32)] * 2,
        compiler_params=_cparams("arbitrary", "arbitrary", "arbitrary"),
        name="dft_forward_" + mode,
    )(*args)


def _dft_inv_kernel(gr_ref, gi_ref, f2c_ref, f1c_ref, u1r, u1i, u0r, u0i, o_ref, sr, si, *, n_major, a_out):
    pitch = DFT_MINOR + DFT_PITCH_PAD
    qb = pl.program_id(2)

    def freq(j, carry):
        q = qb * DFT_QB + j
        g = jnp.concatenate([gr_ref[0, j], gi_ref[0, j]], axis=0)
        t = jnp.dot(f2c_ref[...], g, preferred_element_type=F32)
        twr, twi = _cmul(u1r[q // 8], u1i[q // 8], u0r[q % 8], u0i[q % 8])
        tr, ti = _cmul(t[:DFT_MINOR], t[DFT_MINOR:], twr, -twi)
        row = pl.multiple_of(q * pitch, 8)
        sr[pl.ds(row, DFT_MINOR), :] = tr
        si[pl.ds(row, DFT_MINOR), :] = ti
        return carry
    lax.fori_loop(0, DFT_QB, freq, 0, unroll=DFT_UNROLL)

    @pl.when(qb == pl.num_programs(2) - 1)
    def _stage2():
        def slab(b, carry):
            g = jnp.concatenate([sr[pl.ds(b, n_major, stride=pitch), :],
                                 si[pl.ds(b, n_major, stride=pitch), :]], axis=0).astype(BF16)
            o_ref[0, b] = jnp.dot(f1c_ref[...], g, preferred_element_type=F32)
            return carry
        lax.fori_loop(0, DFT_MINOR, slab, 0, unroll=DFT_UNROLL)


def dft_inverse_real(gr, gi, a_out):
    Bb, n_major, _, C = gr.shape
    N = n_major * DFT_MINOR
    mats = _dft_mats(n_major, a_out)
    ang = 2 * np.pi * np.outer(np.arange(a_out), np.arange(n_major)) / n_major
    f1c = np.concatenate([np.cos(ang), -np.sin(ang)], 1) / N
    pitch = DFT_MINOR + DFT_PITCH_PAD
    const = lambda shape: pl.BlockSpec(shape, lambda bb, cb, qb: (0,) * len(shape))
    gblk = pl.BlockSpec((1, DFT_QB, DFT_MINOR, LANES), lambda bb, cb, qb: (bb, qb, 0, cb))
    kern = functools.partial(_dft_inv_kernel, n_major=n_major, a_out=a_out)
    return pl.pallas_call(
        kern,
        grid=(Bb, C // LANES, n_major // DFT_QB),
        in_specs=[gblk, gblk, const((2 * DFT_MINOR, 2 * DFT_MINOR)), const((a_out, 2 * n_major)),
                  const((n_major // 8, DFT_MINOR, LANES)), const((n_major // 8, DFT_MINOR, LANES)),
                  const((8, DFT_MINOR, LANES)), const((8, DFT_MINOR, LANES))],
        out_specs=pl.BlockSpec((1, DFT_MINOR, a_out, LANES), lambda bb, cb, qb: (bb, 0, 0, cb)),
        out_shape=jax.ShapeDtypeStruct((Bb, DFT_MINOR, a_out, C), F32),
        scratch_shapes=[pltpu.VMEM((n_major * pitch, LANES), F32)] * 2,
        compiler_params=_cparams("arbitrary", "arbitrary", "arbitrary"),
        name="dft_inverse",
    )(gr, gi, jnp.asarray(mats["f2c"], BF16), jnp.asarray(f1c, BF16), *mats["tw_inv"])


def _to_slabs(x, a_used):
    Bb, _, C = x.shape
    return x.reshape(Bb, a_used, DFT_MINOR, C).transpose(0, 2, 1, 3)


def _from_slabs(xs):
    Bb, m, a, C = xs.shape
    return xs.transpose(0, 2, 1, 3).reshape(Bb, a * m, C)


def _channel_dft_mats(L):
    c = np.arange(FN_GROUP_DIM)
    ang = -2 * np.pi * np.outer(c, c) / FN_GROUP_DIM
    eye = np.eye(LANES // FN_GROUP_DIM)
    norm = 1.0 / math.sqrt(L * FN_GROUP_DIM)
    return (jnp.asarray(np.kron(eye, np.cos(ang)) * norm, F32),
            jnp.asarray(np.kron(eye, -np.sin(ang)) * norm, F32))


def fourier_mix_pallas(u):
    B, L, C = u.shape
    n_major = L // DFT_MINOR
    y = dft_forward(_to_slabs(u, n_major), n_major, mode="chanmix", chan=_channel_dft_mats(L))
    return y.transpose(0, 2, 1, 3).reshape(B, L, C)


def long_conv_pallas(vx, h_fwd, h_bwd):
    B, L, D = vx.shape
    n_major = 2 * L // DFT_MINOR
    taps = _to_slabs(jnp.stack([h_fwd, h_bwd]), n_major // 2)
    tr, ti = dft_forward(taps, n_major, mode="spectrum")
    h = ((tr[0:1] + tr[1:2]).astype(BF16), (ti[0:1] - ti[1:2]).astype(BF16))
    gr, gi = dft_forward(_to_slabs(vx, n_major // 2), n_major, mode="mul", h=h)
    return _from_slabs(dft_inverse_real(gr, gi, n_major // 2))


def fourier_mix(u):
    B, L, _ = u.shape
    if L % (DFT_MINOR * DFT_QB) == 0:
        return fourier_mix_pallas(u.astype(F32))
    ug = u.astype(F32).reshape(B, L, FN_GROUPS, FN_GROUP_DIM)
    y = jnp.fft.fft2(ug, axes=(1, 3), norm='ortho').real
    return y.reshape(B, L, FN_WIDTH)


def hyena_filters(L, w1, b1, freq, w2, b2, w3):
    hp = lax.Precision.HIGHEST
    t = jnp.linspace(0.0, 1.0, L, dtype=F32)[:, None]
    w = 2 * math.pi * jnp.arange(L, dtype=F32)[:, None] / L
    bands = jnp.linspace(1e-4, HY_BANDS - 1, HY_BANDS, dtype=F32)[None, :]
    z = jnp.concatenate([t, jnp.cos(bands * w), -jnp.sin(bands * w)], axis=-1)
    hid = jnp.sin(freq * (jnp.dot(z, w1, precision=hp) + b1))
    hid = jnp.sin(freq * (jnp.dot(hid, w2, precision=hp) + b2))
    h = jnp.dot(hid, w3, precision=hp).reshape(L, 2, D_MODEL)
    deltas = jnp.abs(jnp.linspace(HY_MIN_DECAY, HY_MAX_DECAY, D_MODEL, dtype=F32))
    window = jnp.exp(-t * deltas[None, :])
    h = h * window[:, None, :]
    h_fwd = h[:, 0]
    h_bwd = h[:, 1] * (jnp.arange(L) > 0)[:, None]
    norm = jnp.sum(jnp.abs(h_fwd), axis=0, keepdims=True) + jnp.sum(jnp.abs(h_bwd), axis=0, keepdims=True)
    return h_fwd / norm, h_bwd / norm


HALO_ROWS = 8


def _hyena_pre_kernel(u_ref, prev_ref, next_ref, w_ref, b_ref, x0_ref, vx_ref):
    i = pl.program_id(1)
    u = u_ref[0]
    T = u.shape[0]
    D = x0_ref.shape[2]
    row = lax.broadcasted_iota(jnp.int32, (T, 1), 0)
    before = jnp.where(i > 0, prev_ref[0, HALO_ROWS - 1:HALO_ROWS, :], 0.0)
    after = jnp.where(i < pl.num_programs(1) - 1, next_ref[0, 0:1, :], 0.0)
    u_m1 = jnp.where(row == 0, before, pltpu.roll(u, 1, axis=0))
    u_p1 = jnp.where(row == T - 1, after, pltpu.roll(u, T - 1, axis=0))
    y = u_m1 * w_ref[0:1, :] + u * w_ref[1:2, :] + u_p1 * w_ref[2:3, :] + b_ref[...]
    x0_ref[0] = y[:, :D]
    vx_ref[0] = y[:, 2 * D:] * y[:, D:2 * D]


def hyena_pre(u, conv_w, conv_b, tm=256):
    B, L, D3 = u.shape
    D = D3 // 3
    tm = min(tm, L)
    hb = tm // HALO_ROWS
    last = L // HALO_ROWS - 1
    return pl.pallas_call(
        _hyena_pre_kernel,
        grid=(B, L // tm),
        in_specs=[pl.BlockSpec((1, tm, D3), lambda b, i: (b, i, 0)),
                  pl.BlockSpec((1, HALO_ROWS, D3), lambda b, i: (b, jnp.maximum(i * hb - 1, 0), 0)),
                  pl.BlockSpec((1, HALO_ROWS, D3), lambda b, i: (b, jnp.minimum((i + 1) * hb, last), 0)),
                  pl.BlockSpec((HY_SHORT, D3), lambda b, i: (0, 0)),
                  pl.BlockSpec((1, D3), lambda b, i: (0, 0))],
        out_specs=[pl.BlockSpec((1, tm, D), lambda b, i: (b, i, 0)),
                   pl.BlockSpec((1, tm, D), lambda b, i: (b, i, 0))],
        out_shape=[jax.ShapeDtypeStruct((B, L, D), F32)] * 2,
        compiler_params=_cparams("arbitrary", "arbitrary"),
        name="hyena_pre",
    )(u, u, u, conv_w, conv_b.reshape(1, D3))


def _hyena_out_kernel(y_ref, vx_ref, x0_ref, skip_ref, w_ref, x_ref, gate_ref, o_ref):
    z = x0_ref[0] * (y_ref[0] + vx_ref[0] * skip_ref[...])
    mix = jnp.dot(z.astype(BF16), w_ref[...], preferred_element_type=F32)
    o_ref[0] = x_ref[0] + gate_ref[0] * mix


def hyena_out_residual(y, vx, x0, skip, w, x, gate, tm):
    B, L, D = x.shape
    tm = min(tm, L)
    tile = pl.BlockSpec((1, tm, D), lambda b, i: (b, i, 0))
    return pl.pallas_call(
        _hyena_out_kernel,
        grid=(B, L // tm),
        in_specs=[tile, tile, tile,
                  pl.BlockSpec((1, D), lambda b, i: (0, 0)),
                  pl.BlockSpec((D, D), lambda b, i: (0, 0)),
                  tile,
                  pl.BlockSpec((1, 1, D), lambda b, i: (b, 0, 0))],
        out_specs=tile,
        out_shape=jax.ShapeDtypeStruct((B, L, D), F32),
        compiler_params=_cparams("arbitrary", "arbitrary"),
        name="hyena_out_residual",
    )(y, vx, x0, skip.reshape(1, D), w, x, gate.reshape(B, 1, D))


def hyena_mixer_residual(x, u, conv_w, conv_b, taps, skip, w_out, gate, tm):
    L = u.shape[1]
    x0, vx = hyena_pre(u, conv_w, conv_b)
    h_fwd, h_bwd = taps
    if (2 * L) % (DFT_MINOR * DFT_QB) == 0:
        y = long_conv_pallas(vx, h_fwd, h_bwd)
    else:
        filt = jnp.concatenate([h_fwd, jnp.zeros((1, h_fwd.shape[1]), F32), h_bwd[1:][::-1]], axis=0)
        y = jnp.fft.irfft(jnp.fft.rfft(vx, n=2 * L, axis=1) * jnp.fft.rfft(filt, n=2 * L, axis=0)[None],
                          n=2 * L, axis=1)[:, :L]
    return hyena_out_residual(y, vx, x0, skip, w_out, x, gate, tm)


def kernel(x, c, ctx, c_ctx, mod_w, mod_b, norm_mix_g, norm_ffn_g, mix_w_in, na_rpb, mix_w_out,
           hy_w_in, hy_conv_w, hy_conv_b, hy_f_w1, hy_f_b1, hy_f_freq, hy_f_w2, hy_f_b2, hy_f_w3,
           hy_skip, hy_w_out, router_w, exp_w_gate, exp_w_up, exp_w_down, final_norm_g):
    depth = mod_w.shape[0]
    B, L, D = x.shape
    Lc = ctx.shape[1]
    last_ctx_read = depth - 1 if (depth - 1) % 2 == 0 else depth - 2
    TM = 512

    cs = ctx
    cond = jnp.concatenate([jax.nn.silu(c), jax.nn.silu(c_ctx)[None], jnp.zeros((8 - (B + 1) % 8, D), F32)], 0)

    for layer in range(depth):
        j = layer // 2
        m_all = small_matmul(cond, mod_w[layer], mod_b[layer])
        sh1, sc1, g1, sh2, sc2, g2 = jnp.split(m_all[:B], 6, axis=-1)
        m_ctx = jnp.broadcast_to(m_all[B:B + 1], (B, 6 * D))
        csh1, csc1, cg1, csh2, csc2, cg2 = jnp.split(m_ctx, 6, axis=-1)
        upd = layer < last_ctx_read
        wg = exp_w_gate[layer].astype(BF16)
        wu = exp_w_up[layer].astype(BF16)
        wd = exp_w_down[layer].astype(BF16)

        if layer % 2 == 0:
            w_in = mix_w_in[j].astype(BF16)
            w_out = mix_w_out[j].astype(BF16)
            q_scale = jnp.concatenate([jnp.full((NA_WIDTH,), QK_SCALE, F32),
                                       jnp.ones((w_in.shape[1] - NA_WIDTH,), F32)])
            if upd:
                pc = norm_proj(cs, norm_mix_g[layer], csh1, csc1, w_in, BF16, TM, q_scale)
                kc_off = HEAD_PAIRS
                cmix_in = jnp.concatenate([ctx_attention(pc), fourier_mix(pc[..., 3 * NA_WIDTH:]).astype(BF16)], -1)
            else:
                pc = norm_proj(cs, norm_mix_g[layer], csh1, csc1, w_in[:, NA_WIDTH:3 * NA_WIDTH], BF16, TM)
                kc_off = 0
            p = norm_proj(x, norm_mix_g[layer], sh1, sc1, w_in, BF16, TM, q_scale)
            att = na_attention(p, pc, kc_off, na_rpb[j])
            mix_in = jnp.concatenate([att, fourier_mix(p[..., 3 * NA_WIDTH:]).astype(BF16)], -1)
            x = proj_residual(mix_in, w_out, x, g1, TM)
            if upd:
                cs = proj_residual(cmix_in, w_out, cs, cg1, TM)
        else:
            w_in = hy_w_in[j].astype(BF16)
            w_out = hy_w_out[j].astype(BF16)
            fargs = (hy_f_w1[j], hy_f_b1[j], hy_f_freq[j], hy_f_w2[j], hy_f_b2[j], hy_f_w3[j])
            u = norm_proj(x, norm_mix_g[layer], sh1, sc1, w_in, F32, TM)
            x = hyena_mixer_residual(x, u, hy_conv_w[j], hy_conv_b[j], hyena_filters(L, *fargs), hy_skip[j],
                                     w_out, g1, TM)
            if upd:
                uc = norm_proj(cs, norm_mix_g[layer], csh1, csc1, w_in, F32, TM)
                cs = hyena_mixer_residual(cs, uc, hy_conv_w[j], hy_conv_b[j], hyena_filters(Lc, *fargs),
                                          hy_skip[j], w_out, cg1, TM)

        x = expert_choice_ffn(x, norm_ffn_g[layer], sh2, sc2, g2, router_w[layer], wg, wu, wd, TM)
        if upd:
            cs = expert_choice_ffn(cs, norm_ffn_g[layer], csh2, csc2, cg2, router_w[layer], wg, wu, wd, TM)

    return rms_norm_final(x, final_norm_g)
```

```python
import functools
import math

import numpy as np
import jax
import jax.numpy as jnp
from jax import lax
from jax.experimental import pallas as pl
from jax.experimental.pallas import tpu as pltpu

D_MODEL = 1024
GRID_W = 64
HEAD_DIM = 64
NA_HEADS = 12
NA_WIDTH = NA_HEADS * HEAD_DIM
NA_WIN_ROWS = 8
NA_WIN_COLS = 16
FN_GROUPS = 4
FN_GROUP_DIM = 64
FN_WIDTH = FN_GROUPS * FN_GROUP_DIM
HY_SHORT = 3
HY_EMB = 33
HY_BANDS = (HY_EMB - 1) // 2
HY_MIN_DECAY = math.log(1e-2) / 1.5
HY_MAX_DECAY = math.log(1e-2) / 0.3
N_EXPERTS = 16
EC_CAPACITY_FACTOR = 2
RMS_EPS = 1e-6

LANES = 128
HEAD_PAIRS = NA_WIDTH // LANES
NA_Q_ROWS = 2
NEG_INF = -1e30
LOG2E = math.log2(math.e)
QK_SCALE = HEAD_DIM ** -0.5 * LOG2E
VMEM_LIMIT = 56 * 1024 * 1024

BF16 = jnp.bfloat16
F32 = jnp.float32


def _cparams(*sem):
    return pltpu.CompilerParams(dimension_semantics=sem, vmem_limit_bytes=VMEM_LIMIT)


def _small_matmul_kernel(a_ref, w_ref, b_ref, o_ref):
    o_ref[...] = jnp.dot(a_ref[...], w_ref[...], preferred_element_type=F32,
                         precision=lax.Precision.HIGHEST) + b_ref[...]


def small_matmul(a, w, b, tn=1536):
    M, K = a.shape
    N = w.shape[1]
    return pl.pallas_call(
        _small_matmul_kernel,
        grid=(N // tn,),
        in_specs=[pl.BlockSpec((M, K), lambda j: (0, 0)),
                  pl.BlockSpec((K, tn), lambda j: (0, j)),
                  pl.BlockSpec((1, tn), lambda j: (0, j))],
        out_specs=pl.BlockSpec((M, tn), lambda j: (0, j)),
        out_shape=jax.ShapeDtypeStruct((M, N), F32),
        compiler_params=_cparams("arbitrary"),
        name="small_matmul",
    )(a, w, b.reshape(1, N))


def _norm_mod(x, g, shift, scale):
    ms = jnp.mean(x * x, axis=-1, keepdims=True)
    y = x * lax.rsqrt(ms + RMS_EPS) * g
    return y * (1.0 + scale) + shift


def _norm_proj_kernel(x_ref, g_ref, sh_ref, sc_ref, w_ref, cs_ref, o_ref):
    h = _norm_mod(x_ref[0], g_ref[...], sh_ref[0], sc_ref[0])
    y = jnp.dot(h.astype(BF16), w_ref[...], preferred_element_type=F32)
    o_ref[0] = (y * cs_ref[...]).astype(o_ref.dtype)


def norm_proj(x, g, shift, scale, w, out_dtype, tm, col_scale=None):
    B, L, D = x.shape
    N = w.shape[1]
    tm = min(tm, L)
    if col_scale is None:
        col_scale = jnp.ones((N,), F32)
    return pl.pallas_call(
        _norm_proj_kernel,
        grid=(B, L // tm),
        in_specs=[pl.BlockSpec((1, tm, D), lambda b, i: (b, i, 0)),
                  pl.BlockSpec((1, D), lambda b, i: (0, 0)),
                  pl.BlockSpec((1, 1, D), lambda b, i: (b, 0, 0)),
                  pl.BlockSpec((1, 1, D), lambda b, i: (b, 0, 0)),
                  pl.BlockSpec((D, N), lambda b, i: (0, 0)),
                  pl.BlockSpec((1, N), lambda b, i: (0, 0))],
        out_specs=pl.BlockSpec((1, tm, N), lambda b, i: (b, i, 0)),
        out_shape=jax.ShapeDtypeStruct((B, L, N), out_dtype),
        compiler_params=_cparams("arbitrary", "arbitrary"),
        name="norm_proj",
    )(x, g.reshape(1, D), shift.reshape(B, 1, D), scale.reshape(B, 1, D), w, col_scale.reshape(1, N))


def _norm_router_kernel(x_ref, g_ref, sh_ref, sc_ref, rw_ref, h_ref, aff_ref, *, token_tiles):
    h = _norm_mod(x_ref[0], g_ref[...], sh_ref[0], sc_ref[0])
    if token_tiles:
        n_tok, D = h.shape
        tiles = h_ref.at[0]
        for c in range(D // LANES):
            tiles[pl.ds(c, n_tok, stride=D // LANES), :] = h[:, c * LANES:(c + 1) * LANES]
    else:
        h_ref[0] = h.astype(h_ref.dtype)
    logits = jnp.dot(h, rw_ref[...], preferred_element_type=F32, precision=lax.Precision.HIGHEST)
    m = jnp.max(logits, axis=-1, keepdims=True)
    e = jnp.exp(logits - m)
    aff_ref[0] = e / jnp.sum(e, axis=-1, keepdims=True)


def norm_router(x, g, shift, scale, router_w, tm, token_tiles):
    B, L, D = x.shape
    E = router_w.shape[1]
    tm = min(tm, L)
    if token_tiles:
        sub = D // LANES
        h_spec = pl.BlockSpec((1, tm * sub, LANES), lambda b, i: (b, i, 0))
        h_shape = jax.ShapeDtypeStruct((B, L * sub, LANES), F32)
    else:
        h_spec = pl.BlockSpec((1, tm, D), lambda b, i: (b, i, 0))
        h_shape = jax.ShapeDtypeStruct((B, L, D), BF16)
    return pl.pallas_call(
        functools.partial(_norm_router_kernel, token_tiles=token_tiles),
        grid=(B, L // tm),
        in_specs=[pl.BlockSpec((1, tm, D), lambda b, i: (b, i, 0)),
                  pl.BlockSpec((1, D), lambda b, i: (0, 0)),
                  pl.BlockSpec((1, 1, D), lambda b, i: (b, 0, 0)),
                  pl.BlockSpec((1, 1, D), lambda b, i: (b, 0, 0)),
                  pl.BlockSpec((D, E), lambda b, i: (0, 0))],
        out_specs=[h_spec, pl.BlockSpec((1, tm, E), lambda b, i: (b, i, 0))],
        out_shape=[h_shape, jax.ShapeDtypeStruct((B, L, E), F32)],
        compiler_params=_cparams("arbitrary", "arbitrary"),
        name="norm_router",
    )(x, g.reshape(1, D), shift.reshape(B, 1, D), scale.reshape(B, 1, D), router_w)


def _proj_residual_kernel(y_ref, w_ref, x_ref, gate_ref, o_ref):
    mix = jnp.dot(y_ref[0].astype(BF16), w_ref[...], preferred_element_type=F32)
    o_ref[0] = x_ref[0] + gate_ref[0] * mix


def proj_residual(y, w, x, gate, tm):
    B, L, K = y.shape
    D = w.shape[1]
    tm = min(tm, L)
    return pl.pallas_call(
        _proj_residual_kernel,
        grid=(B, L // tm),
        in_specs=[pl.BlockSpec((1, tm, K), lambda b, i: (b, i, 0)),
                  pl.BlockSpec((K, D), lambda b, i: (0, 0)),
                  pl.BlockSpec((1, tm, D), lambda b, i: (b, i, 0)),
                  pl.BlockSpec((1, 1, D), lambda b, i: (b, 0, 0))],
        out_specs=pl.BlockSpec((1, tm, D), lambda b, i: (b, i, 0)),
        out_shape=jax.ShapeDtypeStruct((B, L, D), F32),
        compiler_params=_cparams("arbitrary", "arbitrary"),
        name="proj_residual",
    )(y, w, x, gate.reshape(B, 1, D))


def _rms_kernel(x_ref, g_ref, o_ref):
    x = x_ref[0]
    ms = jnp.mean(x * x, axis=-1, keepdims=True)
    o_ref[0] = x * lax.rsqrt(ms + RMS_EPS) * g_ref[...]


def rms_norm_final(x, g, tm=1024):
    B, L, D = x.shape
    return pl.pallas_call(
        _rms_kernel,
        grid=(B, L // tm),
        in_specs=[pl.BlockSpec((1, tm, D), lambda b, i: (b, i, 0)),
                  pl.BlockSpec((1, D), lambda b, i: (0, 0))],
        out_specs=pl.BlockSpec((1, tm, D), lambda b, i: (b, i, 0)),
        out_shape=jax.ShapeDtypeStruct((B, L, D), F32),
        compiler_params=_cparams("arbitrary", "arbitrary"),
        name="rms_final",
    )(x, g.reshape(1, D))


def _two_head_rows(q):
    lane = lax.broadcasted_iota(jnp.int32, q.shape, 1)
    zero = jnp.zeros_like(q)
    return jnp.concatenate([jnp.where(lane < HEAD_DIM, q, zero),
                            jnp.where(lane >= HEAD_DIM, q, zero)], axis=0)


def _merge_two_heads(o):
    n = o.shape[0] // 2
    lane = lax.broadcasted_iota(jnp.int32, (n, LANES), 1)
    return jnp.where(lane < HEAD_DIM, o[:n], o[n:])


_CONTRACT_LAST = (((1,), (1,)), ((), ()))


def _na_key_rows(rq):
    n = rq + NA_WIN_ROWS - 1
    return n + n % 2


def _na_kernel(q_ref, k_ref, v_ref, kc_ref, vc_ref, bias_ref, o_ref, *, rows, rq):
    nq = rq * GRID_W
    nkr = _na_key_rows(rq)
    kwin = nkr * GRID_W
    kc = kc_ref[0]
    vc = vc_ref[0]

    def step(i, carry):
        r = i * rq
        start = jnp.clip(r - NA_WIN_ROWS // 2, 0, rows - nkr)
        variant = r - start
        q0 = pl.multiple_of(r * GRID_W, nq)
        k0 = pl.multiple_of(start * GRID_W, GRID_W)
        q2 = _two_head_rows(q_ref[0, pl.ds(q0, nq), :])
        kw = k_ref[0, pl.ds(k0, kwin), :]
        vw = v_ref[0, pl.ds(k0, kwin), :]
        s_loc = lax.dot_general(q2, kw, _CONTRACT_LAST, preferred_element_type=F32) + bias_ref[0, variant]
        s_ctx = lax.dot_general(q2, kc, _CONTRACT_LAST, preferred_element_type=F32)
        m = jnp.maximum(jnp.max(s_loc, axis=-1, keepdims=True), jnp.max(s_ctx, axis=-1, keepdims=True))
        p_loc = jnp.exp2(s_loc - m)
        p_ctx = jnp.exp2(s_ctx - m)
        denom = jnp.sum(p_loc, axis=-1, keepdims=True) + jnp.sum(p_ctx, axis=-1, keepdims=True)
        o = (jnp.dot(p_loc.astype(BF16), vw, preferred_element_type=F32)
             + jnp.dot(p_ctx.astype(BF16), vc, preferred_element_type=F32))
        o = o / denom
        o_ref[0, pl.ds(q0, nq), :] = _merge_two_heads(o).astype(o_ref.dtype)
        return carry

    lax.fori_loop(0, rows // rq, step, 0, unroll=2)


def _na_bias_tables(rpb, rows, rq):
    W = GRID_W
    nkr = _na_key_rows(rq)
    n_var = nkr - rq + 1
    n_ro, n_co = 2 * NA_WIN_ROWS - 1, 2 * NA_WIN_COLS - 1
    cols = np.arange(W)
    col_start = np.clip(cols - NA_WIN_COLS // 2, 0, W - NA_WIN_COLS)
    row_sel = np.zeros((n_var, rq, nkr, n_ro), np.float32)
    for var in range(n_var):
        r = var if var <= NA_WIN_ROWS // 2 else rows - nkr + var
        start = r - var
        for dr in range(rq):
            q_row = r + dr
            if q_row >= rows:
                continue
            rs = int(np.clip(q_row - NA_WIN_ROWS // 2, 0, rows - NA_WIN_ROWS))
            for kr in range(nkr):
                key_row = start + kr
                if rs <= key_row < rs + NA_WIN_ROWS:
                    row_sel[var, dr, kr, key_row - q_row + NA_WIN_ROWS - 1] = 1.0
    col_sel = np.zeros((n_co, W, W), np.float32)
    for q in range(W):
        for kc in range(col_start[q], col_start[q] + NA_WIN_COLS):
            col_sel[kc - q + NA_WIN_COLS - 1, q, kc] = 1.0
    inside = (row_sel.sum(-1)[:, :, :, None, None] * col_sel.sum(0)[None, None, None]) > 0
    hp = lax.Precision.HIGHEST
    by_row = jnp.einsum('vdko,hoc->hvdkc', row_sel, rpb.astype(F32), precision=hp)
    tab = jnp.einsum('hvdkc,cqn->hvdkqn', by_row, col_sel, precision=hp)
    tab = jnp.where(inside[None], tab * LOG2E, NEG_INF)
    tab = tab.reshape(HEAD_PAIRS, 2, n_var, rq, nkr, W, W).transpose(0, 2, 1, 3, 5, 4, 6)
    return tab.reshape(HEAD_PAIRS, n_var, 2 * rq * W, nkr * W)


def na_attention(p, pc, kc_off, rpb):
    B, L, _ = p.shape
    Lc = pc.shape[1]
    rows = L // GRID_W
    rq = NA_Q_ROWS
    bias = _na_bias_tables(rpb, rows, rq)
    nkr = _na_key_rows(rq)
    kern = functools.partial(_na_kernel, rows=rows, rq=rq)
    return pl.pallas_call(
        kern,
        grid=(B, HEAD_PAIRS),
        in_specs=[pl.BlockSpec((1, L, LANES), lambda b, j: (b, 0, j)),
                  pl.BlockSpec((1, L, LANES), lambda b, j: (b, 0, HEAD_PAIRS + j)),
                  pl.BlockSpec((1, L, LANES), lambda b, j: (b, 0, 2 * HEAD_PAIRS + j)),
                  pl.BlockSpec((1, Lc, LANES), lambda b, j: (b, 0, kc_off + j)),
                  pl.BlockSpec((1, Lc, LANES), lambda b, j: (b, 0, kc_off + HEAD_PAIRS + j)),
                  pl.BlockSpec((1, nkr - rq + 1, 2 * rq * GRID_W, nkr * GRID_W), lambda b, j: (j, 0, 0, 0))],
        out_specs=pl.BlockSpec((1, L, LANES), lambda b, j: (b, 0, j)),
        out_shape=jax.ShapeDtypeStruct((B, L, NA_WIDTH), BF16),
        compiler_params=_cparams("arbitrary", "arbitrary"),
        name="na_attention",
    )(p, p, p, pc, pc, bias)


def _ctx_attn_kernel(q_ref, k_ref, v_ref, o_ref):
    q2 = _two_head_rows(q_ref[0])
    s = lax.dot_general(q2, k_ref[0], _CONTRACT_LAST, preferred_element_type=F32)
    m = jnp.max(s, axis=-1, keepdims=True)
    e = jnp.exp2(s - m)
    o = jnp.dot(e.astype(BF16), v_ref[0], preferred_element_type=F32) / jnp.sum(e, axis=-1, keepdims=True)
    o_ref[0] = _merge_two_heads(o).astype(o_ref.dtype)


def ctx_attention(pc):
    B, Lc, _ = pc.shape
    return pl.pallas_call(
        _ctx_attn_kernel,
        grid=(B, HEAD_PAIRS),
        in_specs=[pl.BlockSpec((1, Lc, LANES), lambda b, j: (b, 0, j)),
                  pl.BlockSpec((1, Lc, LANES), lambda b, j: (b, 0, HEAD_PAIRS + j)),
                  pl.BlockSpec((1, Lc, LANES), lambda b, j: (b, 0, 2 * HEAD_PAIRS + j))],
        out_specs=pl.BlockSpec((1, Lc, LANES), lambda b, j: (b, 0, j)),
        out_shape=jax.ShapeDtypeStruct((B, Lc, NA_WIDTH), BF16),
        compiler_params=_cparams("arbitrary", "arbitrary"),
        name="ctx_attention",
    )(pc, pc, pc)


def _expert_ffn_kernel(x_ref, gate_ref, wg_ref, wu_ref, wd_ref, o_ref, *, fchunk):
    o_ref[0, 0] = _swiglu(x_ref[0, 0], wg_ref, wu_ref, wd_ref, fchunk) * gate_ref[0, 0]


def expert_ffn(xe, gate, wg, wu, wd, fchunk=512):
    E, B, cap, D = xe.shape
    F = wg.shape[2]
    kern = functools.partial(_expert_ffn_kernel, fchunk=fchunk)
    return pl.pallas_call(
        kern,
        grid=(E, B),
        in_specs=[pl.BlockSpec((1, 1, cap, D), lambda e, b: (e, b, 0, 0)),
                  pl.BlockSpec((1, 1, cap, 1), lambda e, b: (e, b, 0, 0)),
                  pl.BlockSpec((1, D, F), lambda e, b: (e, 0, 0)),
                  pl.BlockSpec((1, D, F), lambda e, b: (e, 0, 0)),
                  pl.BlockSpec((1, F, D), lambda e, b: (e, 0, 0))],
        out_specs=pl.BlockSpec((1, 1, cap, D), lambda e, b: (e, b, 0, 0)),
        out_shape=jax.ShapeDtypeStruct((E, B, cap, D), F32),
        compiler_params=_cparams("arbitrary", "arbitrary"),
        name="expert_ffn",
    )(xe, gate, wg, wu, wd)


def _swiglu(x, wg_ref, wu_ref, wd_ref, fchunk):
    F = wg_ref.shape[2]
    acc = jnp.zeros((x.shape[0], wd_ref.shape[2]), F32)
    for f0 in range(0, F, fchunk):
        g = jnp.dot(x, wg_ref[0, :, f0:f0 + fchunk], preferred_element_type=F32)
        u = jnp.dot(x, wu_ref[0, :, f0:f0 + fchunk], preferred_element_type=F32)
        hid = (g * jax.nn.sigmoid(g) * u).astype(BF16)
        acc = acc + jnp.dot(hid, wd_ref[0, f0:f0 + fchunk, :], preferred_element_type=F32)
    return acc


def _gather_rows_copy(h_hbm, xbuf, sem, slot, row, j, sub):
    src = h_hbm.at[pl.ds(pl.multiple_of(row, sub), sub)]
    dst = xbuf.at[slot, pl.ds(pl.multiple_of(j * sub, sub), sub)]
    return pltpu.make_async_copy(src, dst, sem.at[slot])


def _expert_ffn_gather_kernel(rows_ref, rows_next_ref, gate_ref, h_hbm, wg_ref, wu_ref, wd_ref, o_ref,
                              xbuf, xb_ref, acc_ref, sem, *, fchunk):
    sub = xb_ref.shape[1] // LANES
    cap = xbuf.shape[1] // sub
    step = pl.program_id(0) * pl.num_programs(1) + pl.program_id(1)
    nsteps = pl.num_programs(0) * pl.num_programs(1)
    slot = step % 2

    def drain(s):
        pltpu.make_async_copy(h_hbm.at[pl.ds(0, cap * sub)], xbuf.at[s], sem.at[s]).wait()

    @pl.when(step == 0)
    def _():
        def body(j, carry):
            _gather_rows_copy(h_hbm, xbuf, sem, 0, rows_ref[0, 0, j], j, sub).start()
            return carry
        lax.fori_loop(0, cap, body, 0)

    drain(slot)
    for c in range(sub):
        xb_ref[:, c * LANES:(c + 1) * LANES] = xbuf[slot, pl.ds(c, cap, stride=sub), :].astype(BF16)
    acc_ref[...] = jnp.zeros_like(acc_ref)
    n_chunks = wg_ref.shape[2] // fchunk
    n_issuing = max(n_chunks // 2, 1)
    rows_per_chunk = cap // n_issuing

    def chunk(k, carry, issue):
        if issue:
            for jj in range(rows_per_chunk):
                j = k * rows_per_chunk + jj
                _gather_rows_copy(h_hbm, xbuf, sem, 1 - slot, rows_next_ref[0, 0, j], j, sub).start()
        f0 = pl.multiple_of(k * fchunk, fchunk)
        x = xb_ref[...]
        g = jnp.dot(x, wg_ref[0, :, pl.ds(f0, fchunk)], preferred_element_type=F32)
        u = jnp.dot(x, wu_ref[0, :, pl.ds(f0, fchunk)], preferred_element_type=F32)
        hid = (g * jax.nn.sigmoid(g) * u).astype(BF16)
        acc_ref[...] += jnp.dot(hid, wd_ref[0, pl.ds(f0, fchunk), :], preferred_element_type=F32)
        return carry
    lax.fori_loop(0, n_issuing, functools.partial(chunk, issue=True), 0)
    lax.fori_loop(n_issuing, n_chunks, functools.partial(chunk, issue=False), 0)
    o_ref[0, 0] = (acc_ref[...] * gate_ref[0, 0]).astype(o_ref.dtype)

    @pl.when(step == nsteps - 1)
    def _():
        drain(1 - slot)


def expert_ffn_gather(h_rows, rows, gate, wg, wu, wd, fchunk=512):
    E, B, cap = rows.shape
    D = wg.shape[1]
    sub = D // LANES
    rows = rows * sub
    F = wg.shape[2]

    rows = rows.reshape(E * B, 1, cap)
    fchunk = min(fchunk, F)

    def next_group(e, b):
        return (jnp.minimum(e * B + b + 1, E * B - 1), 0, 0)

    kern = functools.partial(_expert_ffn_gather_kernel, fchunk=fchunk)
    return pl.pallas_call(
        kern,
        grid=(E, B),
        in_specs=[pl.BlockSpec((1, 1, cap), lambda e, b: (e * B + b, 0, 0), memory_space=pltpu.SMEM),
                  pl.BlockSpec((1, 1, cap), next_group, memory_space=pltpu.SMEM),
                  pl.BlockSpec((1, 1, cap, 1), lambda e, b: (e, b, 0, 0)),
                  pl.BlockSpec(memory_space=pl.ANY),
                  pl.BlockSpec((1, D, F), lambda e, b: (e, 0, 0)),
                  pl.BlockSpec((1, D, F), lambda e, b: (e, 0, 0)),
                  pl.BlockSpec((1, F, D), lambda e, b: (e, 0, 0))],
        out_specs=pl.BlockSpec((1, 1, cap, D), lambda e, b: (e, b, 0, 0)),
        out_shape=jax.ShapeDtypeStruct((E, B, cap, D), BF16),
        scratch_shapes=[pltpu.VMEM((2, cap * sub, LANES), F32), pltpu.VMEM((cap, D), BF16), pltpu.VMEM((cap, D), F32),
                        pltpu.SemaphoreType.DMA((2,))],
        compiler_params=_cparams("arbitrary", "arbitrary"),
        name="expert_ffn_gather",
    )(rows, rows, gate, h_rows, wg, wu, wd)


COMBINE_TOKENS = 512
COMBINE_WINDOW = 256
COMBINE_CHUNK = LANES


def _window_copy(ye_hbm, win, sem, e, b, w0, slot):
    return pltpu.make_async_copy(ye_hbm.at[e, b, pl.ds(w0, COMBINE_WINDOW)], win.at[slot], sem.at[slot])


def _moe_combine_kernel(starts_ref, tok_ref, x_ref, gate_ref, ye_hbm, o_ref, acc_ref, win, sem):
    b = pl.program_id(0)
    i = pl.program_id(1)
    n_exp = tok_ref.shape[0]
    cap = ye_hbm.shape[2]
    T = x_ref.shape[1]
    tile0 = i * T
    tok_iota = lax.broadcasted_iota(jnp.int32, (T, COMBINE_CHUNK), 0) + tile0

    nt = pl.num_programs(1)
    step = b * nt + i
    parity = step % 2

    def first_window(e, bb=b, ii=i):
        s = starts_ref[bb, e, ii]
        return jnp.minimum((s // COMBINE_CHUNK) * COMBINE_CHUNK, cap - COMBINE_WINDOW)

    def fetch_first_windows(bb, ii, par):
        for e in range(n_exp):
            _window_copy(ye_hbm, win, sem, e, bb, first_window(e, bb, ii), par * n_exp + e).start()

    def one_hot(e, w0, first_row=None):
        k = w0 // COMBINE_CHUNK
        hot = []
        for c in range(COMBINE_WINDOW // COMBINE_CHUNK):
            match = tok_iota == tok_ref[e, 0, pl.ds(k + c, 1), :]
            if first_row is not None:
                row = lax.broadcasted_iota(jnp.int32, match.shape, 1) + (w0 + c * COMBINE_CHUNK)
                match = match & (row >= first_row)
            hot.append(match.astype(BF16))
        return hot

    @pl.when(step == 0)
    def _():
        fetch_first_windows(b, i, 0)

    @pl.when(step + 1 < pl.num_programs(0) * nt)
    def _():
        fetch_first_windows((step + 1) // nt, (step + 1) % nt, 1 - parity)

    firsts = [first_window(e) for e in range(n_exp)]
    hot = []
    for e in range(n_exp):
        hot += one_hot(e, firsts[e])
    for e in range(n_exp):
        _window_copy(ye_hbm, win, sem, e, b, firsts[e], parity * n_exp + e).wait()
    rows_all = win[pl.ds(parity * n_exp, n_exp)].reshape(n_exp * COMBINE_WINDOW, win.shape[2])
    acc_ref[...] = jnp.dot(jnp.concatenate(hot, axis=1), rows_all, preferred_element_type=F32)

    def per_expert(e, carry):
        w0 = first_window(e)
        end = starts_ref[b, e, i + 1]
        n_more = jnp.maximum(end - (w0 + COMBINE_WINDOW) + COMBINE_WINDOW - 1, 0) // COMBINE_WINDOW

        def more(m, c):
            first_row = w0 + (m + 1) * COMBINE_WINDOW
            w = jnp.minimum(first_row, cap - COMBINE_WINDOW)
            cp = _window_copy(ye_hbm, win, sem, e, b, w, 2 * n_exp)
            cp.start()
            cp.wait()
            sel = jnp.concatenate(one_hot(e, w, first_row), axis=1)
            acc_ref[...] += jnp.dot(sel, win[2 * n_exp], preferred_element_type=F32)
            return c
        lax.fori_loop(0, n_more, more, 0)
        return carry

    lax.fori_loop(0, n_exp, per_expert, 0)
    o_ref[0] = x_ref[0] + gate_ref[0] * acc_ref[...]


def moe_combine(x, gate_res, ye, tok):
    B, n, D = x.shape
    E, _, cap, _ = ye.shape
    T = COMBINE_TOKENS
    nt = n // T
    bounds = jnp.arange(nt + 1, dtype=jnp.int32) * T
    starts = jnp.sum(tok[:, :, None, :] < bounds[None, None, :, None], axis=-1, dtype=jnp.int32)
    starts = jnp.transpose(starts, (1, 0, 2))
    tok4 = tok.reshape(E, B, cap // COMBINE_CHUNK, COMBINE_CHUNK)
    grid_spec = pltpu.PrefetchScalarGridSpec(
        num_scalar_prefetch=1,
        grid=(B, nt),
        in_specs=[pl.BlockSpec((E, 1, cap // COMBINE_CHUNK, COMBINE_CHUNK), lambda b, i, s: (0, b, 0, 0)),
                  pl.BlockSpec((1, T, D), lambda b, i, s: (b, i, 0)),
                  pl.BlockSpec((1, 1, D), lambda b, i, s: (b, 0, 0)),
                  pl.BlockSpec(memory_space=pl.ANY)],
        out_specs=pl.BlockSpec((1, T, D), lambda b, i, s: (b, i, 0)),
        scratch_shapes=[pltpu.VMEM((T, D), F32), pltpu.VMEM((2 * E + 1, COMBINE_WINDOW, D), BF16),
                        pltpu.SemaphoreType.DMA((2 * E + 1,))])
    return pl.pallas_call(
        _moe_combine_kernel,
        grid_spec=grid_spec,
        out_shape=jax.ShapeDtypeStruct((B, n, D), F32),
        compiler_params=_cparams("arbitrary", "arbitrary"),
        name="moe_combine",
    )(starts, tok4, x, gate_res.reshape(B, 1, D), ye)


def expert_choice_ffn(x, g, shift, scale, gate_res, router_w, wg, wu, wd, tm):
    B, n, D = x.shape
    E = N_EXPERTS
    cap = EC_CAPACITY_FACTOR * n // E
    fused = cap % COMBINE_WINDOW == 0 and cap > COMBINE_WINDOW and n % COMBINE_TOKENS == 0
    h, aff = norm_router(x, g, shift, scale, router_w, tm, fused)
    gate, idx = lax.top_k(jnp.transpose(aff, (2, 0, 1)), cap)
    bidx = jnp.arange(B, dtype=idx.dtype)[None, :, None]
    if fused:
        idx, gate = lax.sort((idx, gate), dimension=2, num_keys=1)
        ye = expert_ffn_gather(h.reshape(-1, LANES), idx + bidx * n, gate[..., None], wg, wu, wd)
        return moe_combine(x, gate_res, ye, idx)
    xe = h[bidx, idx].reshape(E, 1, B * cap, D)
    ye = expert_ffn(xe, gate.reshape(E, 1, B * cap, 1), wg, wu, wd)
    flat = (idx + bidx * n).reshape(-1)
    moe = jnp.zeros((B * n, D), F32).at[flat].add(ye.reshape(-1, D)).reshape(B, n, D)
    return x + gate_res[:, None, :] * moe


DFT_MINOR = 128
DFT_PITCH_PAD = 8
DFT_QB = 64
DFT_UNROLL = 16


def _dft_mats(n_major, a_used):
    N = n_major * DFT_MINOR
    q = np.arange(n_major)
    a = np.arange(a_used)
    b = np.arange(DFT_MINOR)
    ang1 = -2 * np.pi * np.outer(q, a) / n_major
    f1 = np.concatenate([np.cos(ang1), np.sin(ang1)], 0)
    ang2 = -2 * np.pi * np.outer(b, b) / DFT_MINOR
    c2, s2 = np.cos(ang2), np.sin(ang2)
    f2 = np.block([[c2, -s2], [s2, c2]])
    f2c = np.block([[c2, s2], [-s2, c2]])
    lane1 = np.ones((1, 1, LANES))

    def tw(hi, lo):
        ang = -2 * np.pi * np.outer(hi, lo) / N
        return (np.cos(ang)[:, :, None] * lane1, np.sin(ang)[:, :, None] * lane1)

    t1r, t1i = tw(8 * np.arange(DFT_MINOR // 8), q)
    t0r, t0i = tw(np.arange(8), q)
    u1r, u1i = tw(8 * np.arange(n_major // 8), b)
    u0r, u0i = tw(np.arange(8), b)
    f32 = lambda *xs: [np.asarray(x, np.float32) for x in xs]
    return dict(f1=f1, f2=f2, f2c=f2c, tw_fwd=f32(t1r, t1i, t0r, t0i), tw_inv=f32(u1r, u1i, u0r, u0i))


def _cmul(ar, ai, br, bi):
    return ar * br - ai * bi, ar * bi + ai * br


def _dft_fwd_kernel(*refs, n_major, a_used, mode, precision):
    x_ref, f1_ref, f2_ref, t1r, t1i, t0r, t0i = refs[:7]
    rest = refs[7:]
    if mode == "mul":
        hr_ref, hi_ref, or_ref, oi_ref, sr, si = rest
    elif mode == "chanmix":
        mc_ref, ms_ref, or_ref, sr, si = rest
    else:
        or_ref, oi_ref, sr, si = rest
    pitch = n_major + DFT_PITCH_PAD
    qb = pl.program_id(2)
    mm_dtype = f1_ref.dtype

    @pl.when(qb == 0)
    def _stage1():
        def slab(b, carry):
            y = jnp.dot(f1_ref[...], x_ref[0, b].astype(mm_dtype), preferred_element_type=F32,
                        precision=precision)
            twr, twi = _cmul(t1r[b // 8], t1i[b // 8], t0r[b % 8], t0i[b % 8])
            yr, yi = _cmul(y[:n_major], y[n_major:], twr, twi)
            row = pl.multiple_of(b * pitch, 8)
            sr[pl.ds(row, n_major), :] = yr
            si[pl.ds(row, n_major), :] = yi
            return carry
        lax.fori_loop(0, DFT_MINOR, slab, 0, unroll=DFT_UNROLL)

    def freq(j, carry):
        q = qb * DFT_QB + j
        g = jnp.concatenate([sr[pl.ds(q, DFT_MINOR, stride=pitch), :],
                             si[pl.ds(q, DFT_MINOR, stride=pitch), :]], axis=0).astype(mm_dtype)
        xf = jnp.dot(f2_ref[...], g, preferred_element_type=F32, precision=precision)
        xr, xi = xf[:DFT_MINOR], xf[DFT_MINOR:]
        if mode == "mul":
            xr, xi = _cmul(xr, xi, hr_ref[0, j].astype(F32), hi_ref[0, j].astype(F32))
            or_ref[0, j] = xr.astype(or_ref.dtype)
            oi_ref[0, j] = xi.astype(oi_ref.dtype)
        elif mode == "chanmix":
            or_ref[0, j] = (jnp.dot(xr.astype(mm_dtype), mc_ref[...], preferred_element_type=F32)
                            + jnp.dot(xi.astype(mm_dtype), ms_ref[...], preferred_element_type=F32))
        else:
            or_ref[0, j] = xr
            oi_ref[0, j] = xi
        return carry
    lax.fori_loop(0, DFT_QB, freq, 0, unroll=DFT_UNROLL)


def dft_forward(xp, n_major, mode="spectrum", h=None, chan=None, precise=False):
    Bb, _, a_used, C = xp.shape
    mats = _dft_mats(n_major, a_used)
    mm_dtype = F32 if precise else BF16
    precision = lax.Precision.HIGHEST if precise else None
    pitch = n_major + DFT_PITCH_PAD
    const = lambda shape: pl.BlockSpec(shape, lambda bb, cb, qb: (0,) * len(shape))
    spec_blk = pl.BlockSpec((1, DFT_QB, DFT_MINOR, LANES), lambda bb, cb, qb: (bb, qb, 0, cb))
    args = [xp, jnp.asarray(mats["f1"], mm_dtype), jnp.asarray(mats["f2"], mm_dtype)] + mats["tw_fwd"]
    in_specs = [pl.BlockSpec((1, DFT_MINOR, a_used, LANES), lambda bb, cb, qb: (bb, 0, 0, cb)),
                const((2 * n_major, a_used)), const((2 * DFT_MINOR, 2 * DFT_MINOR)),
                const((DFT_MINOR // 8, n_major, LANES)), const((DFT_MINOR // 8, n_major, LANES)),
                const((8, n_major, LANES)), const((8, n_major, LANES))]
    spec_shape = (Bb, n_major, DFT_MINOR, C)
    if mode == "mul":
        args += [h[0], h[1]]
        hblk = pl.BlockSpec((1, DFT_QB, DFT_MINOR, LANES), lambda bb, cb, qb: (0, qb, 0, cb))
        in_specs += [hblk, hblk]
        out_specs = [spec_blk, spec_blk]
        out_shape = [jax.ShapeDtypeStruct(spec_shape, BF16)] * 2
    elif mode == "chanmix":
        args += [chan[0].astype(mm_dtype), chan[1].astype(mm_dtype)]
        in_specs += [const((LANES, LANES)), const((LANES, LANES))]
        out_specs = spec_blk
        out_shape = jax.ShapeDtypeStruct(spec_shape, F32)
    else:
        out_specs = [spec_blk, spec_blk]
        out_shape = [jax.ShapeDtypeStruct(spec_shape, F32)] * 2
    kern = functools.partial(_dft_fwd_kernel, n_major=n_major, a_used=a_used, mode=mode, precision=precision)
    return pl.pallas_call(
        kern,
        grid=(Bb, C // LANES, n_major // DFT_QB),
        in_specs=in_specs,
        out_specs=out_specs,
        out_shape=out_shape,
        scratch_shapes=[pltpu.VMEM((DFT_MINOR * pitch, LANES), F32)] * 2,
        compiler_params=_cparams("arbitrary", "arbitrary", "arbitrary"),
        name="dft_forward_" + mode,
    )(*args)


def _dft_inv_kernel(gr_ref, gi_ref, f2c_ref, f1c_ref, u1r, u1i, u0r, u0i, o_ref, sr, si, *, n_major, a_out):
    pitch = DFT_MINOR + DFT_PITCH_PAD
    qb = pl.program_id(2)

    def freq(j, carry):
        q = qb * DFT_QB + j
        g = jnp.concatenate([gr_ref[0, j], gi_ref[0, j]], axis=0)
        t = jnp.dot(f2c_ref[...], g, preferred_element_type=F32)
        twr, twi = _cmul(u1r[q // 8], u1i[q // 8], u0r[q % 8], u0i[q % 8])
        tr, ti = _cmul(t[:DFT_MINOR], t[DFT_MINOR:], twr, -twi)
        row = pl.multiple_of(q * pitch, 8)
        sr[pl.ds(row, DFT_MINOR), :] = tr
        si[pl.ds(row, DFT_MINOR), :] = ti
        return carry
    lax.fori_loop(0, DFT_QB, freq, 0, unroll=DFT_UNROLL)

    @pl.when(qb == pl.num_programs(2) - 1)
    def _stage2():
        def slab(b, carry):
            g = jnp.concatenate([sr[pl.ds(b, n_major, stride=pitch), :],
                                 si[pl.ds(b, n_major, stride=pitch), :]], axis=0).astype(BF16)
            o_ref[0, b] = jnp.dot(f1c_ref[...], g, preferred_element_type=F32)
            return carry
        lax.fori_loop(0, DFT_MINOR, slab, 0, unroll=DFT_UNROLL)


def dft_inverse_real(gr, gi, a_out):
    Bb, n_major, _, C = gr.shape
    N = n_major * DFT_MINOR
    mats = _dft_mats(n_major, a_out)
    ang = 2 * np.pi * np.outer(np.arange(a_out), np.arange(n_major)) / n_major
    f1c = np.concatenate([np.cos(ang), -np.sin(ang)], 1) / N
    pitch = DFT_MINOR + DFT_PITCH_PAD
    const = lambda shape: pl.BlockSpec(shape, lambda bb, cb, qb: (0,) * len(shape))
    gblk = pl.BlockSpec((1, DFT_QB, DFT_MINOR, LANES), lambda bb, cb, qb: (bb, qb, 0, cb))
    kern = functools.partial(_dft_inv_kernel, n_major=n_major, a_out=a_out)
    return pl.pallas_call(
        kern,
        grid=(Bb, C // LANES, n_major // DFT_QB),
        in_specs=[gblk, gblk, const((2 * DFT_MINOR, 2 * DFT_MINOR)), const((a_out, 2 * n_major)),
                  const((n_major // 8, DFT_MINOR, LANES)), const((n_major // 8, DFT_MINOR, LANES)),
                  const((8, DFT_MINOR, LANES)), const((8, DFT_MINOR, LANES))],
        out_specs=pl.BlockSpec((1, DFT_MINOR, a_out, LANES), lambda bb, cb, qb: (bb, 0, 0, cb)),
        out_shape=jax.ShapeDtypeStruct((Bb, DFT_MINOR, a_out, C), F32),
        scratch_shapes=[pltpu.VMEM((n_major * pitch, LANES), F32)] * 2,
        compiler_params=_cparams("arbitrary", "arbitrary", "arbitrary"),
        name="dft_inverse",
    )(gr, gi, jnp.asarray(mats["f2c"], BF16), jnp.asarray(f1c, BF16), *mats["tw_inv"])


def _to_slabs(x, a_used):
    Bb, _, C = x.shape
    return x.reshape(Bb, a_used, DFT_MINOR, C).transpose(0, 2, 1, 3)


def _from_slabs(xs):
    Bb, m, a, C = xs.shape
    return xs.transpose(0, 2, 1, 3).reshape(Bb, a * m, C)


def _channel_dft_mats(L):
    c = np.arange(FN_GROUP_DIM)
    ang = -2 * np.pi * np.outer(c, c) / FN_GROUP_DIM
    eye = np.eye(LANES // FN_GROUP_DIM)
    norm = 1.0 / math.sqrt(L * FN_GROUP_DIM)
    return (jnp.asarray(np.kron(eye, np.cos(ang)) * norm, F32),
            jnp.asarray(np.kron(eye, -np.sin(ang)) * norm, F32))


def fourier_mix_pallas(u):
    B, L, C = u.shape
    n_major = L // DFT_MINOR
    y = dft_forward(_to_slabs(u, n_major), n_major, mode="chanmix", chan=_channel_dft_mats(L))
    return y.transpose(0, 2, 1, 3).reshape(B, L, C)


def long_conv_pallas(vx, h_fwd, h_bwd):
    B, L, D = vx.shape
    n_major = 2 * L // DFT_MINOR
    taps = _to_slabs(jnp.stack([h_fwd, h_bwd]), n_major // 2)
    tr, ti = dft_forward(taps, n_major, mode="spectrum")
    h = ((tr[0:1] + tr[1:2]).astype(BF16), (ti[0:1] - ti[1:2]).astype(BF16))
    gr, gi = dft_forward(_to_slabs(vx, n_major // 2), n_major, mode="mul", h=h)
    return _from_slabs(dft_inverse_real(gr, gi, n_major // 2))


def fourier_mix(u):
    B, L, _ = u.shape
    if L % (DFT_MINOR * DFT_QB) == 0:
        return fourier_mix_pallas(u.astype(F32))
    ug = u.astype(F32).reshape(B, L, FN_GROUPS, FN_GROUP_DIM)
    y = jnp.fft.fft2(ug, axes=(1, 3), norm='ortho').real
    return y.reshape(B, L, FN_WIDTH)


def hyena_filters(L, w1, b1, freq, w2, b2, w3):
    hp = lax.Precision.HIGHEST
    t = jnp.linspace(0.0, 1.0, L, dtype=F32)[:, None]
    w = 2 * math.pi * jnp.arange(L, dtype=F32)[:, None] / L
    bands = jnp.linspace(1e-4, HY_BANDS - 1, HY_BANDS, dtype=F32)[None, :]
    z = jnp.concatenate([t, jnp.cos(bands * w), -jnp.sin(bands * w)], axis=-1)
    hid = jnp.sin(freq * (jnp.dot(z, w1, precision=hp) + b1))
    hid = jnp.sin(freq * (jnp.dot(hid, w2, precision=hp) + b2))
    h = jnp.dot(hid, w3, precision=hp).reshape(L, 2, D_MODEL)
    deltas = jnp.abs(jnp.linspace(HY_MIN_DECAY, HY_MAX_DECAY, D_MODEL, dtype=F32))
    window = jnp.exp(-t * deltas[None, :])
    h = h * window[:, None, :]
    h_fwd = h[:, 0]
    h_bwd = h[:, 1] * (jnp.arange(L) > 0)[:, None]
    norm = jnp.sum(jnp.abs(h_fwd), axis=0, keepdims=True) + jnp.sum(jnp.abs(h_bwd), axis=0, keepdims=True)
    return h_fwd / norm, h_bwd / norm


HALO_ROWS = 8


def _hyena_pre_kernel(u_ref, prev_ref, next_ref, w_ref, b_ref, x0_ref, vx_ref):
    i = pl.program_id(1)
    u = u_ref[0]
    T = u.shape[0]
    D = x0_ref.shape[2]
    row = lax.broadcasted_iota(jnp.int32, (T, 1), 0)
    before = jnp.where(i > 0, prev_ref[0, HALO_ROWS - 1:HALO_ROWS, :], 0.0)
    after = jnp.where(i < pl.num_programs(1) - 1, next_ref[0, 0:1, :], 0.0)
    u_m1 = jnp.where(row == 0, before, pltpu.roll(u, 1, axis=0))
    u_p1 = jnp.where(row == T - 1, after, pltpu.roll(u, T - 1, axis=0))
    y = u_m1 * w_ref[0:1, :] + u * w_ref[1:2, :] + u_p1 * w_ref[2:3, :] + b_ref[...]
    x0_ref[0] = y[:, :D]
    vx_ref[0] = y[:, 2 * D:] * y[:, D:2 * D]


def hyena_pre(u, conv_w, conv_b, tm=256):
    B, L, D3 = u.shape
    D = D3 // 3
    tm = min(tm, L)
    hb = tm // HALO_ROWS
    last = L // HALO_ROWS - 1
    return pl.pallas_call(
        _hyena_pre_kernel,
        grid=(B, L // tm),
        in_specs=[pl.BlockSpec((1, tm, D3), lambda b, i: (b, i, 0)),
                  pl.BlockSpec((1, HALO_ROWS, D3), lambda b, i: (b, jnp.maximum(i * hb - 1, 0), 0)),
                  pl.BlockSpec((1, HALO_ROWS, D3), lambda b, i: (b, jnp.minimum((i + 1) * hb, last), 0)),
                  pl.BlockSpec((HY_SHORT, D3), lambda b, i: (0, 0)),
                  pl.BlockSpec((1, D3), lambda b, i: (0, 0))],
        out_specs=[pl.BlockSpec((1, tm, D), lambda b, i: (b, i, 0)),
                   pl.BlockSpec((1, tm, D), lambda b, i: (b, i, 0))],
        out_shape=[jax.ShapeDtypeStruct((B, L, D), F32)] * 2,
        compiler_params=_cparams("arbitrary", "arbitrary"),
        name="hyena_pre",
    )(u, u, u, conv_w, conv_b.reshape(1, D3))


def _hyena_out_kernel(y_ref, vx_ref, x0_ref, skip_ref, w_ref, x_ref, gate_ref, o_ref):
    z = x0_ref[0] * (y_ref[0] + vx_ref[0] * skip_ref[...])
    mix = jnp.dot(z.astype(BF16), w_ref[...], preferred_element_type=F32)
    o_ref[0] = x_ref[0] + gate_ref[0] * mix


def hyena_out_residual(y, vx, x0, skip, w, x, gate, tm):
    B, L, D = x.shape
    tm = min(tm, L)
    tile = pl.BlockSpec((1, tm, D), lambda b, i: (b, i, 0))
    return pl.pallas_call(
        _hyena_out_kernel,
        grid=(B, L // tm),
        in_specs=[tile, tile, tile,
                  pl.BlockSpec((1, D), lambda b, i: (0, 0)),
                  pl.BlockSpec((D, D), lambda b, i: (0, 0)),
                  tile,
                  pl.BlockSpec((1, 1, D), lambda b, i: (b, 0, 0))],
        out_specs=tile,
        out_shape=jax.ShapeDtypeStruct((B, L, D), F32),
        compiler_params=_cparams("arbitrary", "arbitrary"),
        name="hyena_out_residual",
    )(y, vx, x0, skip.reshape(1, D), w, x, gate.reshape(B, 1, D))


def hyena_mixer_residual(x, u, conv_w, conv_b, taps, skip, w_out, gate, tm):
    L = u.shape[1]
    x0, vx = hyena_pre(u, conv_w, conv_b)
    h_fwd, h_bwd = taps
    if (2 * L) % (DFT_MINOR * DFT_QB) == 0:
        y = long_conv_pallas(vx, h_fwd, h_bwd)
    else:
        filt = jnp.concatenate([h_fwd, jnp.zeros((1, h_fwd.shape[1]), F32), h_bwd[1:][::-1]], axis=0)
        y = jnp.fft.irfft(jnp.fft.rfft(vx, n=2 * L, axis=1) * jnp.fft.rfft(filt, n=2 * L, axis=0)[None],
                          n=2 * L, axis=1)[:, :L]
    return hyena_out_residual(y, vx, x0, skip, w_out, x, gate, tm)


def kernel(x, c, ctx, c_ctx, mod_w, mod_b, norm_mix_g, norm_ffn_g, mix_w_in, na_rpb, mix_w_out,
           hy_w_in, hy_conv_w, hy_conv_b, hy_f_w1, hy_f_b1, hy_f_freq, hy_f_w2, hy_f_b2, hy_f_w3,
           hy_skip, hy_w_out, router_w, exp_w_gate, exp_w_up, exp_w_down, final_norm_g):
    depth = mod_w.shape[0]
    B, L, D = x.shape
    Lc = ctx.shape[1]
    last_ctx_read = depth - 1 if (depth - 1) % 2 == 0 else depth - 2
    TM = 512

    cs = ctx
    cond = jnp.concatenate([jax.nn.silu(c), jax.nn.silu(c_ctx)[None], jnp.zeros((8 - (B + 1) % 8, D), F32)], 0)

    for layer in range(depth):
        j = layer // 2
        m_all = small_matmul(cond, mod_w[layer], mod_b[layer])
        sh1, sc1, g1, sh2, sc2, g2 = jnp.split(m_all[:B], 6, axis=-1)
        m_ctx = jnp.broadcast_to(m_all[B:B + 1], (B, 6 * D))
        csh1, csc1, cg1, csh2, csc2, cg2 = jnp.split(m_ctx, 6, axis=-1)
        upd = layer < last_ctx_read
        wg = exp_w_gate[layer].astype(BF16)
        wu = exp_w_up[layer].astype(BF16)
        wd = exp_w_down[layer].astype(BF16)

        if layer % 2 == 0:
            w_in = mix_w_in[j].astype(BF16)
            w_out = mix_w_out[j].astype(BF16)
            q_scale = jnp.concatenate([jnp.full((NA_WIDTH,), QK_SCALE, F32),
                                       jnp.ones((w_in.shape[1] - NA_WIDTH,), F32)])
            if upd:
                pc = norm_proj(cs, norm_mix_g[layer], csh1, csc1, w_in, BF16, TM, q_scale)
                kc_off = HEAD_PAIRS
                cmix_in = jnp.concatenate([ctx_attention(pc), fourier_mix(pc[..., 3 * NA_WIDTH:]).astype(BF16)], -1)
            else:
                pc = norm_proj(cs, norm_mix_g[layer], csh1, csc1, w_in[:, NA_WIDTH:3 * NA_WIDTH], BF16, TM)
                kc_off = 0
            p = norm_proj(x, norm_mix_g[layer], sh1, sc1, w_in, BF16, TM, q_scale)
            att = na_attention(p, pc, kc_off, na_rpb[j])
            mix_in = jnp.concatenate([att, fourier_mix(p[..., 3 * NA_WIDTH:]).astype(BF16)], -1)
            x = proj_residual(mix_in, w_out, x, g1, TM)
            if upd:
                cs = proj_residual(cmix_in, w_out, cs, cg1, TM)
        else:
            w_in = hy_w_in[j].astype(BF16)
            w_out = hy_w_out[j].astype(BF16)
            fargs = (hy_f_w1[j], hy_f_b1[j], hy_f_freq[j], hy_f_w2[j], hy_f_b2[j], hy_f_w3[j])
            u = norm_proj(x, norm_mix_g[layer], sh1, sc1, w_in, F32, TM)
            x = hyena_mixer_residual(x, u, hy_conv_w[j], hy_conv_b[j], hyena_filters(L, *fargs), hy_skip[j],
                                     w_out, g1, TM)
            if upd:
                uc = norm_proj(cs, norm_mix_g[layer], csh1, csc1, w_in, F32, TM)
                cs = hyena_mixer_residual(cs, uc, hy_conv_w[j], hy_conv_b[j], hyena_filters(Lc, *fargs),
                                          hy_skip[j], w_out, cg1, TM)

        x = expert_choice_ffn(x, norm_ffn_g[layer], sh2, sc2, g2, router_w[layer], wg, wu, wd, TM)
        if upd:
            cs = expert_choice_ffn(cs, norm_ffn_g[layer], csh2, csc2, cg2, router_w[layer], wg, wu, wd, TM)

    return rms_norm_final(x, final_norm_g)
```
